```python
import math
import jax
import jax.numpy as jnp
from jax import lax
import numpy as np

D_MODEL = 1024
BATCH = 32
SEQ = 2048
DEPTH = 4

N_MIXERS = 3
N_SUB = 3
NORM_EPS = 1e-6
D_FF = 2816
MACARON_W = 0.5
N_A = (DEPTH + 2) // 3
N_B = (DEPTH + 1) // 3
N_C = DEPTH // 3
N_A_VRES = max(N_A - 1, 0)

RW_HEAD = 64
RW_HEADS = D_MODEL // RW_HEAD
RW_DECAY_LORA = max(32, int(round(1.8 * D_MODEL ** 0.5 / 32)) * 32)
RW_AAA_LORA = max(32, int(round(1.8 * D_MODEL ** 0.5 / 32)) * 32)
RW_MV_LORA = max(32, int(round(1.3 * D_MODEL ** 0.5 / 32)) * 32)
RW_GATE_LORA = max(32, int(round(0.6 * D_MODEL ** 0.8 / 32)) * 32)
RW_GN_EPS = 64e-5

MB_D_INNER = 2 * D_MODEL
MB_HEADDIM = 64
MB_HEADS = MB_D_INNER // MB_HEADDIM
MB_GROUPS = 8
MB_STATE = 128
MB_CONV = 4
MB_CHUNK = 128
MB_CONV_DIM = MB_D_INNER + 2 * MB_GROUPS * MB_STATE
MB_IN_DIM = MB_D_INNER + MB_CONV_DIM + MB_HEADS

SW_HEAD_DIM = 64
SW_Q_HEADS = D_MODEL // SW_HEAD_DIM
SW_KV_HEADS = 4
SW_WINDOW = 128
SW_BLOCK = 128
NEG_INF = -1e30

kernel_name = "hybrid_rwkv7_mamba2_swa_macaron_adaln"


def rms_norm(x, g, eps=NORM_EPS):
    xf = x.astype(jnp.float32)
    y = xf * lax.rsqrt(jnp.mean(xf * xf, axis=-1, keepdims=True) + eps)
    return (y * g.astype(jnp.float32)).astype(x.dtype)


def adaln(x, g, shift, scale):
    return rms_norm(x, g) * (1 + scale[:, None, :]) + shift[:, None, :]


def swiglu(h, w_in, w_out):
    a, b = jnp.split(h @ w_in, 2, axis=-1)
    return (jax.nn.silu(a) * b) @ w_out


def token_shift(h):
    return jnp.pad(h, ((0, 0), (1, 0), (0, 0)))[:, :-1]


def rwkv7_time_mix(h, mu, w_rkv, w_o, w0, w1, w2, a0, a1, a2, g1, g2,
                   k_k, k_a, r_k, ln_w, ln_b, v_first, vres):
    f32 = jnp.float32
    bsz, seq, d = h.shape
    H, N = RW_HEADS, RW_HEAD
    dx = token_shift(h) - h
    xr, xw, xk, xv, xa, xg = [h + dx * mu[j] for j in range(6)]
    r = xr @ w_rkv[0]
    k = xk @ w_rkv[1]
    v = xv @ w_rkv[2]
    w = -jax.nn.softplus(-(w0 + jnp.tanh(xw @ w1) @ w2)) - 0.5
    decay = jnp.exp(-jnp.exp(w.astype(f32)))
    a = jax.nn.sigmoid(a0 + (xa @ a1) @ a2)
    if vres is None:
        v_first = v
    else:
        v0, v1, v2 = vres
        v = v + (v_first - v) * jax.nn.sigmoid(v0 + (xv @ v1) @ v2)
    g = jax.nn.sigmoid(xg @ g1) @ g2

    def heads(t):
        return t.reshape(bsz, seq, H, N).astype(f32)

    kk = heads(k * k_k)
    kk = kk / jnp.maximum(jnp.sqrt(jnp.sum(kk * kk, axis=-1, keepdims=True)), 1e-12)
    k = k * (1 + (a - 1) * k_a)
    rh, kh, vh, wh, ah = heads(r), heads(k), heads(v), heads(decay), heads(a)

    def step(state, inp):
        r_t, w_t, k_t, v_t, kk_t, a_t = inp
        sa = jnp.einsum('bhvk,bhk->bhv', state, -kk_t)
        state = (state * w_t[:, :, None, :]
                 + sa[..., None] * (kk_t * a_t)[:, :, None, :]
                 + v_t[..., None] * k_t[:, :, None, :])
        return state, jnp.einsum('bhvk,bhk->bhv', state, r_t)

    def tm(t):
        return jnp.swapaxes(t, 0, 1)

    state0 = jnp.zeros((bsz, H, N, N), f32)
    _, y = lax.scan(step, state0, (tm(rh), tm(wh), tm(kh), tm(vh), tm(kk), tm(ah)))
    y = tm(y)
    mean = jnp.mean(y, axis=-1, keepdims=True)
    var = jnp.mean(jnp.square(y - mean), axis=-1, keepdims=True)
    y = ((y - mean) * lax.rsqrt(var + RW_GN_EPS)).reshape(bsz, seq, d) * ln_w + ln_b
    bonus = jnp.sum(rh * kh * r_k, axis=-1, keepdims=True) * vh
    out = ((y + bonus.reshape(bsz, seq, d)) * g).astype(h.dtype) @ w_o
    return out, v_first


def causal_dwconv(x, w, b):
    k_w, ch = w.shape
    out = lax.conv_general_dilated(x, w[:, None, :].astype(x.dtype), window_strides=(1,),
                                   padding=[(k_w - 1, 0)],
                                   dimension_numbers=('NWC', 'WIO', 'NWC'),
                                   feature_group_count=ch)
    return out + b


def ssd_chunked(x, dt, A, Bm, Cm):
    f32 = jnp.float32
    bsz, seq, H, P = x.shape
    G, N = Bm.shape[2], Bm.shape[3]
    hg = H // G
    L = MB_CHUNK
    nc = seq // L
    X = (x.astype(f32) * dt[..., None]).reshape(bsz, nc, L, G, hg, P)
    dA = (dt * A).reshape(bsz, nc, L, G, hg)
    Bc = Bm.astype(f32).reshape(bsz, nc, L, G, N)
    Cc = Cm.astype(f32).reshape(bsz, nc, L, G, N)
    X, dA, Bc, Cc = [jnp.moveaxis(t, 1, 0) for t in (X, dA, Bc, Cc)]
    mask = jnp.tril(jnp.ones((L, L), dtype=bool))[None, :, :, None, None]

    def step(state, inp):
        Xc, dAc, Bk, Ck = inp
        acum = jnp.cumsum(dAc, axis=1)
        seg = acum[:, :, None] - acum[:, None, :]
        lmat = jnp.exp(jnp.where(mask, seg, -jnp.inf))
        cb = jnp.einsum('blgn,bsgn->blsg', Ck, Bk)
        y_diag = jnp.einsum('blsgh,bsghp->blghp', cb[..., None] * lmat, Xc)
        y_off = jnp.einsum('blgn,bghpn->blghp', Ck, state) * jnp.exp(acum)[..., None]
        decay_end = jnp.exp(acum[:, -1:] - acum)
        new_state = (state * jnp.exp(acum[:, -1])[..., None, None]
                     + jnp.einsum('blgn,blghp->bghpn', Bk, Xc * decay_end[..., None]))
        return new_state, y_diag + y_off

    state0 = jnp.zeros((bsz, G, hg, P, N), f32)
    _, y = lax.scan(step, state0, (X, dA, Bc, Cc))
    return jnp.moveaxis(y, 0, 1).reshape(bsz, seq, H, P)


def mamba2_ssd_mix(h, w_in, conv_w, conv_b, dt_bias, A_log, D_skip, norm_g, w_out):
    f32 = jnp.float32
    bsz, seq, _ = h.shape
    zxbcdt = h @ w_in
    z, xbc, dt = jnp.split(zxbcdt, [MB_D_INNER, MB_D_INNER + MB_CONV_DIM], axis=-1)
    xbc = jax.nn.silu(causal_dwconv(xbc, conv_w, conv_b))
    xs, Bm, Cm = jnp.split(xbc, [MB_D_INNER, MB_D_INNER + MB_GROUPS * MB_STATE], axis=-1)
    dt = jax.nn.softplus((dt + dt_bias).astype(f32))
    A = -jnp.exp(A_log.astype(f32))
    xs = xs.reshape(bsz, seq, MB_HEADS, MB_HEADDIM)
    y = ssd_chunked(xs, dt, A,
                    Bm.reshape(bsz, seq, MB_GROUPS, MB_STATE),
                    Cm.reshape(bsz, seq, MB_GROUPS, MB_STATE))
    y = y + xs.astype(f32) * D_skip.astype(f32)[:, None]
    y = y.reshape(bsz, seq, MB_D_INNER) * jax.nn.silu(z.astype(f32))
    y = rms_norm(y.reshape(bsz, seq, MB_GROUPS, -1), norm_g.reshape(MB_GROUPS, -1))
    return y.reshape(bsz, seq, MB_D_INNER).astype(h.dtype) @ w_out


def swa_sink_attention(h, w_qkv, q_norm, k_norm, sinks, w_o):
    f32 = jnp.float32
    bsz, seq, _ = h.shape
    Hq, Hk, Dh, T = SW_Q_HEADS, SW_KV_HEADS, SW_HEAD_DIM, SW_BLOCK
    G = Hq // Hk
    nb = seq // T
    q, k, v = jnp.split(h @ w_qkv, [Hq * Dh, (Hq + Hk) * Dh], axis=-1)
    q = rms_norm(q.reshape(bsz, seq, Hq, Dh), q_norm).reshape(bsz, nb, T, Hk, G, Dh)
    k = rms_norm(k.reshape(bsz, seq, Hk, Dh), k_norm).reshape(bsz, nb, T, Hk, Dh)
    v = v.reshape(bsz, nb, T, Hk, Dh)

    def with_prev(t):
        prev = jnp.pad(t, ((0, 0), (1, 0), (0, 0), (0, 0), (0, 0)))[:, :-1]
        return jnp.concatenate([prev, t], axis=2)

    kb, vb = with_prev(k), with_prev(v)
    qi = jnp.arange(T)[:, None]
    si = jnp.arange(2 * T)[None, :]
    rel = qi + T - si
    band = (rel >= 0) & (rel < SW_WINDOW)
    mask = band[None] & ((jnp.arange(nb)[:, None, None] > 0) | (si >= T)[None])
    scale = Dh ** -0.5
    sink = sinks.astype(f32).reshape(Hk, G)[None, :, :, None, None]

    def block(args):
        qb, kbb, vbb, mb = args
        s = jnp.einsum('bqkgd,bskd->bkgqs', qb, kbb).astype(f32) * scale
        s = jnp.where(mb, s, NEG_INF)
        m = jnp.maximum(jnp.max(s, axis=-1, keepdims=True), sink)
        p = jnp.exp(s - m)
        denom = jnp.sum(p, axis=-1, keepdims=True) + jnp.exp(sink - m)
        return jnp.einsum('bkgqs,bskd->bqkgd', (p / denom).astype(vbb.dtype), vbb)

    o = lax.map(block, (jnp.moveaxis(q, 1, 0), jnp.moveaxis(kb, 1, 0),
                        jnp.moveaxis(vb, 1, 0), mask))
    o = jnp.moveaxis(o, 0, 1).reshape(bsz, seq, Hq * Dh)
    return o @ w_o


def setup_inputs(seed: int = 0) -> dict:
    key = jax.random.key(seed)
    ks = iter(jax.random.split(key, 64))

    def nrm(shape, scale=1.0):
        return scale * jax.random.normal(next(ks), shape, jnp.float32)

    def unif(shape, lo, hi):
        return jax.random.uniform(next(ks), shape, jnp.float32, lo, hi)

    D = D_MODEL
    dt0 = jnp.exp(unif((N_B, MB_HEADS), math.log(1e-3), math.log(1e-1)))
    return {
        'x': nrm((BATCH, SEQ, D)),
        'c': nrm((BATCH, D)),
        'ada_w': nrm((DEPTH, D, N_SUB * 3 * D), 0.5 * D ** -0.5),
        'ada_b': nrm((DEPTH, N_SUB * 3 * D), 0.02),
        'norm_g': 1.0 + nrm((DEPTH, N_SUB, D), 0.05),
        'ffn_w_in': nrm((DEPTH, 2, D, 2 * D_FF), D ** -0.5),
        'ffn_w_out': nrm((DEPTH, 2, D_FF, D), D_FF ** -0.5),
        'rw_mu': unif((N_A, 6, D), 0.0, 1.0),
        'rw_w_rkv': nrm((N_A, 3, D, D), D ** -0.5),
        'rw_w_o': nrm((N_A, D, D), D ** -0.5),
        'rw_w0': unif((N_A, D), -6.5, -1.5),
        'rw_w1': nrm((N_A, D, RW_DECAY_LORA), D ** -0.5),
        'rw_w2': nrm((N_A, RW_DECAY_LORA, D), 0.1 * RW_DECAY_LORA ** -0.5),
        'rw_a0': nrm((N_A, D), 0.1),
        'rw_a1': nrm((N_A, D, RW_AAA_LORA), D ** -0.5),
        'rw_a2': nrm((N_A, RW_AAA_LORA, D), 0.5 * RW_AAA_LORA ** -0.5),
        'rw_g1': nrm((N_A, D, RW_GATE_LORA), D ** -0.5),
        'rw_g2': nrm((N_A, RW_GATE_LORA, D), RW_GATE_LORA ** -0.5),
        'rw_k_k': 0.85 + nrm((N_A, D), 0.05),
        'rw_k_a': 1.0 + nrm((N_A, D), 0.05),
        'rw_r_k': nrm((N_A, RW_HEADS, RW_HEAD), 0.1),
        'rw_ln_w': 1.0 + nrm((N_A, D), 0.05),
        'rw_ln_b': nrm((N_A, D), 0.02),
        'rw_v0': 1.0 + nrm((N_A_VRES, D), 0.1),
        'rw_v1': nrm((N_A_VRES, D, RW_MV_LORA), D ** -0.5),
        'rw_v2': nrm((N_A_VRES, RW_MV_LORA, D), 0.5 * RW_MV_LORA ** -0.5),
        'mb_w_in': nrm((N_B, D, MB_IN_DIM), D ** -0.5),
        'mb_conv_w': nrm((N_B, MB_CONV, MB_CONV_DIM), 0.5),
        'mb_conv_b': nrm((N_B, MB_CONV_DIM), 0.02),
        'mb_dt_bias': dt0 + jnp.log(-jnp.expm1(-dt0)),
        'mb_A_log': jnp.log(unif((N_B, MB_HEADS), 1.0, 16.0)),
        'mb_D': 1.0 + nrm((N_B, MB_HEADS), 0.1),
        'mb_norm_g': 1.0 + nrm((N_B, MB_D_INNER), 0.05),
        'mb_w_out': nrm((N_B, MB_D_INNER, D), MB_D_INNER ** -0.5),
        'sw_w_qkv': nrm((N_C, D, (SW_Q_HEADS + 2 * SW_KV_HEADS) * SW_HEAD_DIM), D ** -0.5),
        'sw_q_norm': 1.0 + nrm((N_C, SW_HEAD_DIM), 0.05),
        'sw_k_norm': 1.0 + nrm((N_C, SW_HEAD_DIM), 0.05),
        'sw_sinks': nrm((N_C, SW_Q_HEADS), 0.5),
        'sw_w_o': nrm((N_C, SW_Q_HEADS * SW_HEAD_DIM, D), (SW_Q_HEADS * SW_HEAD_DIM) ** -0.5),
    }


def reference(x, c, ada_w, ada_b, norm_g, ffn_w_in, ffn_w_out,
              rw_mu, rw_w_rkv, rw_w_o, rw_w0, rw_w1, rw_w2, rw_a0, rw_a1, rw_a2,
              rw_g1, rw_g2, rw_k_k, rw_k_a, rw_r_k, rw_ln_w, rw_ln_b,
              rw_v0, rw_v1, rw_v2,
              mb_w_in, mb_conv_w, mb_conv_b, mb_dt_bias, mb_A_log, mb_D, mb_norm_g, mb_w_out,
              sw_w_qkv, sw_q_norm, sw_k_norm, sw_sinks, sw_w_o):
    bsz, _, d = x.shape
    c_act = jax.nn.silu(c)
    v_first = None
    for i in range(DEPTH):
        mod = (c_act @ ada_w[i] + ada_b[i]).reshape(bsz, N_SUB, 3, d)
        shift, scale, gate = mod[:, :, 0], mod[:, :, 1], mod[:, :, 2]

        h = adaln(x, norm_g[i, 0], shift[:, 0], scale[:, 0])
        x = x + MACARON_W * gate[:, 0, None] * swiglu(h, ffn_w_in[i, 0], ffn_w_out[i, 0])

        h = adaln(x, norm_g[i, 1], shift[:, 1], scale[:, 1])
        kind, j = i % N_MIXERS, i // N_MIXERS
        if kind == 0:
            vres = None if v_first is None else (rw_v0[j - 1], rw_v1[j - 1], rw_v2[j - 1])
            y, v_first = rwkv7_time_mix(h, rw_mu[j], rw_w_rkv[j], rw_w_o[j], rw_w0[j], rw_w1[j],
                                        rw_w2[j], rw_a0[j], rw_a1[j], rw_a2[j], rw_g1[j],
                                        rw_g2[j], rw_k_k[j], rw_k_a[j], rw_r_k[j],
                                        rw_ln_w[j], rw_ln_b[j], v_first, vres)
        elif kind == 1:
            y = mamba2_ssd_mix(h, mb_w_in[j], mb_conv_w[j], mb_conv_b[j], mb_dt_bias[j],
                               mb_A_log[j], mb_D[j], mb_norm_g[j], mb_w_out[j])
        else:
            y = swa_sink_attention(h, sw_w_qkv[j], sw_q_norm[j], sw_k_norm[j],
                                   sw_sinks[j], sw_w_o[j])
        x = x + gate[:, 1, None] * y

        h = adaln(x, norm_g[i, 2], shift[:, 2], scale[:, 2])
        x = x + MACARON_W * gate[:, 2, None] * swiglu(h, ffn_w_in[i, 1], ffn_w_out[i, 1])
    return x
```

```python
import functools

import jax
import jax.numpy as jnp
from jax import lax
from jax.experimental import pallas as pl
from jax.experimental.pallas import tpu as pltpu

F32 = jnp.float32
BF16 = jnp.bfloat16

NORM_EPS = 1e-6
MACARON_W = 0.5
N_SUB = 3
HEAD = 64
RW_GN_EPS = 64e-5
RW_CHUNK = 64
RW_GROUP = 2
MB_GROUPS = 8
MB_STATE = 128
MB_CONV = 4
MB_CHUNK = 128
SW_KV_HEADS = 4
SW_BLOCK = 128
NEG_INF = -1e30
LANE = 128
VMEM_LIMIT = 56 * 1024 * 1024


def _cparams(sem):
    return pltpu.CompilerParams(dimension_semantics=sem, vmem_limit_bytes=VMEM_LIMIT)


def _dot(a, b):
    return jnp.dot(a, b, preferred_element_type=F32)


def _dot_nt(a, b):
    return lax.dot_general(a, b, (((1,), (1,)), ((), ())), preferred_element_type=F32)


def _dot_tn(a, b):
    return lax.dot_general(a, b, (((0,), (0,)), ((), ())), preferred_element_type=F32)


def _split3(x):
    hi = x.astype(BF16)
    r1 = x - hi.astype(F32)
    mid = r1.astype(BF16)
    lo = (r1 - mid.astype(F32)).astype(BF16)
    return hi, mid, lo


def _dot_exact_rhs(x, m):
    hi, mid, lo = _split3(x)
    return _dot(hi, m) + _dot(mid, m) + _dot(lo, m)


def _dot_exact_lhs(m, x):
    hi, mid, lo = _split3(x)
    return _dot(m, hi) + _dot(m, mid) + _dot(m, lo)


def _sigmoid(x):
    return 1.0 / (1.0 + jnp.exp(-x))


def _silu(x):
    return x * _sigmoid(x)


def _softplus(x):
    return jnp.maximum(x, 0.0) + jnp.log(1.0 + jnp.exp(-jnp.abs(x)))


def _adaln(x, g, shift, scale):
    ms = jnp.mean(x * x, axis=-1, keepdims=True)
    return (x * lax.rsqrt(ms + NORM_EPS)) * g * (1.0 + scale) + shift


def _iota(shape, axis):
    return lax.broadcasted_iota(jnp.int32, shape, axis)


def _shift_rows(cur, tail, s):
    rolled = pltpu.roll(cur, s, 0)
    head = jnp.where(_iota(tail.shape, 0) < s, pltpu.roll(tail, s, 0), rolled[0:8])
    return jnp.concatenate([head, rolled[8:]], axis=0)


def _mod_kernel(c_ref, w_ref, b_ref, o_ref):
    ca = _silu(c_ref[...])
    a_hi, a_mid, a_lo = _split3(ca)
    w_hi, w_mid, w_lo = _split3(w_ref[0])
    acc = _dot(a_hi, w_hi) + (_dot(a_hi, w_mid) + _dot(a_mid, w_hi))
    acc = acc + (_dot(a_hi, w_lo) + _dot(a_mid, w_mid) + _dot(a_lo, w_hi))
    o_ref[0] = acc + b_ref[0]


def _modulation(c, ada_w, ada_b):
    depth, d, n = ada_w.shape
    bsz = c.shape[0]
    tn = 1152
    out = pl.pallas_call(
        _mod_kernel,
        grid=(depth, n // tn),
        in_specs=[
            pl.BlockSpec((bsz, d), lambda l, j: (0, 0)),
            pl.BlockSpec((1, d, tn), lambda l, j: (l, 0, j)),
            pl.BlockSpec((1, 1, tn), lambda l, j: (l, 0, j)),
        ],
        out_specs=pl.BlockSpec((1, bsz, tn), lambda l, j: (l, 0, j)),
        out_shape=jax.ShapeDtypeStruct((depth, bsz, n), F32),
        compiler_params=_cparams(("parallel", "parallel")),
        name="adaln_modulation",
    )(c, ada_w, ada_b.reshape(depth, 1, n))
    return out.reshape(depth, bsz, N_SUB * 3, d)


def _ffn_kernel(x_ref, g_ref, mod_ref, win_ref, wout_ref, o_ref, h_ref, acc_ref, *, sub, fc):
    dff = wout_ref.shape[0]
    x = x_ref[...]
    shift = mod_ref[0, 3 * sub:3 * sub + 1, :]
    scale = mod_ref[0, 3 * sub + 1:3 * sub + 2, :]
    gate = mod_ref[0, 3 * sub + 2:3 * sub + 3, :]
    h_ref[...] = _adaln(x, g_ref[...], shift, scale).astype(BF16)
    for c in range(dff // fc):
        h = h_ref[...]
        a = _dot(h, win_ref[:, c * fc:(c + 1) * fc])
        b = _dot(h, win_ref[:, dff + c * fc:dff + (c + 1) * fc])
        act = (_silu(a) * b).astype(BF16)
        part = _dot(act, wout_ref[c * fc:(c + 1) * fc, :])
        if c == 0:
            acc_ref[...] = part
        else:
            acc_ref[...] += part
    o_ref[...] = x + (MACARON_W * gate) * acc_ref[...]


def _ffn(x2, g, mod, w_in, w_out, *, sub, seq, tm=512, fc=256):
    t, d = x2.shape
    dff = w_out.shape[0]
    tm = min(tm, seq)
    tiles_per_seq = seq // tm
    resident = dict(pipeline_mode=pl.Buffered(1))
    return pl.pallas_call(
        functools.partial(_ffn_kernel, sub=sub, fc=fc),
        grid=(t // tm,),
        in_specs=[
            pl.BlockSpec((tm, d), lambda i: (i, 0)),
            pl.BlockSpec((1, d), lambda i: (0, 0)),
            pl.BlockSpec((1, N_SUB * 3, d), lambda i: (i // tiles_per_seq, 0, 0)),
            pl.BlockSpec((d, 2 * dff), lambda i: (0, 0), **resident),
            pl.BlockSpec((dff, d), lambda i: (0, 0), **resident),
        ],
        out_specs=pl.BlockSpec((tm, d), lambda i: (i, 0)),
        out_shape=jax.ShapeDtypeStruct((t, d), F32),
        scratch_shapes=[pltpu.VMEM((tm, d), BF16), pltpu.VMEM((tm, d), F32)],
        compiler_params=_cparams(("parallel",)),
        name="macaron_ffn",
    )(x2, g.reshape(1, d), mod, w_in, w_out)


def _proj_res_kernel(x_ref, z_ref, mod_ref, w_ref, o_ref):
    gate = mod_ref[0, 5:6, :]
    o_ref[...] = x_ref[...] + gate * _dot(z_ref[...].astype(BF16), w_ref[...])


def _proj_residual(x2, z2, mod, w, *, seq, tm=512):
    t, d = x2.shape
    kdim = z2.shape[1]
    tm = min(tm, seq)
    tiles_per_seq = seq // tm
    return pl.pallas_call(
        _proj_res_kernel,
        grid=(t // tm,),
        in_specs=[
            pl.BlockSpec((tm, d), lambda i: (i, 0)),
            pl.BlockSpec((tm, kdim), lambda i: (i, 0)),
            pl.BlockSpec((1, N_SUB * 3, d), lambda i: (i // tiles_per_seq, 0, 0)),
            pl.BlockSpec((kdim, d), lambda i: (0, 0), pipeline_mode=pl.Buffered(1)),
        ],
        out_specs=pl.BlockSpec((tm, d), lambda i: (i, 0)),
        out_shape=jax.ShapeDtypeStruct((t, d), F32),
        compiler_params=_cparams(("parallel",)),
        name="mixer_out_proj",
    )(x2, z2, mod, w)


def _rwkv_pre_kernel(*refs, tiles_per_seq, has_vres):
    if has_vres:
        (x_ref, xp_ref, g_ref, mod_ref, mu_ref, wr_ref, wk_ref, wv_ref, w0_ref, w1_ref, w2_ref,
         a0_ref, a1_ref, a2_ref, g1_ref, g2_ref, v0_ref, v1_ref, v2_ref, vf_ref,
         r_ref, k_ref, v_ref, ld_ref, a_ref, gt_ref, h_ref, dx_ref) = refs
    else:
        (x_ref, xp_ref, g_ref, mod_ref, mu_ref, wr_ref, wk_ref, wv_ref, w0_ref, w1_ref, w2_ref,
         a0_ref, a1_ref, a2_ref, g1_ref, g2_ref,
         r_ref, k_ref, v_ref, ld_ref, a_ref, gt_ref, h_ref, dx_ref) = refs
    shift = mod_ref[0, 3:4, :]
    scale = mod_ref[0, 4:5, :]
    g = g_ref[...]
    h = _adaln(x_ref[...], g, shift, scale)
    first = (pl.program_id(0) % tiles_per_seq) == 0
    hp = _adaln(xp_ref[...], g, shift, scale)
    hp = jnp.where(first, 0.0, hp)
    h_ref[...] = h
    dx_ref[...] = _shift_rows(h, hp, 1) - h

    def mixed(j):
        return (h_ref[...] + dx_ref[...] * mu_ref[j:j + 1, :]).astype(BF16)

    r_ref[...] = _dot(mixed(0), wr_ref[...])
    k_ref[...] = _dot(mixed(2), wk_ref[...])
    xv = mixed(3)
    v = _dot(xv, wv_ref[...])
    if has_vres:
        mixv = _sigmoid(v0_ref[...] + _dot(_dot(xv, v1_ref[...]).astype(BF16), v2_ref[...]))
        v = v + (vf_ref[...] - v) * mixv
    v_ref[...] = v
    wl = w0_ref[...] + _dot(jnp.tanh(_dot(mixed(1), w1_ref[...])).astype(BF16), w2_ref[...])
    wl = -_softplus(-wl) - 0.5
    ld_ref[...] = -jnp.exp(wl)
    a_ref[...] = _sigmoid(a0_ref[...] + _dot(_dot(mixed(4), a1_ref[...]).astype(BF16), a2_ref[...]))
    gt_ref[...] = _dot(_sigmoid(_dot(mixed(5), g1_ref[...])).astype(BF16), g2_ref[...])


def _pad_cols(w, n):
    return jnp.pad(w, ((0, 0), (0, n - w.shape[1])))


def _pad_rows(w, n):
    return jnp.pad(w, ((0, n - w.shape[0]), (0, 0)))


def _round_up(n, m):
    return (n + m - 1) // m * m


def _lora_pair(w_a, w_b):
    n = _round_up(w_a.shape[1], LANE)
    return _pad_cols(w_a, n).astype(BF16), _pad_rows(w_b, n).astype(BF16)


def _rwkv_pre(x2, g, mod, mu, w_rkv, w0, w1, w2, a0, a1, a2, g1, g2, vres, v_first, *, seq, tm=256):
    t, d = x2.shape
    tm = min(tm, seq)
    tiles_per_seq = seq // tm
    has_vres = vres is not None
    const = lambda i: (0, 0)
    res = dict(pipeline_mode=pl.Buffered(1))
    row = lambda i: (i, 0)
    w1p, w2p = _lora_pair(w1, w2)
    a1p, a2p = _lora_pair(a1, a2)
    g1p, g2p = _lora_pair(g1, g2)
    args = [x2, x2, g.reshape(1, d), mod, _pad_rows(mu, 8),
            w_rkv[0].astype(BF16), w_rkv[1].astype(BF16), w_rkv[2].astype(BF16),
            w0.reshape(1, d), w1p, w2p, a0.reshape(1, d), a1p, a2p, g1p, g2p]
    in_specs = [
        pl.BlockSpec((tm, d), row),
        pl.BlockSpec((8, d), lambda i: (jnp.maximum(i * (tm // 8) - 1, 0), 0)),
        pl.BlockSpec((1, d), const),
        pl.BlockSpec((1, N_SUB * 3, d), lambda i: (i // tiles_per_seq, 0, 0)),
        pl.BlockSpec((8, d), const),
        pl.BlockSpec((d, d), const, **res), pl.BlockSpec((d, d), const, **res),
        pl.BlockSpec((d, d), const, **res),
        pl.BlockSpec((1, d), const),
        pl.BlockSpec(w1p.shape, const, **res), pl.BlockSpec(w2p.shape, const, **res),
        pl.BlockSpec((1, d), const),
        pl.BlockSpec(a1p.shape, const, **res), pl.BlockSpec(a2p.shape, const, **res),
        pl.BlockSpec(g1p.shape, const, **res), pl.BlockSpec(g2p.shape, const, **res),
    ]
    if has_vres:
        v0, v1, v2 = vres
        v1p, v2p = _lora_pair(v1, v2)
        args += [v0.reshape(1, d), v1p, v2p, v_first]
        in_specs += [pl.BlockSpec((1, d), const), pl.BlockSpec(v1p.shape, const, **res),
                     pl.BlockSpec(v2p.shape, const, **res), pl.BlockSpec((tm, d), row)]
    out = jax.ShapeDtypeStruct((t, d), F32)
    return pl.pallas_call(
        functools.partial(_rwkv_pre_kernel, tiles_per_seq=tiles_per_seq, has_vres=has_vres),
        grid=(t // tm,),
        in_specs=in_specs,
        out_specs=[pl.BlockSpec((tm, d), row)] * 6,
        out_shape=[out] * 6,
        scratch_shapes=[pltpu.VMEM((tm, d), F32), pltpu.VMEM((tm, d), F32)],
        compiler_params=_cparams(("parallel",)),
        name="rwkv7_projections",
    )(*args)


def _rwkv_scan_kernel(r_ref, k_ref, v_ref, ld_ref, a_ref, gt_ref, kk_ref, ka_ref, rk_ref,
                      lnw_ref, lnb_ref, z_ref, st_ref):
    L = RW_CHUNK
    W = RW_GROUP * HEAD
    R = RW_GROUP * L

    @pl.when(pl.program_id(2) == 0)
    def _():
        st_ref[...] = jnp.zeros_like(st_ref)

    r = r_ref[0]
    k = k_ref[0]
    v = v_ref[0]
    ld = ld_ref[0]
    a = a_ref[0]

    head_ones = jnp.where((_iota((W, W), 0) >> 6) == (_iota((W, W), 1) >> 6), 1.0, 0.0).astype(BF16)

    def head_sum(x):
        return _dot_exact_rhs(x, head_ones)

    kkn = k * kk_ref[...]
    kk = kkn / jnp.maximum(jnp.sqrt(head_sum(kkn * kkn)), 1e-12)
    kmod = k * (1.0 + (a - 1.0) * ka_ref[...])

    tril = jnp.where(_iota((L, L), 0) >= _iota((L, L), 1), 1.0, 0.0).astype(BF16)
    cum = _dot_exact_lhs(tril, ld)
    c_end = cum[L - 1:L, :]
    e_pos = jnp.exp(cum)
    e_neg = jnp.exp(-cum)
    e_prev = jnp.exp(cum - ld)
    e_end = jnp.exp(c_end - cum)
    kka = kk * a

    rows = _iota((R, W), 0)
    cols = _iota((R, W), 1)
    same_head = (rows >> 6) == (cols >> 6)

    def stack(x):
        xb = x.astype(BF16)
        return jnp.where(same_head, jnp.concatenate([xb] * RW_GROUP, axis=0), jnp.zeros((), BF16))

    a_s = stack(-kk * e_prev)
    r_s = stack(r * e_pos)
    b_s = stack(kka * e_neg)
    k_s = stack(kmod * e_neg)
    v_s = stack(v)
    bp_s = stack(kka * e_end)
    kp_s = stack(kmod * e_end)

    mr = _iota((R, R), 0)
    mc = _iota((R, R), 1)
    strict = (mr & (L - 1)) > (mc & (L - 1))
    incl = (mr & (L - 1)) >= (mc & (L - 1))
    a_ab = jnp.where(strict, _dot_nt(a_s, b_s), 0.0)
    a_ak = jnp.where(strict, _dot_nt(a_s, k_s), 0.0)
    a_rb = jnp.where(incl, _dot_nt(r_s, b_s), 0.0)
    a_rk = jnp.where(incl, _dot_nt(r_s, k_s), 0.0)

    def mm(x, y):
        return _dot(x.astype(BF16), y.astype(BF16))

    eye = jnp.where(mr == mc, 1.0, 0.0)
    a_d = jnp.where((mr >> 3) == (mc >> 3), a_ab, 0.0)
    a_d2 = mm(a_d, a_d)
    a_d4 = mm(a_d2, a_d2)
    tinv = mm(mm(eye + a_d, eye + a_d2), eye + a_d4)
    for sh in (3, 4, 5):
        off = ((mr >> (sh + 1)) == (mc >> (sh + 1))) & ((mr >> sh) != (mc >> sh))
        tinv = tinv + mm(mm(tinv, jnp.where(off, a_ab, 0.0)), tinv)

    st = st_ref[...].astype(BF16)
    u = mm(tinv, _dot_nt(a_s, st) + mm(a_ak, v_s))
    u_b = u.astype(BF16)
    y_s = _dot_nt(r_s, st) + mm(a_rb, u_b) + mm(a_rk, v_s)
    y = y_s[0:L]
    for i in range(1, RW_GROUP):
        y = y + y_s[i * L:(i + 1) * L]
    st_ref[...] = st_ref[...] * jnp.exp(c_end) + _dot_tn(u_b, bp_s) + _dot_tn(v_s, kp_s)

    mean = head_sum(y) * (1.0 / HEAD)
    yc = y - mean
    var = head_sum(yc * yc) * (1.0 / HEAD)
    yn = yc * lax.rsqrt(var + RW_GN_EPS) * lnw_ref[...] + lnb_ref[...]
    bonus = head_sum(r * kmod * rk_ref[...]) * v
    z_ref[0] = (yn + bonus) * gt_ref[0]


def _rwkv_scan(r, k, v, ld, a, gt, k_k, k_a, r_k, ln_w, ln_b):
    bsz, seq, d = r.shape
    w = RW_GROUP * HEAD
    act = pl.BlockSpec((1, RW_CHUNK, w), lambda b, g, c: (b, c, g))
    par = pl.BlockSpec((1, w), lambda b, g, c: (0, g))
    return pl.pallas_call(
        _rwkv_scan_kernel,
        grid=(bsz, d // w, seq // RW_CHUNK),
        in_specs=[act] * 6 + [par] * 5,
        out_specs=act,
        out_shape=jax.ShapeDtypeStruct((bsz, seq, d), F32),
        scratch_shapes=[pltpu.VMEM((w, w), F32)],
        compiler_params=_cparams(("parallel", "parallel", "arbitrary")),
        name="rwkv7_chunk_scan",
    )(r, k, v, ld, a, gt, k_k.reshape(1, d), k_a.reshape(1, d), r_k.reshape(1, d),
      ln_w.reshape(1, d), ln_b.reshape(1, d))


def _mamba_in_kernel(x_ref, g_ref, mod_ref, wz_ref, wx_ref, wb_ref, wc_ref, wdt_ref,
                     z_ref, xs_ref, b_ref, c_ref, dt_ref, h_ref):
    h_ref[...] = _adaln(x_ref[...], g_ref[...], mod_ref[0, 3:4, :], mod_ref[0, 4:5, :]).astype(BF16)
    z_ref[...] = _dot(h_ref[...], wz_ref[...])
    xs_ref[...] = _dot(h_ref[...], wx_ref[...])
    b_ref[...] = _dot(h_ref[...], wb_ref[...])
    c_ref[...] = _dot(h_ref[...], wc_ref[...])
    dt_ref[...] = _dot(h_ref[...], wdt_ref[...])


def _mamba_in(x2, g, mod, w_in, *, seq, d_inner, tm=256):
    t, d = x2.shape
    gn = MB_GROUPS * MB_STATE
    tm = min(tm, seq)
    tiles_per_seq = seq // tm
    wz = w_in[:, :d_inner].astype(BF16)
    wx = w_in[:, d_inner:2 * d_inner].astype(BF16)
    wb = w_in[:, 2 * d_inner:2 * d_inner + gn].astype(BF16)
    wc = w_in[:, 2 * d_inner + gn:2 * d_inner + 2 * gn].astype(BF16)
    wdt = _pad_cols(w_in[:, 2 * d_inner + 2 * gn:], LANE).astype(BF16)
    const = lambda i: (0, 0)
    row = lambda i: (i, 0)
    res = dict(pipeline_mode=pl.Buffered(1))
    widths = (d_inner, d_inner, gn, gn, LANE)
    return pl.pallas_call(
        _mamba_in_kernel,
        grid=(t // tm,),
        in_specs=[pl.BlockSpec((tm, d), row), pl.BlockSpec((1, d), const),
                  pl.BlockSpec((1, N_SUB * 3, d), lambda i: (i // tiles_per_seq, 0, 0))]
        + [pl.BlockSpec((d, n), const, **res) for n in widths],
        out_specs=[pl.BlockSpec((tm, n), row) for n in widths],
        out_shape=[jax.ShapeDtypeStruct((t, n), F32) for n in widths],
        scratch_shapes=[pltpu.VMEM((tm, d), BF16)],
        compiler_params=_cparams(("parallel",)),
        name="mamba2_in_proj",
    )(x2, g.reshape(1, d), mod, wz, wx, wb, wc, wdt)


def _ssd_kernel(xs_ref, b_ref, c_ref, z_ref, dt_ref, cwx_ref, cwb_ref, cwc_ref, cbx_ref, cbb_ref,
                cbc_ref, dtb_ref, alog_ref, dsk_ref, ng_ref, y_ref,
                st_ref, tx_ref, tb_ref, tc_ref):
    L = MB_CHUNK
    hg = xs_ref.shape[2] // HEAD
    wx = hg * HEAD
    grp = pl.program_id(1)

    @pl.when(pl.program_id(2) == 0)
    def _():
        st_ref[...] = jnp.zeros_like(st_ref)
        tx_ref[...] = jnp.zeros_like(tx_ref)
        tb_ref[...] = jnp.zeros_like(tb_ref)
        tc_ref[...] = jnp.zeros_like(tc_ref)

    def conv_silu(cur_ref, tail_ref, w_ref, bias_ref):
        cur = cur_ref[0]
        tail = tail_ref[...]
        acc = cur * w_ref[MB_CONV - 1:MB_CONV, :] + bias_ref[...]
        for j in range(MB_CONV - 1):
            acc = acc + _shift_rows(cur, tail, MB_CONV - 1 - j) * w_ref[j:j + 1, :]
        tail_ref[...] = cur[L - 8:L]
        return _silu(acc)

    xs = conv_silu(xs_ref, tx_ref, cwx_ref, cbx_ref)
    bm = conv_silu(b_ref, tb_ref, cwb_ref, cbb_ref)
    cm = conv_silu(c_ref, tc_ref, cwc_ref, cbc_ref)

    dt = _softplus(dt_ref[0] + dtb_ref[...])
    da = dt * (-jnp.exp(alog_ref[...]))
    tril = jnp.where(_iota((L, L), 0) >= _iota((L, L), 1), 1.0, 0.0).astype(BF16)
    acum = _dot_exact_lhs(tril, da)

    head0 = hg * grp
    sel_x = jnp.where(_iota((LANE, wx), 0) == head0 + (_iota((LANE, wx), 1) >> 6), 1.0, 0.0).astype(BF16)
    sel_r = jnp.where(_iota((8, LANE), 1) == head0 + _iota((8, LANE), 0), 1.0, 0.0).astype(BF16)
    dt_x = _dot_exact_rhs(dt, sel_x)
    ac_x = _dot_exact_rhs(acum, sel_x)
    ahi, amid, alo = _split3(acum)
    ac_row = _dot_nt(sel_r, ahi) + _dot_nt(sel_r, amid) + _dot_nt(sel_r, alo)

    xdt = xs * dt_x
    cb = _dot_nt(cm.astype(BF16), bm.astype(BF16))
    causal = _iota((L, L), 0) >= _iota((L, L), 1)
    lane_head = _iota((L, wx), 1) >> 6
    xdt_b = xdt.astype(BF16)
    g_parts = []
    x_parts = []
    for j in range(hg):
        sel_c = jnp.where(_iota((LANE, L), 0) == head0 + j, 1.0, 0.0).astype(BF16)
        ac_col = _dot_exact_rhs(acum, sel_c)
        seg = jnp.where(causal, ac_col - ac_row[j:j + 1, :], -jnp.inf)
        g_parts.append((cb * jnp.exp(seg)).astype(BF16))
        x_parts.append(jnp.where(lane_head == j, xdt_b, jnp.zeros((), BF16)))
    y_diag = _dot(jnp.concatenate(g_parts, axis=1), jnp.concatenate(x_parts, axis=0))

    st = st_ref[...]
    y_off = _dot(cm.astype(BF16), st.astype(BF16)) * jnp.exp(ac_x)
    ac_last = ac_x[L - 1:L, :]
    xdec = (xdt * jnp.exp(ac_last - ac_x)).astype(BF16)
    st_ref[...] = st * jnp.exp(ac_last) + _dot_tn(bm.astype(BF16), xdec)

    y = y_diag + y_off + xs * dsk_ref[...]
    y = y * _silu(z_ref[0])
    ms = jnp.mean(y * y, axis=-1, keepdims=True)
    y_ref[0] = y * lax.rsqrt(ms + NORM_EPS) * ng_ref[...]


def _ssd(xs, bm, cm, z, dt, conv_w, conv_b, dt_bias, a_log, d_skip, norm_g):
    bsz, seq, d_inner = xs.shape
    n = MB_STATE
    gn = MB_GROUPS * n
    wx = d_inner // MB_GROUPS
    heads = d_inner // HEAD
    cw = _pad_rows(conv_w, 8)
    cwx, cwb, cwc = cw[:, :d_inner], cw[:, d_inner:d_inner + gn], cw[:, d_inner + gn:]
    cb = conv_b.reshape(1, -1)
    cbx, cbb, cbc = cb[:, :d_inner], cb[:, d_inner:d_inner + gn], cb[:, d_inner + gn:]
    dtb = _pad_cols(dt_bias.reshape(1, heads), LANE)
    alog = _pad_cols(a_log.reshape(1, heads), LANE)
    dsk = jnp.repeat(d_skip, HEAD).reshape(1, d_inner)
    tok = lambda w: pl.BlockSpec((1, MB_CHUNK, w), lambda b, g, c: (b, c, g))
    par = lambda r, w: pl.BlockSpec((r, w), lambda b, g, c: (0, g))
    whole = lambda r, w: pl.BlockSpec((r, w), lambda b, g, c: (0, 0))
    return pl.pallas_call(
        _ssd_kernel,
        grid=(bsz, MB_GROUPS, seq // MB_CHUNK),
        in_specs=[tok(wx), tok(n), tok(n), tok(wx),
                  pl.BlockSpec((1, MB_CHUNK, LANE), lambda b, g, c: (b, c, 0)),
                  par(8, wx), par(8, n), par(8, n), par(1, wx), par(1, n), par(1, n),
                  whole(1, LANE), whole(1, LANE), par(1, wx), par(1, wx)],
        out_specs=tok(wx),
        out_shape=jax.ShapeDtypeStruct((bsz, seq, d_inner), F32),
        scratch_shapes=[pltpu.VMEM((n, wx), F32), pltpu.VMEM((8, wx), F32),
                        pltpu.VMEM((8, n), F32), pltpu.VMEM((8, n), F32)],
        compiler_params=_cparams(("parallel", "parallel", "arbitrary")),
        name="mamba2_conv_ssd",
    )(xs, bm, cm, z, dt, cwx, cwb, cwc, cbx, cbb, cbc, dtb, alog, dsk, norm_g.reshape(1, d_inner))


def _swa_qkv_kernel(x_ref, g_ref, mod_ref, wq_ref, wk_ref, wv_ref, q_ref, k_ref, v_ref, h_ref):
    h_ref[...] = _adaln(x_ref[...], g_ref[...], mod_ref[0, 3:4, :], mod_ref[0, 4:5, :]).astype(BF16)
    q_ref[...] = _dot(h_ref[...], wq_ref[...])
    k_ref[...] = _dot(h_ref[...], wk_ref[...])
    v_ref[...] = _dot(h_ref[...], wv_ref[...])


def _swa_qkv(x2, g, mod, w_qkv, *, seq, tm=512):
    t, d = x2.shape
    nq = d
    nk = SW_KV_HEADS * HEAD
    tm = min(tm, seq)
    tiles_per_seq = seq // tm
    wq = w_qkv[:, :nq].astype(BF16)
    wk = w_qkv[:, nq:nq + nk].astype(BF16)
    wv = w_qkv[:, nq + nk:].astype(BF16)
    const = lambda i: (0, 0)
    row = lambda i: (i, 0)
    res = dict(pipeline_mode=pl.Buffered(1))
    widths = (nq, nk, nk)
    return pl.pallas_call(
        _swa_qkv_kernel,
        grid=(t // tm,),
        in_specs=[pl.BlockSpec((tm, d), row), pl.BlockSpec((1, d), const),
                  pl.BlockSpec((1, N_SUB * 3, d), lambda i: (i // tiles_per_seq, 0, 0))]
        + [pl.BlockSpec((d, n), const, **res) for n in widths],
        out_specs=[pl.BlockSpec((tm, n), row) for n in widths],
        out_shape=[jax.ShapeDtypeStruct((t, n), F32) for n in widths],
        scratch_shapes=[pltpu.VMEM((tm, d), BF16)],
        compiler_params=_cparams(("parallel",)),
        name="swa_qkv_proj",
    )(x2, g.reshape(1, d), mod, wq, wk, wv)


def _swa_kernel(q_ref, kc_ref, kp_ref, vc_ref, vp_ref, qn_ref, kn_ref, sink_ref, o_ref):
    T = SW_BLOCK
    nq = q_ref.shape[2] // HEAD
    gq = nq // SW_KV_HEADS
    first_key = jnp.where(pl.program_id(1) > 0, 0, T)
    scale = HEAD ** -0.5

    def head_norm(x, g):
        ms = jnp.mean(x * x, axis=-1, keepdims=True)
        return x * lax.rsqrt(ms + NORM_EPS) * g

    qi = _iota((gq * T, 2 * T), 0) & (T - 1)
    si = _iota((gq * T, 2 * T), 1)
    rel = qi + T - si
    mask = (rel >= 0) & (rel < T) & (si >= first_key)
    for kv in range(SW_KV_HEADS):
        lo, hi = kv * HEAD, (kv + 1) * HEAD
        kcat = jnp.concatenate([kp_ref[0, :, lo:hi], kc_ref[0, :, lo:hi]], axis=0)
        vcat = jnp.concatenate([vp_ref[0, :, lo:hi], vc_ref[0, :, lo:hi]], axis=0)
        kcat = head_norm(kcat, kn_ref[...]).astype(BF16)
        qs = [head_norm(q_ref[0, :, (kv * gq + j) * HEAD:(kv * gq + j + 1) * HEAD], qn_ref[...])
              for j in range(gq)]
        qcat = jnp.concatenate(qs, axis=0).astype(BF16)
        s = _dot_nt(qcat, kcat) * scale
        s = jnp.where(mask, s, NEG_INF)
        sink = jnp.concatenate(
            [jnp.broadcast_to(sink_ref[0:1, kv * gq + j:kv * gq + j + 1], (T, 1)) for j in range(gq)], axis=0)
        m = jnp.maximum(jnp.max(s, axis=-1, keepdims=True), sink)
        p = jnp.exp(s - m)
        denom = jnp.sum(p, axis=-1, keepdims=True) + jnp.exp(sink - m)
        o = _dot((p / denom).astype(BF16), vcat.astype(BF16))
        for j in range(gq):
            hq = kv * gq + j
            o_ref[0, :, hq * HEAD:(hq + 1) * HEAD] = o[j * T:(j + 1) * T]


def _swa(q, k, v, q_norm, k_norm, sinks):
    bsz, seq, dq = q.shape
    dk = k.shape[2]
    nq = dq // HEAD
    cur = lambda b, i: (b, i, 0)
    prev = lambda b, i: (b, jnp.maximum(i - 1, 0), 0)
    const = lambda b, i: (0, 0)
    return pl.pallas_call(
        _swa_kernel,
        grid=(bsz, seq // SW_BLOCK),
        in_specs=[pl.BlockSpec((1, SW_BLOCK, dq), cur),
                  pl.BlockSpec((1, SW_BLOCK, dk), cur), pl.BlockSpec((1, SW_BLOCK, dk), prev),
                  pl.BlockSpec((1, SW_BLOCK, dk), cur), pl.BlockSpec((1, SW_BLOCK, dk), prev),
                  pl.BlockSpec((1, HEAD), const), pl.BlockSpec((1, HEAD), const),
                  pl.BlockSpec((1, nq), const)],
        out_specs=pl.BlockSpec((1, SW_BLOCK, dq), cur),
        out_shape=jax.ShapeDtypeStruct((bsz, seq, dq), F32),
        compiler_params=_cparams(("parallel", "parallel")),
        name="swa_sink_attention",
    )(q, k, k, v, v, q_norm.reshape(1, HEAD), k_norm.reshape(1, HEAD), sinks.reshape(1, nq))


def kernel(x, c, ada_w, ada_b, norm_g, ffn_w_in, ffn_w_out, rw_mu, rw_w_rkv, rw_w_o, rw_w0, rw_w1, rw_w2, rw_a0, rw_a1, rw_a2, rw_g1, rw_g2, rw_k_k, rw_k_a, rw_r_k, rw_ln_w, rw_ln_b, rw_v0, rw_v1, rw_v2, mb_w_in, mb_conv_w, mb_conv_b, mb_dt_bias, mb_A_log, mb_D, mb_norm_g, mb_w_out, sw_w_qkv, sw_q_norm, sw_k_norm, sw_sinks, sw_w_o):
    bsz, seq, d = x.shape
    depth = ada_w.shape[0]
    t = bsz * seq
    mods = _modulation(c, ada_w, ada_b)
    x2 = x.reshape(t, d)
    v_first = None
    for i in range(depth):
        mod = mods[i]
        x2 = _ffn(x2, norm_g[i, 0], mod, ffn_w_in[i, 0].astype(BF16), ffn_w_out[i, 0].astype(BF16),
                  sub=0, seq=seq)
        kind, j = i % 3, i // 3
        if kind == 0:
            vres = None if v_first is None else (rw_v0[j - 1], rw_v1[j - 1], rw_v2[j - 1])
            r, k, v, ld, a, gt = _rwkv_pre(
                x2, norm_g[i, 1], mod, rw_mu[j], rw_w_rkv[j], rw_w0[j], rw_w1[j], rw_w2[j],
                rw_a0[j], rw_a1[j], rw_a2[j], rw_g1[j], rw_g2[j], vres, v_first, seq=seq)
            if v_first is None:
                v_first = v
            sh = (bsz, seq, d)
            z = _rwkv_scan(r.reshape(sh), k.reshape(sh), v.reshape(sh), ld.reshape(sh), a.reshape(sh),
                           gt.reshape(sh), rw_k_k[j], rw_k_a[j], rw_r_k[j], rw_ln_w[j], rw_ln_b[j])
            x2 = _proj_residual(x2, z.reshape(t, d), mod, rw_w_o[j].astype(BF16), seq=seq)
        elif kind == 1:
            d_inner = mb_w_out.shape[1]
            z, xs, bm, cm, dt = _mamba_in(x2, norm_g[i, 1], mod, mb_w_in[j], seq=seq, d_inner=d_inner)
            y = _ssd(xs.reshape(bsz, seq, -1), bm.reshape(bsz, seq, -1), cm.reshape(bsz, seq, -1),
                     z.reshape(bsz, seq, -1), dt.reshape(bsz, seq, -1), mb_conv_w[j], mb_conv_b[j],
                     mb_dt_bias[j], mb_A_log[j], mb_D[j], mb_norm_g[j])
            x2 = _proj_residual(x2, y.reshape(t, d_inner), mod, mb_w_out[j].astype(BF16), seq=seq)
        else:
            q, k, v = _swa_qkv(x2, norm_g[i, 1], mod, sw_w_qkv[j], seq=seq)
            o = _swa(q.reshape(bsz, seq, -1), k.reshape(bsz, seq, -1), v.reshape(bsz, seq, -1),
                     sw_q_norm[j], sw_k_norm[j], sw_sinks[j])
            x2 = _proj_residual(x2, o.reshape(t, -1), mod, sw_w_o[j].astype(BF16), seq=seq)
        x2 = _ffn(x2, norm_g[i, 2], mod, ffn_w_in[i, 1].astype(BF16), ffn_w_out[i, 1].astype(BF16),
                  sub=2, seq=seq)
    return x2.reshape(bsz, seq, d)
```

```python
import functools

import jax
import jax.numpy as jnp
from jax import lax
from jax.experimental import pallas as pl
from jax.experimental.pallas import tpu as pltpu

F32 = jnp.float32
BF16 = jnp.bfloat16

NORM_EPS = 1e-6
MACARON_W = 0.5
N_SUB = 3
HEAD = 64
RW_GN_EPS = 64e-5
RW_CHUNK = 64
RW_GROUP = 2
MB_GROUPS = 8
MB_STATE = 128
MB_CONV = 4
MB_CHUNK = 128
SW_KV_HEADS = 4
SW_BLOCK = 128
NEG_INF = -1e30
LANE = 128
VMEM_LIMIT = 56 * 1024 * 1024


def _cparams(sem):
    return pltpu.CompilerParams(dimension_semantics=sem, vmem_limit_bytes=VMEM_LIMIT)


def _dot(a, b):
    return jnp.dot(a, b, preferred_element_type=F32)


def _dot_nt(a, b):
    return lax.dot_general(a, b, (((1,), (1,)), ((), ())), preferred_element_type=F32)


def _dot_tn(a, b):
    return lax.dot_general(a, b, (((0,), (0,)), ((), ())), preferred_element_type=F32)


def _split3(x):
    hi = x.astype(BF16)
    r1 = x - hi.astype(F32)
    mid = r1.astype(BF16)
    lo = (r1 - mid.astype(F32)).astype(BF16)
    return hi, mid, lo


def _dot_exact_lhs(m, x):
    hi, mid, lo = _split3(x)
    return _dot(m, hi) + _dot(m, mid) + _dot(m, lo)


def _sigmoid(x):
    return 1.0 / (1.0 + jnp.exp(-x))


def _silu(x):
    return x * _sigmoid(x)


def _softplus(x):
    return jnp.maximum(x, 0.0) + jnp.log(1.0 + jnp.exp(-jnp.abs(x)))


def _adaln(x, g, shift, scale):
    ms = jnp.mean(x * x, axis=-1, keepdims=True)
    return (x * lax.rsqrt(ms + NORM_EPS)) * g * (1.0 + scale) + shift


def _iota(shape, axis):
    return lax.broadcasted_iota(jnp.int32, shape, axis)


def _shift_rows(cur, tail, s):
    rolled = pltpu.roll(cur, s, 0)
    head = jnp.where(_iota(tail.shape, 0) < s, pltpu.roll(tail, s, 0), rolled[0:8])
    return jnp.concatenate([head, rolled[8:]], axis=0)


def _run_interleaved(chains):
    while chains:
        chains = [ch for ch in chains if next(ch, "done") != "done"]


def _mod_kernel(c_ref, w_ref, b_ref, o_ref):
    ca = _silu(c_ref[...])
    a_hi, a_mid, a_lo = _split3(ca)
    w_hi, w_mid, w_lo = _split3(w_ref[0])
    acc = _dot(a_hi, w_hi) + (_dot(a_hi, w_mid) + _dot(a_mid, w_hi))
    acc = acc + (_dot(a_hi, w_lo) + _dot(a_mid, w_mid) + _dot(a_lo, w_hi))
    o_ref[0] = acc + b_ref[0]


def _modulation(c, ada_w, ada_b):
    depth, d, n = ada_w.shape
    bsz = c.shape[0]
    tn = 1152
    out = pl.pallas_call(
        _mod_kernel,
        grid=(depth, n // tn),
        in_specs=[
            pl.BlockSpec((bsz, d), lambda l, j: (0, 0)),
            pl.BlockSpec((1, d, tn), lambda l, j: (l, 0, j)),
            pl.BlockSpec((1, 1, tn), lambda l, j: (l, 0, j)),
        ],
        out_specs=pl.BlockSpec((1, bsz, tn), lambda l, j: (l, 0, j)),
        out_shape=jax.ShapeDtypeStruct((depth, bsz, n), F32),
        compiler_params=_cparams(("parallel", "parallel")),
        name="adaln_modulation",
    )(c, ada_w, ada_b.reshape(depth, 1, n))
    return out.reshape(depth, bsz, N_SUB * 3, d)


def _ffn_kernel(x_ref, g_ref, mod_ref, win_ref, wout_ref, o_ref, h_ref, acc_ref, *, sub, fc):
    dff = wout_ref.shape[0]
    x = x_ref[...]
    shift = mod_ref[0, 3 * sub:3 * sub + 1, :]
    scale = mod_ref[0, 3 * sub + 1:3 * sub + 2, :]
    gate = mod_ref[0, 3 * sub + 2:3 * sub + 3, :]
    h_ref[...] = _adaln(x, g_ref[...], shift, scale).astype(BF16)
    for c in range(dff // fc):
        h = h_ref[...]
        a = _dot(h, win_ref[:, c * fc:(c + 1) * fc])
        b = _dot(h, win_ref[:, dff + c * fc:dff + (c + 1) * fc])
        act = (_silu(a) * b).astype(BF16)
        part = _dot(act, wout_ref[c * fc:(c + 1) * fc, :])
        if c == 0:
            acc_ref[...] = part
        else:
            acc_ref[...] += part
    o_ref[...] = x + (MACARON_W * gate) * acc_ref[...]


def _ffn(x2, g, mod, w_in, w_out, *, sub, seq, tm=512, fc=256):
    t, d = x2.shape
    dff = w_out.shape[0]
    tm = min(tm, seq)
    tiles_per_seq = seq // tm
    resident = dict(pipeline_mode=pl.Buffered(1))
    return pl.pallas_call(
        functools.partial(_ffn_kernel, sub=sub, fc=fc),
        grid=(t // tm,),
        in_specs=[
            pl.BlockSpec((tm, d), lambda i: (i, 0)),
            pl.BlockSpec((1, d), lambda i: (0, 0)),
            pl.BlockSpec((1, N_SUB * 3, d), lambda i: (i // tiles_per_seq, 0, 0)),
            pl.BlockSpec((d, 2 * dff), lambda i: (0, 0), **resident),
            pl.BlockSpec((dff, d), lambda i: (0, 0), **resident),
        ],
        out_specs=pl.BlockSpec((tm, d), lambda i: (i, 0)),
        out_shape=jax.ShapeDtypeStruct((t, d), F32),
        scratch_shapes=[pltpu.VMEM((tm, d), BF16), pltpu.VMEM((tm, d), F32)],
        compiler_params=_cparams(("parallel",)),
        name="macaron_ffn",
    )(x2, g.reshape(1, d), mod, w_in, w_out)


def _proj_res_kernel(x_ref, z_ref, mod_ref, w_ref, o_ref):
    gate = mod_ref[0, 5:6, :]
    o_ref[...] = x_ref[...] + gate * _dot(z_ref[...].astype(BF16), w_ref[...])


def _proj_residual(x2, z2, mod, w, *, seq, tm=512):
    t, d = x2.shape
    kdim = z2.shape[1]
    tm = min(tm, seq)
    tiles_per_seq = seq // tm
    return pl.pallas_call(
        _proj_res_kernel,
        grid=(t // tm,),
        in_specs=[
            pl.BlockSpec((tm, d), lambda i: (i, 0)),
            pl.BlockSpec((tm, kdim), lambda i: (i, 0)),
            pl.BlockSpec((1, N_SUB * 3, d), lambda i: (i // tiles_per_seq, 0, 0)),
            pl.BlockSpec((kdim, d), lambda i: (0, 0), pipeline_mode=pl.Buffered(1)),
        ],
        out_specs=pl.BlockSpec((tm, d), lambda i: (i, 0)),
        out_shape=jax.ShapeDtypeStruct((t, d), F32),
        compiler_params=_cparams(("parallel",)),
        name="mixer_out_proj",
    )(x2, z2, mod, w)


def _rwkv_pre_kernel(*refs, tiles_per_seq, has_vres):
    if has_vres:
        (x_ref, xp_ref, g_ref, mod_ref, mu_ref, wr_ref, wk_ref, wv_ref, w0_ref, w1_ref, w2_ref,
         a0_ref, a1_ref, a2_ref, g1_ref, g2_ref, v0_ref, v1_ref, v2_ref, vf_ref,
         r_ref, k_ref, v_ref, ld_ref, a_ref, gt_ref, h_ref, dx_ref) = refs
    else:
        (x_ref, xp_ref, g_ref, mod_ref, mu_ref, wr_ref, wk_ref, wv_ref, w0_ref, w1_ref, w2_ref,
         a0_ref, a1_ref, a2_ref, g1_ref, g2_ref,
         r_ref, k_ref, v_ref, ld_ref, a_ref, gt_ref, h_ref, dx_ref) = refs
    shift = mod_ref[0, 3:4, :]
    scale = mod_ref[0, 4:5, :]
    g = g_ref[...]
    h = _adaln(x_ref[...], g, shift, scale)
    first = (pl.program_id(0) % tiles_per_seq) == 0
    hp = _adaln(xp_ref[...], g, shift, scale)
    hp = jnp.where(first, 0.0, hp)
    h_ref[...] = h
    dx_ref[...] = _shift_rows(h, hp, 1) - h

    def mixed(j):
        return (h_ref[...] + dx_ref[...] * mu_ref[j:j + 1, :]).astype(BF16)

    r_ref[...] = _dot(mixed(0), wr_ref[...])
    k_ref[...] = _dot(mixed(2), wk_ref[...])
    xv = mixed(3)
    v = _dot(xv, wv_ref[...])
    if has_vres:
        mixv = _sigmoid(v0_ref[...] + _dot(_dot(xv, v1_ref[...]).astype(BF16), v2_ref[...]))
        v = v + (vf_ref[...] - v) * mixv
    v_ref[...] = v
    wl = w0_ref[...] + _dot(jnp.tanh(_dot(mixed(1), w1_ref[...])).astype(BF16), w2_ref[...])
    wl = -_softplus(-wl) - 0.5
    ld_ref[...] = -jnp.exp(wl)
    a_ref[...] = _sigmoid(a0_ref[...] + _dot(_dot(mixed(4), a1_ref[...]).astype(BF16), a2_ref[...]))
    gt_ref[...] = _dot(_sigmoid(_dot(mixed(5), g1_ref[...])).astype(BF16), g2_ref[...])


def _pad_cols(w, n):
    return jnp.pad(w, ((0, 0), (0, n - w.shape[1])))


def _pad_rows(w, n):
    return jnp.pad(w, ((0, n - w.shape[0]), (0, 0)))


def _round_up(n, m):
    return (n + m - 1) // m * m


def _lora_pair(w_a, w_b):
    n = _round_up(w_a.shape[1], LANE)
    return _pad_cols(w_a, n).astype(BF16), _pad_rows(w_b, n).astype(BF16)


def _rwkv_pre(x2, g, mod, mu, w_rkv, w0, w1, w2, a0, a1, a2, g1, g2, vres, v_first, *, seq, tm=256):
    t, d = x2.shape
    tm = min(tm, seq)
    tiles_per_seq = seq // tm
    has_vres = vres is not None
    const = lambda i: (0, 0)
    res = dict(pipeline_mode=pl.Buffered(1))
    row = lambda i: (i, 0)
    w1p, w2p = _lora_pair(w1, w2)
    a1p, a2p = _lora_pair(a1, a2)
    g1p, g2p = _lora_pair(g1, g2)
    args = [x2, x2, g.reshape(1, d), mod, _pad_rows(mu, 8),
            w_rkv[0].astype(BF16), w_rkv[1].astype(BF16), w_rkv[2].astype(BF16),
            w0.reshape(1, d), w1p, w2p, a0.reshape(1, d), a1p, a2p, g1p, g2p]
    in_specs = [
        pl.BlockSpec((tm, d), row),
        pl.BlockSpec((8, d), lambda i: (jnp.maximum(i * (tm // 8) - 1, 0), 0)),
        pl.BlockSpec((1, d), const),
        pl.BlockSpec((1, N_SUB * 3, d), lambda i: (i // tiles_per_seq, 0, 0)),
        pl.BlockSpec((8, d), const),
        pl.BlockSpec((d, d), const, **res), pl.BlockSpec((d, d), const, **res),
        pl.BlockSpec((d, d), const, **res),
        pl.BlockSpec((1, d), const),
        pl.BlockSpec(w1p.shape, const, **res), pl.BlockSpec(w2p.shape, const, **res),
        pl.BlockSpec((1, d), const),
        pl.BlockSpec(a1p.shape, const, **res), pl.BlockSpec(a2p.shape, const, **res),
        pl.BlockSpec(g1p.shape, const, **res), pl.BlockSpec(g2p.shape, const, **res),
    ]
    if has_vres:
        v0, v1, v2 = vres
        v1p, v2p = _lora_pair(v1, v2)
        args += [v0.reshape(1, d), v1p, v2p, v_first]
        in_specs += [pl.BlockSpec((1, d), const), pl.BlockSpec(v1p.shape, const, **res),
                     pl.BlockSpec(v2p.shape, const, **res), pl.BlockSpec((tm, d), row)]
    out = jax.ShapeDtypeStruct((t, d), F32)
    return pl.pallas_call(
        functools.partial(_rwkv_pre_kernel, tiles_per_seq=tiles_per_seq, has_vres=has_vres),
        grid=(t // tm,),
        in_specs=in_specs,
        out_specs=[pl.BlockSpec((tm, d), row)] * 6,
        out_shape=[out] * 6,
        scratch_shapes=[pltpu.VMEM((tm, d), F32), pltpu.VMEM((tm, d), F32)],
        compiler_params=_cparams(("parallel",)),
        name="rwkv7_projections",
    )(*args)


def _rwkv_scan_kernel(*refs):
    st_ref = refs[-1]

    @pl.when(pl.program_id(2) == 0)
    def _():
        st_ref[...] = jnp.zeros_like(st_ref)

    _run_interleaved([_rwkv_chain(gi, *refs) for gi in range(st_ref.shape[0])])


def _rwkv_chain(gi, r_ref, k_ref, v_ref, ld_ref, a_ref, gt_ref, kk_ref, ka_ref, rk_ref,
                lnw_ref, lnb_ref, z_ref, st_ref):
    L = RW_CHUNK
    W = RW_GROUP * HEAD
    R = RW_GROUP * L
    lo, hi = gi * W, (gi + 1) * W

    r = r_ref[0, :, lo:hi]
    k = k_ref[0, :, lo:hi]
    v = v_ref[0, :, lo:hi]
    ld = ld_ref[0, :, lo:hi]
    a = a_ref[0, :, lo:hi]

    head_ones = jnp.where((_iota((W, W), 0) >> 6) == (_iota((W, W), 1) >> 6), 1.0, 0.0).astype(BF16)

    def head_sum(x, passes=1):
        hi_part = x.astype(BF16)
        out = _dot(hi_part, head_ones)
        if passes == 2:
            out = out + _dot((x - hi_part.astype(F32)).astype(BF16), head_ones)
        return out

    kkn = k * kk_ref[:, lo:hi]
    kk_ss = head_sum(kkn * kkn)
    tril = jnp.where(_iota((L, L), 0) >= _iota((L, L), 1), 1.0, 0.0).astype(BF16)
    cum = _dot_exact_lhs(tril, ld)
    yield
    kk = kkn / jnp.maximum(jnp.sqrt(kk_ss), 1e-12)
    kmod = k * (1.0 + (a - 1.0) * ka_ref[:, lo:hi])
    c_end = cum[L - 1:L, :]
    e_pos = jnp.exp(cum)
    e_neg = jnp.exp(-cum)
    e_prev = jnp.exp(cum - ld)
    e_end = jnp.exp(c_end - cum)
    kka = kk * a

    rows = _iota((R, W), 0)
    cols = _iota((R, W), 1)
    same_head = (rows >> 6) == (cols >> 6)

    def stack(x):
        xb = x.astype(BF16)
        return jnp.where(same_head, jnp.concatenate([xb] * RW_GROUP, axis=0), jnp.zeros((), BF16))

    ar_s = jnp.concatenate([stack(-kk * e_prev), stack(r * e_pos)], axis=0)
    bk_s = jnp.concatenate([stack(kka * e_neg), stack(kmod * e_neg)], axis=0)
    v_s = stack(v)
    bp_s = stack(kka * e_end)
    kp_s = stack(kmod * e_end)

    mr = _iota((R, R), 0)
    mc = _iota((R, R), 1)
    strict = (mr & (L - 1)) > (mc & (L - 1))
    incl = (mr & (L - 1)) >= (mc & (L - 1))
    prod = _dot_nt(ar_s, bk_s)
    st = st_ref[gi]
    from_state = _dot_nt(ar_s, st.astype(BF16))
    yield
    a_ab = jnp.where(strict, prod[0:R, 0:R], 0.0)
    a_ak = jnp.where(strict, prod[0:R, R:2 * R], 0.0)
    a_rb = jnp.where(incl, prod[R:2 * R, 0:R], 0.0)
    a_rk = jnp.where(incl, prod[R:2 * R, R:2 * R], 0.0)

    def mm(x, y):
        return _dot(x.astype(BF16), y.astype(BF16))

    eye = jnp.where(mr == mc, 1.0, 0.0)
    a_d = jnp.where((mr >> 3) == (mc >> 3), a_ab, 0.0)
    a_d2 = mm(a_d, a_d)
    rhs = from_state[0:R] + mm(a_ak, v_s)
    bonus = head_sum(r * kmod * rk_ref[:, lo:hi]) * v
    yield
    a_d4 = mm(a_d2, a_d2)
    tinv = mm(eye + a_d, eye + a_d2)
    yield
    tinv = mm(tinv, eye + a_d4)
    yield
    for sh in (3, 4, 5):
        off = ((mr >> (sh + 1)) == (mc >> (sh + 1))) & ((mr >> sh) != (mc >> sh))
        half = mm(tinv, jnp.where(off, a_ab, 0.0))
        yield
        tinv = tinv + mm(half, tinv)
        yield

    u_b = mm(tinv, rhs).astype(BF16)
    yield
    uv = jnp.concatenate([u_b, v_s], axis=0)
    y_s = from_state[R:2 * R] + _dot(
        jnp.concatenate([a_rb.astype(BF16), a_rk.astype(BF16)], axis=1), uv)
    st_ref[gi] = st * jnp.exp(c_end) + _dot_tn(uv, jnp.concatenate([bp_s, kp_s], axis=0))
    yield
    y = y_s[0:L]
    for i in range(1, RW_GROUP):
        y = y + y_s[i * L:(i + 1) * L]
    mean = head_sum(y, passes=2) * (1.0 / HEAD)
    yield
    yc = y - mean
    var = head_sum(yc * yc) * (1.0 / HEAD)
    yield
    yn = yc * lax.rsqrt(var + RW_GN_EPS) * lnw_ref[:, lo:hi] + lnb_ref[:, lo:hi]
    z_ref[0, :, lo:hi] = (yn + bonus) * gt_ref[0, :, lo:hi]


def _rwkv_scan(r, k, v, ld, a, gt, k_k, k_a, r_k, ln_w, ln_b, *, lanes=1024):
    bsz, seq, d = r.shape
    w = RW_GROUP * HEAD
    lanes = min(lanes, d)
    act = pl.BlockSpec((1, RW_CHUNK, lanes), lambda b, g, c: (b, c, g))
    par = pl.BlockSpec((1, lanes), lambda b, g, c: (0, g))
    return pl.pallas_call(
        _rwkv_scan_kernel,
        grid=(bsz, d // lanes, seq // RW_CHUNK),
        in_specs=[act] * 6 + [par] * 5,
        out_specs=act,
        out_shape=jax.ShapeDtypeStruct((bsz, seq, d), F32),
        scratch_shapes=[pltpu.VMEM((lanes // w, w, w), F32)],
        compiler_params=_cparams(("parallel", "parallel", "arbitrary")),
        name="rwkv7_chunk_scan",
    )(r, k, v, ld, a, gt, k_k.reshape(1, d), k_a.reshape(1, d), r_k.reshape(1, d),
      ln_w.reshape(1, d), ln_b.reshape(1, d))


def _mamba_in_kernel(x_ref, g_ref, mod_ref, wz_ref, wx_ref, wb_ref, wc_ref, wdt_ref,
                     z_ref, xs_ref, b_ref, c_ref, dt_ref, h_ref):
    h_ref[...] = _adaln(x_ref[...], g_ref[...], mod_ref[0, 3:4, :], mod_ref[0, 4:5, :]).astype(BF16)
    z_ref[...] = _dot(h_ref[...], wz_ref[...])
    xs_ref[...] = _dot(h_ref[...], wx_ref[...])
    b_ref[...] = _dot(h_ref[...], wb_ref[...])
    c_ref[...] = _dot(h_ref[...], wc_ref[...])
    dt_ref[...] = _dot(h_ref[...], wdt_ref[...])


def _mamba_in(x2, g, mod, w_in, *, seq, d_inner, tm=256):
    t, d = x2.shape
    gn = MB_GROUPS * MB_STATE
    tm = min(tm, seq)
    tiles_per_seq = seq // tm
    wz = w_in[:, :d_inner].astype(BF16)
    wx = w_in[:, d_inner:2 * d_inner].astype(BF16)
    wb = w_in[:, 2 * d_inner:2 * d_inner + gn].astype(BF16)
    wc = w_in[:, 2 * d_inner + gn:2 * d_inner + 2 * gn].astype(BF16)
    wdt = _pad_cols(w_in[:, 2 * d_inner + 2 * gn:], LANE).astype(BF16)
    const = lambda i: (0, 0)
    row = lambda i: (i, 0)
    res = dict(pipeline_mode=pl.Buffered(1))
    widths = (d_inner, d_inner, gn, gn, LANE)
    return pl.pallas_call(
        _mamba_in_kernel,
        grid=(t // tm,),
        in_specs=[pl.BlockSpec((tm, d), row), pl.BlockSpec((1, d), const),
                  pl.BlockSpec((1, N_SUB * 3, d), lambda i: (i // tiles_per_seq, 0, 0))]
        + [pl.BlockSpec((d, n), const, **res) for n in widths],
        out_specs=[pl.BlockSpec((tm, n), row) for n in widths],
        out_shape=[jax.ShapeDtypeStruct((t, n), F32) for n in widths],
        scratch_shapes=[pltpu.VMEM((tm, d), BF16)],
        compiler_params=_cparams(("parallel",)),
        name="mamba2_in_proj",
    )(x2, g.reshape(1, d), mod, wz, wx, wb, wc, wdt)


def _ssd_kernel(xs_ref, b_ref, c_ref, z_ref, dt_ref, cwx_ref, cwb_ref, cwc_ref, cbx_ref, cbb_ref,
                cbc_ref, dtb_ref, alog_ref, dsk_ref, ng_ref, y_ref,
                st_ref, sx_ref, sb_ref, sc_ref):
    L = MB_CHUNK
    stages = ((sx_ref, xs_ref), (sb_ref, b_ref), (sc_ref, c_ref))

    @pl.when(pl.program_id(1) == 0)
    def _():
        st_ref[...] = jnp.zeros_like(st_ref)
        for s_ref, _ in stages:
            s_ref[0:8, :] = jnp.zeros((8, s_ref.shape[1]), F32)

    for s_ref, cur_ref in stages:
        s_ref[8:8 + L, :] = cur_ref[0]

    dt = _softplus(dt_ref[0] + dtb_ref[...])
    da = dt * (-jnp.exp(alog_ref[...]))
    tril = jnp.where(_iota((L, L), 0) >= _iota((L, L), 1), 1.0, 0.0).astype(BF16)
    acum = _dot_exact_lhs(tril, da)
    shared = (_split3(dt), _split3(acum))
    refs = (sx_ref, sb_ref, sc_ref, z_ref, cwx_ref, cwb_ref, cwc_ref, cbx_ref, cbb_ref, cbc_ref,
            dsk_ref, ng_ref, y_ref, st_ref)
    _run_interleaved([_ssd_chain(g, shared, *refs) for g in range(st_ref.shape[0])])

    for s_ref, _ in stages:
        s_ref[0:8, :] = s_ref[L:L + 8, :]


def _ssd_chain(g, shared, sx_ref, sb_ref, sc_ref, z_ref, cwx_ref, cwb_ref, cwc_ref, cbx_ref,
               cbb_ref, cbc_ref, dsk_ref, ng_ref, y_ref, st_ref):
    L = MB_CHUNK
    n = MB_STATE
    wx = st_ref.shape[2]
    hg = wx // HEAD
    dt3, ac3 = shared
    xl, xh = g * wx, (g + 1) * wx
    nl, nh = g * n, (g + 1) * n

    def conv_silu(s_ref, w_ref, bias_ref, lo, hi):
        acc = bias_ref[:, lo:hi]
        for j in range(MB_CONV):
            off = 8 - (MB_CONV - 1) + j
            acc = acc + s_ref[off:off + L, lo:hi] * w_ref[j:j + 1, lo:hi]
        return _silu(acc)

    xs = conv_silu(sx_ref, cwx_ref, cbx_ref, xl, xh)
    bm = conv_silu(sb_ref, cwb_ref, cbb_ref, nl, nh).astype(BF16)
    cm = conv_silu(sc_ref, cwc_ref, cbc_ref, nl, nh).astype(BF16)

    head0 = hg * g
    sel_x = jnp.where(_iota((LANE, wx), 0) == head0 + (_iota((LANE, wx), 1) >> 6), 1.0, 0.0).astype(BF16)
    sel_r = jnp.where(_iota((8, LANE), 1) == head0 + _iota((8, LANE), 0), 1.0, 0.0).astype(BF16)
    sel_c = jnp.where(_iota((LANE, hg * L), 0) == head0 + (_iota((LANE, hg * L), 1) >> 7), 1.0, 0.0).astype(BF16)
    both = [jnp.concatenate([d_p, a_p], axis=0) for d_p, a_p in zip(dt3, ac3)]
    in_x = _dot(both[0], sel_x) + _dot(both[1], sel_x) + _dot(both[2], sel_x)
    dt_x, ac_x = in_x[0:L], in_x[L:2 * L]
    ac_row = _dot_nt(sel_r, ac3[0]) + _dot_nt(sel_r, ac3[1]) + _dot_nt(sel_r, ac3[2])
    ac_col = _dot(ac3[0], sel_c) + _dot(ac3[1], sel_c) + _dot(ac3[2], sel_c)
    cb = _dot_nt(cm, bm)
    st = st_ref[g]
    y_off = _dot(cm, st.astype(BF16))
    yield

    xdt = xs * dt_x
    causal = _iota((L, L), 0) >= _iota((L, L), 1)
    lane_head = _iota((L, wx), 1) >> 6
    xdt_b = xdt.astype(BF16)
    g_parts = []
    x_parts = []
    for j in range(hg):
        seg = jnp.where(causal, ac_col[:, j * L:(j + 1) * L] - ac_row[j:j + 1, :], -jnp.inf)
        g_parts.append((cb * jnp.exp(seg)).astype(BF16))
        x_parts.append(jnp.where(lane_head == j, xdt_b, jnp.zeros((), BF16)))
    y_diag = _dot(jnp.concatenate(g_parts, axis=1), jnp.concatenate(x_parts, axis=0))
    ac_last = ac_x[L - 1:L, :]
    xdec = (xdt * jnp.exp(ac_last - ac_x)).astype(BF16)
    st_ref[g] = st * jnp.exp(ac_last) + _dot_tn(bm, xdec)
    yield

    y = y_diag + y_off * jnp.exp(ac_x) + xs * dsk_ref[:, xl:xh]
    y = y * _silu(z_ref[0, :, xl:xh])
    ms = jnp.mean(y * y, axis=-1, keepdims=True)
    y_ref[0, :, xl:xh] = y * lax.rsqrt(ms + NORM_EPS) * ng_ref[:, xl:xh]


def _ssd(xs, bm, cm, z, dt, conv_w, conv_b, dt_bias, a_log, d_skip, norm_g):
    bsz, seq, d_inner = xs.shape
    n = MB_STATE
    gn = MB_GROUPS * n
    wx = d_inner // MB_GROUPS
    heads = d_inner // HEAD
    cw = _pad_rows(conv_w, 8)
    cwx, cwb, cwc = cw[:, :d_inner], cw[:, d_inner:d_inner + gn], cw[:, d_inner + gn:]
    cb = conv_b.reshape(1, -1)
    cbx, cbb, cbc = cb[:, :d_inner], cb[:, d_inner:d_inner + gn], cb[:, d_inner + gn:]
    dtb = _pad_cols(dt_bias.reshape(1, heads), LANE)
    alog = _pad_cols(a_log.reshape(1, heads), LANE)
    dsk = jnp.repeat(d_skip, HEAD).reshape(1, d_inner)
    tok = lambda w: pl.BlockSpec((1, MB_CHUNK, w), lambda b, c: (b, c, 0))
    par = lambda r, w: pl.BlockSpec((r, w), lambda b, c: (0, 0))
    return pl.pallas_call(
        _ssd_kernel,
        grid=(bsz, seq // MB_CHUNK),
        in_specs=[tok(d_inner), tok(gn), tok(gn), tok(d_inner), tok(LANE),
                  par(8, d_inner), par(8, gn), par(8, gn), par(1, d_inner), par(1, gn), par(1, gn),
                  par(1, LANE), par(1, LANE), par(1, d_inner), par(1, d_inner)],
        out_specs=tok(d_inner),
        out_shape=jax.ShapeDtypeStruct((bsz, seq, d_inner), F32),
        scratch_shapes=[pltpu.VMEM((MB_GROUPS, n, wx), F32),
                        pltpu.VMEM((8 + MB_CHUNK,d_inner), F32),
                        pltpu.VMEM((8 + MB_CHUNK,gn), F32),
                        pltpu.VMEM((8 + MB_CHUNK,gn), F32)],
        compiler_params=_cparams(("parallel", "arbitrary")),
        name="mamba2_conv_ssd",
    )(xs, bm, cm, z, dt, cwx, cwb, cwc, cbx, cbb, cbc, dtb, alog, dsk, norm_g.reshape(1, d_inner))


def _swa_qkv_kernel(x_ref, g_ref, mod_ref, wq_ref, wk_ref, wv_ref, q_ref, k_ref, v_ref, h_ref):
    h_ref[...] = _adaln(x_ref[...], g_ref[...], mod_ref[0, 3:4, :], mod_ref[0, 4:5, :]).astype(BF16)
    q_ref[...] = _dot(h_ref[...], wq_ref[...])
    k_ref[...] = _dot(h_ref[...], wk_ref[...])
    v_ref[...] = _dot(h_ref[...], wv_ref[...])


def _swa_qkv(x2, g, mod, w_qkv, *, seq, tm=512):
    t, d = x2.shape
    nq = d
    nk = SW_KV_HEADS * HEAD
    tm = min(tm, seq)
    tiles_per_seq = seq // tm
    wq = w_qkv[:, :nq].astype(BF16)
    wk = w_qkv[:, nq:nq + nk].astype(BF16)
    wv = w_qkv[:, nq + nk:].astype(BF16)
    const = lambda i: (0, 0)
    row = lambda i: (i, 0)
    res = dict(pipeline_mode=pl.Buffered(1))
    widths = (nq, nk, nk)
    return pl.pallas_call(
        _swa_qkv_kernel,
        grid=(t // tm,),
        in_specs=[pl.BlockSpec((tm, d), row), pl.BlockSpec((1, d), const),
                  pl.BlockSpec((1, N_SUB * 3, d), lambda i: (i // tiles_per_seq, 0, 0))]
        + [pl.BlockSpec((d, n), const, **res) for n in widths],
        out_specs=[pl.BlockSpec((tm, n), row) for n in widths],
        out_shape=[jax.ShapeDtypeStruct((t, n), F32) for n in widths],
        scratch_shapes=[pltpu.VMEM((tm, d), BF16)],
        compiler_params=_cparams(("parallel",)),
        name="swa_qkv_proj",
    )(x2, g.reshape(1, d), mod, wq, wk, wv)


def _swa_kernel(q_ref, kc_ref, kp_ref, vc_ref, vp_ref, qn_ref, kn_ref, sink_ref, o_ref):
    T = SW_BLOCK
    wk = kc_ref.shape[2]
    first_key = jnp.where(pl.program_id(1) > 0, 0, T)

    def head_norm(x, g):
        w = x.shape[1]
        head_ones = jnp.where((_iota((w, w), 0) >> 6) == (_iota((w, w), 1) >> 6), 1.0, 0.0).astype(BF16)
        ms = _dot((x * x).astype(BF16), head_ones) * (1.0 / HEAD)
        return x * lax.rsqrt(ms + NORM_EPS) * g

    kcat = jnp.concatenate([kp_ref[0], kc_ref[0]], axis=0)
    kcat = head_norm(kcat, kn_ref[...]).astype(BF16)
    vcat = jnp.concatenate([vp_ref[0], vc_ref[0]], axis=0).astype(BF16)

    gq = q_ref.shape[2] // wk
    qi = _iota((gq * T, 2 * T), 0) & (T - 1)
    si = _iota((gq * T, 2 * T), 1)
    rel = qi + T - si
    mask = (rel >= 0) & (rel < T) & (si >= first_key)
    shared = (kcat, vcat, mask, head_norm)
    _run_interleaved([_swa_chain(kv, shared, q_ref, qn_ref, sink_ref, o_ref) for kv in range(wk // HEAD)])


def _swa_chain(kv, shared, q_ref, qn_ref, sink_ref, o_ref):
    T = SW_BLOCK
    kcat, vcat, mask, head_norm = shared
    wk = kcat.shape[1]
    gq = q_ref.shape[2] // wk
    wq = gq * HEAD
    scale = HEAD ** -0.5
    lo, hi = kv * wq, (kv + 1) * wq

    rep = jnp.where(_iota((wk, wq), 0) == kv * HEAD + (_iota((wk, wq), 1) & (HEAD - 1)), 1.0, 0.0).astype(BF16)
    k_rep = _dot(kcat, rep).astype(BF16)
    v_rep = _dot(vcat, rep).astype(BF16)
    q = head_norm(q_ref[0, :, lo:hi], qn_ref[...]).astype(BF16)
    yield
    rows = _iota((gq * T, wq), 0)
    cols = _iota((gq * T, wq), 1)
    own = (rows >> 7) == (cols >> 6)
    q_s = jnp.where(own, jnp.concatenate([q] * gq, axis=0), jnp.zeros((), BF16))
    s = _dot_nt(q_s, k_rep) * scale
    yield
    s = jnp.where(mask, s, NEG_INF)
    sink = jnp.concatenate(
        [jnp.broadcast_to(sink_ref[0:1, kv * gq + j:kv * gq + j + 1], (T, 1)) for j in range(gq)], axis=0)
    m = jnp.maximum(jnp.max(s, axis=-1, keepdims=True), sink)
    yield
    p = jnp.exp(s - m)
    denom = jnp.sum(p, axis=-1, keepdims=True) + jnp.exp(sink - m)
    yield
    o_s = _dot((p * (1.0 / denom)).astype(BF16), v_rep)
    yield
    o_s = jnp.where(own, o_s, 0.0)
    o = o_s[0:T]
    for j in range(1, gq):
        o = o + o_s[j * T:(j + 1) * T]
    o_ref[0, :, lo:hi] = o


def _swa(q, k, v, q_norm, k_norm, sinks):
    bsz, seq, dq = q.shape
    dk = k.shape[2]
    nq = dq // HEAD
    cur = lambda b, i: (b, i, 0)
    prev = lambda b, i: (b, jnp.maximum(i - 1, 0), 0)
    const = lambda b, i: (0, 0)
    qn = jnp.tile(q_norm, dq // SW_KV_HEADS // HEAD).reshape(1, -1)
    kn = jnp.tile(k_norm, dk // HEAD).reshape(1, dk)
    return pl.pallas_call(
        _swa_kernel,
        grid=(bsz, seq // SW_BLOCK),
        in_specs=[pl.BlockSpec((1, SW_BLOCK, dq), cur),
                  pl.BlockSpec((1, SW_BLOCK, dk), cur), pl.BlockSpec((1, SW_BLOCK, dk), prev),
                  pl.BlockSpec((1, SW_BLOCK, dk), cur), pl.BlockSpec((1, SW_BLOCK, dk), prev),
                  pl.BlockSpec(qn.shape, const), pl.BlockSpec((1, dk), const),
                  pl.BlockSpec((1, nq), const)],
        out_specs=pl.BlockSpec((1, SW_BLOCK, dq), cur),
        out_shape=jax.ShapeDtypeStruct((bsz, seq, dq), F32),
        compiler_params=_cparams(("parallel", "parallel")),
        name="swa_sink_attention",
    )(q, k, k, v, v, qn, kn, sinks.reshape(1, nq))


def kernel(x, c, ada_w, ada_b, norm_g, ffn_w_in, ffn_w_out, rw_mu, rw_w_rkv, rw_w_o, rw_w0, rw_w1, rw_w2, rw_a0, rw_a1, rw_a2, rw_g1, rw_g2, rw_k_k, rw_k_a, rw_r_k, rw_ln_w, rw_ln_b, rw_v0, rw_v1, rw_v2, mb_w_in, mb_conv_w, mb_conv_b, mb_dt_bias, mb_A_log, mb_D, mb_norm_g, mb_w_out, sw_w_qkv, sw_q_norm, sw_k_norm, sw_sinks, sw_w_o):
    bsz, seq, d = x.shape
    depth = ada_w.shape[0]
    t = bsz * seq
    mods = _modulation(c, ada_w, ada_b)
    x2 = x.reshape(t, d)
    v_first = None
    for i in range(depth):
        mod = mods[i]
        x2 = _ffn(x2, norm_g[i, 0], mod, ffn_w_in[i, 0].astype(BF16), ffn_w_out[i, 0].astype(BF16),
                  sub=0, seq=seq)
        kind, j = i % 3, i // 3
        if kind == 0:
            vres = None if v_first is None else (rw_v0[j - 1], rw_v1[j - 1], rw_v2[j - 1])
            r, k, v, ld, a, gt = _rwkv_pre(
                x2, norm_g[i, 1], mod, rw_mu[j], rw_w_rkv[j], rw_w0[j], rw_w1[j], rw_w2[j],
                rw_a0[j], rw_a1[j], rw_a2[j], rw_g1[j], rw_g2[j], vres, v_first, seq=seq)
            if v_first is None:
                v_first = v
            sh = (bsz, seq, d)
            z = _rwkv_scan(r.reshape(sh), k.reshape(sh), v.reshape(sh), ld.reshape(sh), a.reshape(sh),
                           gt.reshape(sh), rw_k_k[j], rw_k_a[j], rw_r_k[j], rw_ln_w[j], rw_ln_b[j])
            x2 = _proj_residual(x2, z.reshape(t, d), mod, rw_w_o[j].astype(BF16), seq=seq)
        elif kind == 1:
            d_inner = mb_w_out.shape[1]
            z, xs, bm, cm, dt = _mamba_in(x2, norm_g[i, 1], mod, mb_w_in[j], seq=seq, d_inner=d_inner)
            y = _ssd(xs.reshape(bsz, seq, -1), bm.reshape(bsz, seq, -1), cm.reshape(bsz, seq, -1),
                     z.reshape(bsz, seq, -1), dt.reshape(bsz, seq, -1), mb_conv_w[j], mb_conv_b[j],
                     mb_dt_bias[j], mb_A_log[j], mb_D[j], mb_norm_g[j])
            x2 = _proj_residual(x2, y.reshape(t, d_inner), mod, mb_w_out[j].astype(BF16), seq=seq)
        else:
            q, k, v = _swa_qkv(x2, norm_g[i, 1], mod, sw_w_qkv[j], seq=seq)
            o = _swa(q.reshape(bsz, seq, -1), k.reshape(bsz, seq, -1), v.reshape(bsz, seq, -1),
                     sw_q_norm[j], sw_k_norm[j], sw_sinks[j])
            x2 = _proj_residual(x2, o.reshape(t, -1), mod, sw_w_o[j].astype(BF16), seq=seq)
        x2 = _ffn(x2, norm_g[i, 2], mod, ffn_w_in[i, 1].astype(BF16), ffn_w_out[i, 1].astype(BF16),
                  sub=2, seq=seq)
    return x2.reshape(bsz, seq, d)
```

```python
import functools

import jax
import jax.numpy as jnp
from jax import lax
from jax.experimental import pallas as pl
from jax.experimental.pallas import tpu as pltpu

F32 = jnp.float32
BF16 = jnp.bfloat16

NORM_EPS = 1e-6
MACARON_W = 0.5
N_SUB = 3
HEAD = 64
RW_GN_EPS = 64e-5
RW_CHUNK = 64
RW_GROUP = 2
MB_GROUPS = 8
MB_STATE = 128
MB_CONV = 4
MB_CHUNK = 128
SW_KV_HEADS = 4
SW_BLOCK = 128
NEG_INF = -1e30
LANE = 128
VMEM_LIMIT = 56 * 1024 * 1024


def _cparams(sem):
    return pltpu.CompilerParams(dimension_semantics=sem, vmem_limit_bytes=VMEM_LIMIT)


def _dot(a, b):
    return jnp.dot(a, b, preferred_element_type=F32)


def _dot_nt(a, b):
    return lax.dot_general(a, b, (((1,), (1,)), ((), ())), preferred_element_type=F32)


def _dot_tn(a, b):
    return lax.dot_general(a, b, (((0,), (0,)), ((), ())), preferred_element_type=F32)


def _split3(x):
    hi = x.astype(BF16)
    r1 = x - hi.astype(F32)
    mid = r1.astype(BF16)
    lo = (r1 - mid.astype(F32)).astype(BF16)
    return hi, mid, lo


def _dot_exact_lhs(m, x):
    hi, mid, lo = _split3(x)
    return _dot(m, hi) + _dot(m, mid) + _dot(m, lo)


def _sigmoid(x):
    return 1.0 / (1.0 + jnp.exp(-x))


def _silu(x):
    return x * _sigmoid(x)


def _softplus(x):
    return jnp.maximum(x, 0.0) + jnp.log(1.0 + jnp.exp(-jnp.abs(x)))


def _adaln(x, g, shift, scale):
    ms = jnp.mean(x * x, axis=-1, keepdims=True)
    return (x * lax.rsqrt(ms + NORM_EPS)) * g * (1.0 + scale) + shift


def _iota(shape, axis):
    return lax.broadcasted_iota(jnp.int32, shape, axis)


def _shift_rows(cur, tail, s):
    rolled = pltpu.roll(cur, s, 0)
    head = jnp.where(_iota(tail.shape, 0) < s, pltpu.roll(tail, s, 0), rolled[0:8])
    return jnp.concatenate([head, rolled[8:]], axis=0)


def _run_interleaved(chains):
    while chains:
        chains = [ch for ch in chains if next(ch, "done") != "done"]


def _mod_kernel(c_ref, w_ref, b_ref, o_ref):
    ca = _silu(c_ref[...])
    a_hi, a_mid, a_lo = _split3(ca)
    w_hi, w_mid, w_lo = _split3(w_ref[0])
    acc = _dot(a_hi, w_hi) + (_dot(a_hi, w_mid) + _dot(a_mid, w_hi))
    acc = acc + (_dot(a_hi, w_lo) + _dot(a_mid, w_mid) + _dot(a_lo, w_hi))
    o_ref[0] = acc + b_ref[0]


def _modulation(c, ada_w, ada_b):
    depth, d, n = ada_w.shape
    bsz = c.shape[0]
    tn = 1152
    out = pl.pallas_call(
        _mod_kernel,
        grid=(depth, n // tn),
        in_specs=[
            pl.BlockSpec((bsz, d), lambda l, j: (0, 0)),
            pl.BlockSpec((1, d, tn), lambda l, j: (l, 0, j)),
            pl.BlockSpec((1, 1, tn), lambda l, j: (l, 0, j)),
        ],
        out_specs=pl.BlockSpec((1, bsz, tn), lambda l, j: (l, 0, j)),
        out_shape=jax.ShapeDtypeStruct((depth, bsz, n), F32),
        compiler_params=_cparams(("parallel", "parallel")),
        name="adaln_modulation",
    )(c, ada_w, ada_b.reshape(depth, 1, n))
    return out.reshape(depth, bsz, N_SUB * 3, d)


def _ffn_kernel(x_ref, g_ref, mod_ref, win_ref, wout_ref, o_ref, h_ref, acc_ref, *, sub, fc):
    dff = wout_ref.shape[0]
    x = x_ref[...]
    shift = mod_ref[0, 3 * sub:3 * sub + 1, :]
    scale = mod_ref[0, 3 * sub + 1:3 * sub + 2, :]
    gate = mod_ref[0, 3 * sub + 2:3 * sub + 3, :]
    h_ref[...] = _adaln(x, g_ref[...], shift, scale).astype(BF16)
    for c in range(dff // fc):
        h = h_ref[...]
        a = _dot(h, win_ref[:, c * fc:(c + 1) * fc])
        b = _dot(h, win_ref[:, dff + c * fc:dff + (c + 1) * fc])
        act = (_silu(a) * b).astype(BF16)
        part = _dot(act, wout_ref[c * fc:(c + 1) * fc, :])
        if c == 0:
            acc_ref[...] = part
        else:
            acc_ref[...] += part
    o_ref[...] = x + (MACARON_W * gate) * acc_ref[...]


def _ffn(x2, g, mod, w_in, w_out, *, sub, seq, tm=1024, fc=256):
    t, d = x2.shape
    dff = w_out.shape[0]
    tm = min(tm, seq)
    tiles_per_seq = seq // tm
    resident = dict(pipeline_mode=pl.Buffered(1))
    return pl.pallas_call(
        functools.partial(_ffn_kernel, sub=sub, fc=fc),
        grid=(t // tm,),
        in_specs=[
            pl.BlockSpec((tm, d), lambda i: (i, 0)),
            pl.BlockSpec((1, d), lambda i: (0, 0)),
            pl.BlockSpec((1, N_SUB * 3, d), lambda i: (i // tiles_per_seq, 0, 0)),
            pl.BlockSpec((d, 2 * dff), lambda i: (0, 0), **resident),
            pl.BlockSpec((dff, d), lambda i: (0, 0), **resident),
        ],
        out_specs=pl.BlockSpec((tm, d), lambda i: (i, 0)),
        out_shape=jax.ShapeDtypeStruct((t, d), F32),
        scratch_shapes=[pltpu.VMEM((tm, d), BF16), pltpu.VMEM((tm, d), F32)],
        compiler_params=_cparams(("parallel",)),
        name="macaron_ffn",
    )(x2, g.reshape(1, d), mod, w_in, w_out)


def _proj_res_kernel(x_ref, z_ref, mod_ref, w_ref, o_ref):
    gate = mod_ref[0, 5:6, :]
    o_ref[...] = x_ref[...] + gate * _dot(z_ref[...].astype(BF16), w_ref[...])


def _proj_residual(x2, z2, mod, w, *, seq, tm=512):
    t, d = x2.shape
    kdim = z2.shape[1]
    tm = min(tm, seq)
    tiles_per_seq = seq // tm
    return pl.pallas_call(
        _proj_res_kernel,
        grid=(t // tm,),
        in_specs=[
            pl.BlockSpec((tm, d), lambda i: (i, 0)),
            pl.BlockSpec((tm, kdim), lambda i: (i, 0)),
            pl.BlockSpec((1, N_SUB * 3, d), lambda i: (i // tiles_per_seq, 0, 0)),
            pl.BlockSpec((kdim, d), lambda i: (0, 0), pipeline_mode=pl.Buffered(1)),
        ],
        out_specs=pl.BlockSpec((tm, d), lambda i: (i, 0)),
        out_shape=jax.ShapeDtypeStruct((t, d), F32),
        compiler_params=_cparams(("parallel",)),
        name="mixer_out_proj",
    )(x2, z2, mod, w)


def _rwkv_pre_kernel(*refs, tiles_per_seq, has_vres):
    if has_vres:
        (x_ref, xp_ref, g_ref, mod_ref, mu_ref, wr_ref, wk_ref, wv_ref, w0_ref, w1_ref, w2_ref,
         a0_ref, a1_ref, a2_ref, g1_ref, g2_ref, v0_ref, v1_ref, v2_ref, vf_ref,
         r_ref, k_ref, v_ref, ld_ref, a_ref, gt_ref, h_ref, dx_ref) = refs
    else:
        (x_ref, xp_ref, g_ref, mod_ref, mu_ref, wr_ref, wk_ref, wv_ref, w0_ref, w1_ref, w2_ref,
         a0_ref, a1_ref, a2_ref, g1_ref, g2_ref,
         r_ref, k_ref, v_ref, ld_ref, a_ref, gt_ref, h_ref, dx_ref) = refs
    shift = mod_ref[0, 3:4, :]
    scale = mod_ref[0, 4:5, :]
    g = g_ref[...]
    h = _adaln(x_ref[...], g, shift, scale)
    first = (pl.program_id(0) % tiles_per_seq) == 0
    hp = _adaln(xp_ref[...], g, shift, scale)
    hp = jnp.where(first, 0.0, hp)
    h_ref[...] = h
    dx_ref[...] = _shift_rows(h, hp, 1) - h

    def mixed(j):
        return (h_ref[...] + dx_ref[...] * mu_ref[j:j + 1, :]).astype(BF16)

    r_ref[...] = _dot(mixed(0), wr_ref[...]).astype(r_ref.dtype)
    k_ref[...] = _dot(mixed(2), wk_ref[...]).astype(k_ref.dtype)
    xv = mixed(3)
    v = _dot(xv, wv_ref[...])
    if has_vres:
        mixv = _sigmoid(v0_ref[...] + _dot(_dot(xv, v1_ref[...]).astype(BF16), v2_ref[...]))
        v = v + (vf_ref[...].astype(F32) - v) * mixv
    v_ref[...] = v.astype(v_ref.dtype)
    wl = w0_ref[...] + _dot(jnp.tanh(_dot(mixed(1), w1_ref[...])).astype(BF16), w2_ref[...])
    wl = -_softplus(-wl) - 0.5
    ld_ref[...] = -jnp.exp(wl)
    a = _sigmoid(a0_ref[...] + _dot(_dot(mixed(4), a1_ref[...]).astype(BF16), a2_ref[...]))
    a_ref[...] = a.astype(a_ref.dtype)
    gt_ref[...] = _dot(_sigmoid(_dot(mixed(5), g1_ref[...])).astype(BF16), g2_ref[...]).astype(gt_ref.dtype)


def _pad_cols(w, n):
    return jnp.pad(w, ((0, 0), (0, n - w.shape[1])))


def _pad_rows(w, n):
    return jnp.pad(w, ((0, n - w.shape[0]), (0, 0)))


def _round_up(n, m):
    return (n + m - 1) // m * m


def _lora_pair(w_a, w_b):
    n = _round_up(w_a.shape[1], LANE)
    return _pad_cols(w_a, n).astype(BF16), _pad_rows(w_b, n).astype(BF16)


def _rwkv_pre(x2, g, mod, mu, w_rkv, w0, w1, w2, a0, a1, a2, g1, g2, vres, v_first, *, seq, tm=512):
    t, d = x2.shape
    tm = min(tm, seq)
    tiles_per_seq = seq // tm
    has_vres = vres is not None
    const = lambda i: (0, 0)
    res = dict(pipeline_mode=pl.Buffered(1))
    row = lambda i: (i, 0)
    w1p, w2p = _lora_pair(w1, w2)
    a1p, a2p = _lora_pair(a1, a2)
    g1p, g2p = _lora_pair(g1, g2)
    args = [x2, x2, g.reshape(1, d), mod, _pad_rows(mu, 8),
            w_rkv[0].astype(BF16), w_rkv[1].astype(BF16), w_rkv[2].astype(BF16),
            w0.reshape(1, d), w1p, w2p, a0.reshape(1, d), a1p, a2p, g1p, g2p]
    in_specs = [
        pl.BlockSpec((tm, d), row),
        pl.BlockSpec((8, d), lambda i: (jnp.maximum(i * (tm // 8) - 1, 0), 0)),
        pl.BlockSpec((1, d), const),
        pl.BlockSpec((1, N_SUB * 3, d), lambda i: (i // tiles_per_seq, 0, 0)),
        pl.BlockSpec((8, d), const),
        pl.BlockSpec((d, d), const, **res), pl.BlockSpec((d, d), const, **res),
        pl.BlockSpec((d, d), const, **res),
        pl.BlockSpec((1, d), const),
        pl.BlockSpec(w1p.shape, const, **res), pl.BlockSpec(w2p.shape, const, **res),
        pl.BlockSpec((1, d), const),
        pl.BlockSpec(a1p.shape, const, **res), pl.BlockSpec(a2p.shape, const, **res),
        pl.BlockSpec(g1p.shape, const, **res), pl.BlockSpec(g2p.shape, const, **res),
    ]
    if has_vres:
        v0, v1, v2 = vres
        v1p, v2p = _lora_pair(v1, v2)
        args += [v0.reshape(1, d), v1p, v2p, v_first]
        in_specs += [pl.BlockSpec((1, d), const), pl.BlockSpec(v1p.shape, const, **res),
                     pl.BlockSpec(v2p.shape, const, **res), pl.BlockSpec((tm, d), row)]
    out_dtypes = (BF16, BF16, BF16, F32, BF16, BF16)
    return pl.pallas_call(
        functools.partial(_rwkv_pre_kernel, tiles_per_seq=tiles_per_seq, has_vres=has_vres),
        grid=(t // tm,),
        in_specs=in_specs,
        out_specs=[pl.BlockSpec((tm, d), row)] * 6,
        out_shape=[jax.ShapeDtypeStruct((t, d), dt_) for dt_ in out_dtypes],
        scratch_shapes=[pltpu.VMEM((tm, d), F32), pltpu.VMEM((tm, d), F32)],
        compiler_params=_cparams(("parallel",)),
        name="rwkv7_projections",
    )(*args)


def _rwkv_scan_kernel(*refs):
    st_ref = refs[-1]

    @pl.when(pl.program_id(2) == 0)
    def _():
        st_ref[...] = jnp.zeros_like(st_ref)

    _run_interleaved([_rwkv_chain(gi, *refs) for gi in range(st_ref.shape[0])])


def _rwkv_chain(gi, r_ref, k_ref, v_ref, ld_ref, a_ref, gt_ref, kk_ref, ka_ref, rk_ref,
                lnw_ref, lnb_ref, z_ref, st_ref):
    L = RW_CHUNK
    W = RW_GROUP * HEAD
    R = RW_GROUP * L
    lo, hi = gi * W, (gi + 1) * W

    r = r_ref[0, :, lo:hi].astype(F32)
    k = k_ref[0, :, lo:hi].astype(F32)
    v = v_ref[0, :, lo:hi].astype(F32)
    ld = ld_ref[0, :, lo:hi]
    a = a_ref[0, :, lo:hi].astype(F32)

    head_ones = jnp.where((_iota((W, W), 0) >> 6) == (_iota((W, W), 1) >> 6), 1.0, 0.0).astype(BF16)

    def head_sum(x, passes=1):
        hi_part = x.astype(BF16)
        out = _dot(hi_part, head_ones)
        if passes == 2:
            out = out + _dot((x - hi_part.astype(F32)).astype(BF16), head_ones)
        return out

    kkn = k * kk_ref[:, lo:hi]
    kk_ss = head_sum(kkn * kkn)
    tril = jnp.where(_iota((L, L), 0) >= _iota((L, L), 1), 1.0, 0.0).astype(BF16)
    cum = _dot_exact_lhs(tril, ld)
    yield
    kk = kkn / jnp.maximum(jnp.sqrt(kk_ss), 1e-12)
    kmod = k * (1.0 + (a - 1.0) * ka_ref[:, lo:hi])
    c_end = cum[L - 1:L, :]
    e_pos = jnp.exp(cum)
    e_neg = jnp.exp(-cum)
    e_prev = jnp.exp(cum - ld)
    e_end = jnp.exp(c_end - cum)
    kka = kk * a

    rows = _iota((R, W), 0)
    cols = _iota((R, W), 1)
    same_head = (rows >> 6) == (cols >> 6)

    def stack(x):
        xb = x.astype(BF16)
        return jnp.where(same_head, jnp.concatenate([xb] * RW_GROUP, axis=0), jnp.zeros((), BF16))

    ar_s = jnp.concatenate([stack(-kk * e_prev), stack(r * e_pos)], axis=0)
    bk_s = jnp.concatenate([stack(kka * e_neg), stack(kmod * e_neg)], axis=0)
    v_s = stack(v)
    bp_s = stack(kka * e_end)
    kp_s = stack(kmod * e_end)

    mr = _iota((R, R), 0)
    mc = _iota((R, R), 1)
    strict = (mr & (L - 1)) > (mc & (L - 1))
    incl = (mr & (L - 1)) >= (mc & (L - 1))
    prod = _dot_nt(ar_s, bk_s)
    st = st_ref[gi]
    from_state = _dot_nt(ar_s, st.astype(BF16))
    yield
    a_ab = jnp.where(strict, prod[0:R, 0:R], 0.0)
    a_ak = jnp.where(strict, prod[0:R, R:2 * R], 0.0)
    a_rb = jnp.where(incl, prod[R:2 * R, 0:R], 0.0)
    a_rk = jnp.where(incl, prod[R:2 * R, R:2 * R], 0.0)

    def mm(x, y):
        return _dot(x.astype(BF16), y.astype(BF16))

    eye = jnp.where(mr == mc, 1.0, 0.0)
    a_d = jnp.where((mr >> 3) == (mc >> 3), a_ab, 0.0)
    a_d2 = mm(a_d, a_d)
    rhs = from_state[0:R] + mm(a_ak, v_s)
    bonus = head_sum(r * kmod * rk_ref[:, lo:hi]) * v
    yield
    a_d4 = mm(a_d2, a_d2)
    tinv = mm(eye + a_d, eye + a_d2)
    yield
    tinv = mm(tinv, eye + a_d4)
    yield
    for sh in (3, 4, 5):
        off =((mr >> (sh + 1)) == (mc >> (sh + 1))) & ((mr >> sh) != (mc >> sh))
        half = mm(tinv, jnp.where(off, a_ab, 0.0))
        yield
        tinv = tinv + mm(half, tinv)
        yield

    u_b = mm(tinv, rhs).astype(BF16)
    yield
    uv = jnp.concatenate([u_b, v_s], axis=0)
    y_s = from_state[R:2 * R] + _dot(
        jnp.concatenate([a_rb.astype(BF16), a_rk.astype(BF16)], axis=1), uv)
    st_ref[gi] = st * jnp.exp(c_end) + _dot_tn(uv, jnp.concatenate([bp_s, kp_s], axis=0))
    yield
    y = y_s[0:L]
    for i in range(1, RW_GROUP):
        y = y + y_s[i * L:(i + 1) * L]
    mean = head_sum(y, passes=2) * (1.0 / HEAD)
    yield
    yc = y - mean
    var = head_sum(yc * yc) * (1.0 / HEAD)
    yield
    yn = yc * lax.rsqrt(var + RW_GN_EPS) * lnw_ref[:, lo:hi] + lnb_ref[:, lo:hi]
    z_ref[0, :, lo:hi] = ((yn + bonus) * gt_ref[0, :, lo:hi].astype(F32)).astype(z_ref.dtype)


def _rwkv_scan(r, k, v, ld, a, gt, k_k, k_a, r_k, ln_w, ln_b, *, lanes=1024):
    bsz, seq, d = r.shape
    w = RW_GROUP * HEAD
    lanes = min(lanes, d)
    act = pl.BlockSpec((1, RW_CHUNK, lanes), lambda b, g, c: (b, c, g))
    par = pl.BlockSpec((1, lanes), lambda b, g, c: (0, g))
    return pl.pallas_call(
        _rwkv_scan_kernel,
        grid=(bsz, d // lanes, seq // RW_CHUNK),
        in_specs=[act] * 6 + [par] * 5,
        out_specs=act,
        out_shape=jax.ShapeDtypeStruct((bsz, seq, d), BF16),
        scratch_shapes=[pltpu.VMEM((lanes // w, w, w), F32)],
        compiler_params=_cparams(("parallel", "parallel", "arbitrary")),
        name="rwkv7_chunk_scan",
    )(r, k, v, ld, a, gt, k_k.reshape(1, d), k_a.reshape(1, d), r_k.reshape(1, d),
      ln_w.reshape(1, d), ln_b.reshape(1, d))


def _mamba_in_kernel(x_ref, g_ref, mod_ref, wz_ref, wx_ref, wb_ref, wc_ref, wdt_ref,
                     z_ref, xs_ref, b_ref, c_ref, dt_ref, h_ref):
    h_ref[...] = _adaln(x_ref[...], g_ref[...], mod_ref[0, 3:4, :], mod_ref[0, 4:5, :]).astype(BF16)
    z_ref[...] = _dot(h_ref[...], wz_ref[...]).astype(z_ref.dtype)
    xs_ref[...] = _dot(h_ref[...], wx_ref[...]).astype(xs_ref.dtype)
    b_ref[...] = _dot(h_ref[...], wb_ref[...]).astype(b_ref.dtype)
    c_ref[...] = _dot(h_ref[...], wc_ref[...]).astype(c_ref.dtype)
    dt_ref[...] = _dot(h_ref[...], wdt_ref[...])


def _mamba_in(x2, g, mod, w_in, *, seq, d_inner, tm=512):
    t, d = x2.shape
    gn = MB_GROUPS * MB_STATE
    tm = min(tm, seq)
    tiles_per_seq = seq // tm
    wz = w_in[:, :d_inner].astype(BF16)
    wx = w_in[:, d_inner:2 * d_inner].astype(BF16)
    wb = w_in[:, 2 * d_inner:2 * d_inner + gn].astype(BF16)
    wc = w_in[:, 2 * d_inner + gn:2 * d_inner + 2 * gn].astype(BF16)
    wdt = _pad_cols(w_in[:, 2 * d_inner + 2 * gn:], LANE).astype(BF16)
    const = lambda i: (0, 0)
    row = lambda i: (i, 0)
    res = dict(pipeline_mode=pl.Buffered(1))
    widths = (d_inner, d_inner, gn, gn, LANE)
    return pl.pallas_call(
        _mamba_in_kernel,
        grid=(t // tm,),
        in_specs=[pl.BlockSpec((tm, d), row), pl.BlockSpec((1, d), const),
                  pl.BlockSpec((1, N_SUB * 3, d), lambda i: (i // tiles_per_seq, 0, 0))]
        + [pl.BlockSpec((d, n), const, **res) for n in widths],
        out_specs=[pl.BlockSpec((tm, n), row) for n in widths],
        out_shape=[jax.ShapeDtypeStruct((t, n), dt_)
                   for n, dt_ in zip(widths, (BF16, BF16, BF16, BF16, F32))],
        scratch_shapes=[pltpu.VMEM((tm, d), BF16)],
        compiler_params=_cparams(("parallel",)),
        name="mamba2_in_proj",
    )(x2, g.reshape(1, d), mod, wz, wx, wb, wc, wdt)


def _ssd_kernel(xs_ref, b_ref, c_ref, z_ref, dt_ref, cwx_ref, cwb_ref, cwc_ref, cbx_ref, cbb_ref,
                cbc_ref, dtb_ref, alog_ref, dsk_ref, ng_ref, y_ref,
                st_ref, sx_ref, sb_ref, sc_ref):
    L = MB_CHUNK
    stages = ((sx_ref, xs_ref), (sb_ref, b_ref), (sc_ref, c_ref))

    @pl.when(pl.program_id(1) == 0)
    def _():
        st_ref[...] = jnp.zeros_like(st_ref)
        for s_ref, _ in stages:
            s_ref[0:8, :] = jnp.zeros((8, s_ref.shape[1]), F32)

    for s_ref, cur_ref in stages:
        s_ref[8:8 + L, :] = cur_ref[0].astype(F32)

    dt = _softplus(dt_ref[0] + dtb_ref[...])
    da = dt * (-jnp.exp(alog_ref[...]))
    tril = jnp.where(_iota((L, L), 0) >= _iota((L, L), 1), 1.0, 0.0).astype(BF16)
    acum = _dot_exact_lhs(tril, da)
    shared = (_split3(dt), _split3(acum))
    refs = (sx_ref, sb_ref, sc_ref, z_ref, cwx_ref, cwb_ref, cwc_ref, cbx_ref, cbb_ref, cbc_ref,
            dsk_ref, ng_ref, y_ref, st_ref)
    _run_interleaved([_ssd_chain(g, shared, *refs) for g in range(st_ref.shape[0])])

    for s_ref, _ in stages:
        s_ref[0:8, :] = s_ref[L:L + 8, :]


def _ssd_chain(g, shared, sx_ref, sb_ref, sc_ref, z_ref, cwx_ref, cwb_ref, cwc_ref, cbx_ref,
               cbb_ref, cbc_ref, dsk_ref, ng_ref, y_ref, st_ref):
    L = MB_CHUNK
    n = MB_STATE
    wx = st_ref.shape[2]
    hg = wx // HEAD
    dt3, ac3 = shared
    xl, xh = g * wx, (g + 1) * wx
    nl, nh = g * n, (g + 1) * n

    def conv_silu(s_ref, w_ref, bias_ref, lo, hi):
        acc = bias_ref[:, lo:hi]
        for j in range(MB_CONV):
            off = 8 - (MB_CONV - 1) + j
            acc = acc + s_ref[off:off + L, lo:hi] * w_ref[j:j + 1, lo:hi]
        return _silu(acc)

    xs = conv_silu(sx_ref, cwx_ref, cbx_ref, xl, xh)
    bm = conv_silu(sb_ref, cwb_ref, cbb_ref, nl, nh).astype(BF16)
    cm = conv_silu(sc_ref, cwc_ref, cbc_ref, nl, nh).astype(BF16)

    head0 = hg * g
    sel_x = jnp.where(_iota((LANE, wx), 0) == head0 + (_iota((LANE, wx), 1) >> 6), 1.0, 0.0).astype(BF16)
    sel_r = jnp.where(_iota((8, LANE), 1) == head0 + _iota((8, LANE), 0), 1.0, 0.0).astype(BF16)
    sel_c = jnp.where(_iota((LANE, hg * L), 0) == head0 + (_iota((LANE, hg * L), 1) >> 7), 1.0, 0.0).astype(BF16)
    both = [jnp.concatenate([d_p, a_p], axis=0) for d_p, a_p in zip(dt3, ac3)]
    in_x = _dot(both[0], sel_x) + _dot(both[1], sel_x) + _dot(both[2], sel_x)
    dt_x, ac_x = in_x[0:L], in_x[L:2 * L]
    ac_row = _dot_nt(sel_r, ac3[0]) + _dot_nt(sel_r, ac3[1]) + _dot_nt(sel_r, ac3[2])
    ac_col = _dot(ac3[0], sel_c) + _dot(ac3[1], sel_c) + _dot(ac3[2], sel_c)
    cb = _dot_nt(cm, bm)
    st = st_ref[g]
    y_off = _dot(cm, st.astype(BF16))
    yield

    xdt = xs * dt_x
    causal = _iota((L, L), 0) >= _iota((L, L), 1)
    lane_head = _iota((L, wx), 1) >> 6
    xdt_b = xdt.astype(BF16)
    g_parts = []
    x_parts = []
    for j in range(hg):
        seg = jnp.where(causal, ac_col[:, j * L:(j + 1) * L] - ac_row[j:j + 1, :], -jnp.inf)
        g_parts.append((cb * jnp.exp(seg)).astype(BF16))
        x_parts.append(jnp.where(lane_head == j, xdt_b, jnp.zeros((), BF16)))
    y_diag = _dot(jnp.concatenate(g_parts, axis=1), jnp.concatenate(x_parts, axis=0))
    ac_last = ac_x[L - 1:L, :]
    xdec = (xdt * jnp.exp(ac_last - ac_x)).astype(BF16)
    st_ref[g] = st * jnp.exp(ac_last) + _dot_tn(bm, xdec)
    yield

    y = y_diag + y_off * jnp.exp(ac_x) + xs * dsk_ref[:, xl:xh]
    y = y * _silu(z_ref[0, :, xl:xh].astype(F32))
    ms = jnp.mean(y * y, axis=-1, keepdims=True)
    y_ref[0, :, xl:xh] = (y * lax.rsqrt(ms + NORM_EPS) * ng_ref[:, xl:xh]).astype(y_ref.dtype)


def _ssd(xs, bm, cm, z, dt, conv_w, conv_b, dt_bias, a_log, d_skip, norm_g):
    bsz, seq, d_inner = xs.shape
    n = MB_STATE
    gn = MB_GROUPS * n
    wx = d_inner // MB_GROUPS
    heads = d_inner // HEAD
    cw = _pad_rows(conv_w, 8)
    cwx, cwb, cwc = cw[:, :d_inner], cw[:, d_inner:d_inner + gn], cw[:, d_inner + gn:]
    cb = conv_b.reshape(1, -1)
    cbx, cbb, cbc = cb[:, :d_inner], cb[:, d_inner:d_inner + gn], cb[:, d_inner + gn:]
    dtb = _pad_cols(dt_bias.reshape(1, heads), LANE)
    alog = _pad_cols(a_log.reshape(1, heads), LANE)
    dsk = jnp.repeat(d_skip, HEAD).reshape(1, d_inner)
    tok = lambda w: pl.BlockSpec((1, MB_CHUNK, w), lambda b, c: (b, c, 0))
    par = lambda r, w: pl.BlockSpec((r, w), lambda b, c: (0, 0))
    return pl.pallas_call(
        _ssd_kernel,
        grid=(bsz, seq // MB_CHUNK),
        in_specs=[tok(d_inner), tok(gn), tok(gn), tok(d_inner), tok(LANE),
                  par(8, d_inner), par(8, gn), par(8, gn), par(1, d_inner), par(1, gn), par(1, gn),
                  par(1, LANE), par(1, LANE), par(1, d_inner), par(1, d_inner)],
        out_specs=tok(d_inner),
        out_shape=jax.ShapeDtypeStruct((bsz, seq, d_inner), BF16),
        scratch_shapes=[pltpu.VMEM((MB_GROUPS, n, wx), F32),
                        pltpu.VMEM((8 + MB_CHUNK,d_inner), F32),
                        pltpu.VMEM((8 + MB_CHUNK,gn), F32),
                        pltpu.VMEM((8 + MB_CHUNK,gn), F32)],
        compiler_params=_cparams(("parallel", "arbitrary")),
        name="mamba2_conv_ssd",
    )(xs, bm, cm, z, dt, cwx, cwb, cwc, cbx, cbb, cbc, dtb, alog, dsk, norm_g.reshape(1, d_inner))


def _swa_qkv_kernel(x_ref, g_ref, mod_ref, wq_ref, wk_ref, wv_ref, q_ref, k_ref, v_ref, h_ref):
    h_ref[...] = _adaln(x_ref[...], g_ref[...], mod_ref[0, 3:4, :], mod_ref[0, 4:5, :]).astype(BF16)
    q_ref[...] = _dot(h_ref[...], wq_ref[...]).astype(q_ref.dtype)
    k_ref[...] = _dot(h_ref[...], wk_ref[...]).astype(k_ref.dtype)
    v_ref[...] = _dot(h_ref[...], wv_ref[...]).astype(v_ref.dtype)


def _swa_qkv(x2, g, mod, w_qkv, *, seq, tm=512):
    t, d = x2.shape
    nq = d
    nk = SW_KV_HEADS * HEAD
    tm = min(tm, seq)
    tiles_per_seq = seq // tm
    wq = w_qkv[:, :nq].astype(BF16)
    wk = w_qkv[:, nq:nq + nk].astype(BF16)
    wv = w_qkv[:, nq + nk:].astype(BF16)
    const = lambda i: (0, 0)
    row = lambda i: (i, 0)
    res = dict(pipeline_mode=pl.Buffered(1))
    widths = (nq, nk, nk)
    return pl.pallas_call(
        _swa_qkv_kernel,
        grid=(t // tm,),
        in_specs=[pl.BlockSpec((tm, d), row), pl.BlockSpec((1, d), const),
                  pl.BlockSpec((1, N_SUB * 3, d), lambda i: (i // tiles_per_seq, 0, 0))]
        + [pl.BlockSpec((d, n), const, **res) for n in widths],
        out_specs=[pl.BlockSpec((tm, n), row) for n in widths],
        out_shape=[jax.ShapeDtypeStruct((t, n), BF16) for n in widths],
        scratch_shapes=[pltpu.VMEM((tm, d), BF16)],
        compiler_params=_cparams(("parallel",)),
        name="swa_qkv_proj",
    )(x2, g.reshape(1, d), mod, wq, wk, wv)


def _swa_kernel(q_ref, kc_ref, kp_ref, vc_ref, vp_ref, qn_ref, kn_ref, sink_ref, o_ref):
    T = SW_BLOCK
    wk = kc_ref.shape[2]
    first_key = jnp.where(pl.program_id(1) > 0, 0, T)

    def head_norm(x, g):
        w = x.shape[1]
        head_ones = jnp.where((_iota((w, w), 0) >> 6) == (_iota((w, w), 1) >> 6), 1.0, 0.0).astype(BF16)
        ms = _dot((x * x).astype(BF16), head_ones) * (1.0 / HEAD)
        return x * lax.rsqrt(ms + NORM_EPS) * g

    kcat = jnp.concatenate([kp_ref[0], kc_ref[0]], axis=0).astype(F32)
    kcat = head_norm(kcat, kn_ref[...]).astype(BF16)
    vcat = jnp.concatenate([vp_ref[0], vc_ref[0]], axis=0).astype(BF16)

    qi = _iota((T, 2 * T), 0)
    si = _iota((T, 2 * T), 1)
    rel = qi + T - si
    mask = (rel >= 0) & (rel < T) & (si >= first_key)
    shared = (kcat, vcat, mask, si == qi, head_norm)
    _run_interleaved([_swa_chain(kv, shared, q_ref, qn_ref, sink_ref, o_ref) for kv in range(wk // HEAD)])


def _swa_chain(kv, shared, q_ref, qn_ref, sink_ref, o_ref):
    T = SW_BLOCK
    kcat, vcat, mask, sink_slot, head_norm = shared
    wk = kcat.shape[1]
    gq = q_ref.shape[2] // wk
    wq = gq * HEAD
    scale = HEAD ** -0.5
    lo, hi = kv * wq, (kv + 1) * wq

    rep = jnp.where(_iota((wk, wq), 0) == kv * HEAD + (_iota((wk, wq), 1) & (HEAD - 1)), 1.0, 0.0).astype(BF16)
    k_rep = _dot(kcat, rep).astype(BF16)
    v_rep = _dot(vcat, rep).astype(BF16)
    q = head_norm(q_ref[0, :, lo:hi].astype(F32), qn_ref[...]).astype(BF16)
    yield
    lane_head = _iota((T, wq), 1) >> 6
    scores = []
    for j in range(gq):
        q_j = jnp.where(lane_head == j, q, jnp.zeros((), BF16))
        scores.append(_dot_nt(q_j, k_rep) * scale)
        yield
    ones = jnp.ones((2 * T, wq), BF16)
    probs = []
    for j in range(gq):
        sink = sink_ref[0:1, kv * gq + j:kv * gq + j + 1]
        s = jnp.where(sink_slot, sink, jnp.where(mask, scores[j], NEG_INF))
        m = jnp.max(s, axis=-1, keepdims=True)
        yield
        probs.append(jnp.exp(s - m).astype(BF16))
    o = jnp.zeros((T, wq), F32)
    for j in range(gq):
        denom = _dot(probs[j], ones)
        o_j = _dot(jnp.where(sink_slot, jnp.zeros((), BF16), probs[j]), v_rep)
        yield
        o = o + jnp.where(lane_head == j, o_j / denom, 0.0)
    o_ref[0, :, lo:hi] = o.astype(o_ref.dtype)


def _swa(q, k, v, q_norm, k_norm, sinks):
    bsz, seq, dq = q.shape
    dk = k.shape[2]
    nq = dq // HEAD
    cur = lambda b, i: (b, i, 0)
    prev = lambda b, i: (b, jnp.maximum(i - 1, 0), 0)
    const = lambda b, i: (0, 0)
    qn = jnp.tile(q_norm, dq // SW_KV_HEADS // HEAD).reshape(1, -1)
    kn = jnp.tile(k_norm, dk // HEAD).reshape(1, dk)
    return pl.pallas_call(
        _swa_kernel,
        grid=(bsz, seq // SW_BLOCK),
        in_specs=[pl.BlockSpec((1, SW_BLOCK, dq), cur),
                  pl.BlockSpec((1, SW_BLOCK, dk), cur), pl.BlockSpec((1, SW_BLOCK, dk), prev),
                  pl.BlockSpec((1, SW_BLOCK, dk), cur), pl.BlockSpec((1, SW_BLOCK, dk), prev),
                  pl.BlockSpec(qn.shape, const), pl.BlockSpec((1, dk), const),
                  pl.BlockSpec((1, nq), const)],
        out_specs=pl.BlockSpec((1, SW_BLOCK, dq), cur),
        out_shape=jax.ShapeDtypeStruct((bsz, seq, dq), BF16),
        compiler_params=_cparams(("parallel", "parallel")),
        name="swa_sink_attention",
    )(q, k, k, v, v, qn, kn, sinks.reshape(1, nq))


def kernel(x, c, ada_w, ada_b, norm_g, ffn_w_in, ffn_w_out, rw_mu, rw_w_rkv, rw_w_o, rw_w0, rw_w1, rw_w2, rw_a0, rw_a1, rw_a2, rw_g1, rw_g2, rw_k_k, rw_k_a, rw_r_k, rw_ln_w, rw_ln_b, rw_v0, rw_v1, rw_v2, mb_w_in, mb_conv_w, mb_conv_b, mb_dt_bias, mb_A_log, mb_D, mb_norm_g, mb_w_out, sw_w_qkv, sw_q_norm, sw_k_norm, sw_sinks, sw_w_o):
    bsz, seq, d = x.shape
    depth = ada_w.shape[0]
    t = bsz * seq
    mods = _modulation(c, ada_w, ada_b)
    x2 = x.reshape(t, d)
    v_first = None
    for i in range(depth):
        mod = mods[i]
        x2 = _ffn(x2, norm_g[i, 0], mod, ffn_w_in[i, 0].astype(BF16), ffn_w_out[i, 0].astype(BF16),
                  sub=0, seq=seq)
        kind, j = i % 3, i // 3
        if kind == 0:
            vres = None if v_first is None else (rw_v0[j - 1], rw_v1[j - 1], rw_v2[j - 1])
            r, k, v, ld, a, gt = _rwkv_pre(
                x2, norm_g[i, 1], mod, rw_mu[j], rw_w_rkv[j], rw_w0[j], rw_w1[j], rw_w2[j],
                rw_a0[j], rw_a1[j], rw_a2[j], rw_g1[j], rw_g2[j], vres, v_first, seq=seq)
            if v_first is None:
                v_first = v
            sh = (bsz, seq, d)
            z = _rwkv_scan(r.reshape(sh), k.reshape(sh), v.reshape(sh), ld.reshape(sh), a.reshape(sh),
                           gt.reshape(sh), rw_k_k[j], rw_k_a[j], rw_r_k[j], rw_ln_w[j], rw_ln_b[j])
            x2 = _proj_residual(x2, z.reshape(t, d), mod, rw_w_o[j].astype(BF16), seq=seq)
        elif kind == 1:
            d_inner = mb_w_out.shape[1]
            z, xs, bm, cm, dt = _mamba_in(x2, norm_g[i, 1], mod, mb_w_in[j], seq=seq, d_inner=d_inner)
            y = _ssd(xs.reshape(bsz, seq, -1), bm.reshape(bsz, seq, -1), cm.reshape(bsz, seq, -1),
                     z.reshape(bsz, seq, -1), dt.reshape(bsz, seq, -1), mb_conv_w[j], mb_conv_b[j],
                     mb_dt_bias[j], mb_A_log[j], mb_D[j], mb_norm_g[j])
            x2 = _proj_residual(x2, y.reshape(t, d_inner), mod, mb_w_out[j].astype(BF16), seq=seq)
        else:
            q, k, v = _swa_qkv(x2, norm_g[i, 1], mod, sw_w_qkv[j], seq=seq)
            o = _swa(q.reshape(bsz, seq, -1), k.reshape(bsz, seq, -1), v.reshape(bsz, seq, -1),
                     sw_q_norm[j], sw_k_norm[j], sw_sinks[j])
            x2 = _proj_residual(x2, o.reshape(t, -1), mod, sw_w_o[j].astype(BF16), seq=seq)
        x2 = _ffn(x2, norm_g[i, 2], mod, ffn_w_in[i, 1].astype(BF16), ffn_w_out[i, 1].astype(BF16),
                  sub=2, seq=seq)
    return x2.reshape(bsz, seq, d)
```

```python
import functools

import jax
import jax.numpy as jnp
from jax import lax
from jax.experimental import pallas as pl
from jax.experimental.pallas import tpu as pltpu

F32 = jnp.float32
BF16 = jnp.bfloat16

NORM_EPS = 1e-6
MACARON_W = 0.5
N_SUB = 3
HEAD = 64
RW_GN_EPS = 64e-5
RW_CHUNK = 64
RW_GROUP = 2
MB_GROUPS = 8
MB_STATE = 128
MB_CONV = 4
MB_CHUNK = 128
SW_KV_HEADS = 4
SW_BLOCK = 128
NEG_INF = -1e30
LANE = 128
VMEM_LIMIT = 56 * 1024 * 1024


def _cparams(sem):
    return pltpu.CompilerParams(dimension_semantics=sem, vmem_limit_bytes=VMEM_LIMIT)


def _dot(a, b):
    return jnp.dot(a, b, preferred_element_type=F32)


def _dot_nt(a, b):
    return lax.dot_general(a, b, (((1,), (1,)), ((), ())), preferred_element_type=F32)


def _dot_tn(a, b):
    return lax.dot_general(a, b, (((0,), (0,)), ((), ())), preferred_element_type=F32)


def _split3(x):
    hi = x.astype(BF16)
    r1 = x - hi.astype(F32)
    mid = r1.astype(BF16)
    lo = (r1 - mid.astype(F32)).astype(BF16)
    return hi, mid, lo


def _dot_exact_lhs(m, x):
    hi, mid, lo = _split3(x)
    return _dot(m, hi) + _dot(m, mid) + _dot(m, lo)


def _sigmoid(x):
    return 1.0 / (1.0 + jnp.exp(-x))


def _silu(x):
    return x * _sigmoid(x)


def _softplus(x):
    return jnp.maximum(x, 0.0) + jnp.log(1.0 + jnp.exp(-jnp.abs(x)))


def _adaln(x, g, shift, scale):
    ms = jnp.mean(x * x, axis=-1, keepdims=True)
    return (x * lax.rsqrt(ms + NORM_EPS)) * g * (1.0 + scale) + shift


def _iota(shape, axis):
    return lax.broadcasted_iota(jnp.int32, shape, axis)


def _shift_rows(cur, tail, s):
    rolled = pltpu.roll(cur, s, 0)
    head = jnp.where(_iota(tail.shape, 0) < s, pltpu.roll(tail, s, 0), rolled[0:8])
    return jnp.concatenate([head, rolled[8:]], axis=0)


def _run_interleaved(chains):
    while chains:
        chains = [ch for ch in chains if next(ch, "done") != "done"]


def _mod_kernel(c_ref, w_ref, b_ref, o_ref):
    ca = _silu(c_ref[...])
    a_hi, a_mid, a_lo = _split3(ca)
    w_hi, w_mid, w_lo = _split3(w_ref[0])
    acc = _dot(a_hi, w_hi) + (_dot(a_hi, w_mid) + _dot(a_mid, w_hi))
    acc = acc + (_dot(a_hi, w_lo) + _dot(a_mid, w_mid) + _dot(a_lo, w_hi))
    o_ref[0] = acc + b_ref[0]


def _modulation(c, ada_w, ada_b):
    depth, d, n = ada_w.shape
    bsz = c.shape[0]
    tn = 1152
    out = pl.pallas_call(
        _mod_kernel,
        grid=(depth, n // tn),
        in_specs=[
            pl.BlockSpec((bsz, d), lambda l, j: (0, 0)),
            pl.BlockSpec((1, d, tn), lambda l, j: (l, 0, j)),
            pl.BlockSpec((1, 1, tn), lambda l, j: (l, 0, j)),
        ],
        out_specs=pl.BlockSpec((1, bsz, tn), lambda l, j: (l, 0, j)),
        out_shape=jax.ShapeDtypeStruct((depth, bsz, n), F32),
        compiler_params=_cparams(("parallel", "parallel")),
        name="adaln_modulation",
    )(c, ada_w, ada_b.reshape(depth, 1, n))
    return out.reshape(depth, bsz, N_SUB * 3, d)


def _ffn_kernel(*refs, sub, fc, fused_mixer):
    if fused_mixer:
        x_ref, g_ref, mod_ref, win_ref, wout_ref, z_ref, wo_ref, o_ref, h_ref, acc_ref = refs
        o_ref[...] = x_ref[...] + mod_ref[0, 5:6, :] * _dot(z_ref[...], wo_ref[...])
        res_ref = o_ref
    else:
        x_ref, g_ref, mod_ref, win_ref, wout_ref, o_ref, h_ref, acc_ref = refs
        res_ref = x_ref
    dff = wout_ref.shape[0]
    shift = mod_ref[0, 3 * sub:3 * sub + 1, :]
    scale = mod_ref[0, 3 * sub + 1:3 * sub + 2, :]
    gate = mod_ref[0, 3 * sub + 2:3 * sub + 3, :]
    h_ref[...] = _adaln(res_ref[...], g_ref[...], shift, scale).astype(BF16)
    for c in range(dff // fc):
        h = h_ref[...]
        a = _dot(h, win_ref[:, c * fc:(c + 1) * fc])
        b = _dot(h, win_ref[:, dff + c * fc:dff + (c + 1) * fc])
        act = (_silu(a) * b).astype(BF16)
        part = _dot(act, wout_ref[c * fc:(c + 1) * fc, :])
        if c == 0:
            acc_ref[...] = part
        else:
            acc_ref[...] += part
    o_ref[...] = res_ref[...] + (MACARON_W * gate) * acc_ref[...]


def _ffn(x2, g, mod, w_in, w_out, *, sub, seq, mixer=None, tm=None, fc=256):
    t, d = x2.shape
    dff = w_out.shape[0]
    if tm is None:
        tm = 1024 if mixer is None else 512
    tm = min(tm, seq)
    tiles_per_seq = seq // tm
    resident = dict(pipeline_mode=pl.Buffered(1))
    args = [x2, g.reshape(1, d), mod, w_in, w_out]
    in_specs = [
        pl.BlockSpec((tm, d), lambda i: (i, 0)),
        pl.BlockSpec((1, d), lambda i: (0, 0)),
        pl.BlockSpec((1, N_SUB * 3, d), lambda i: (i // tiles_per_seq, 0, 0)),
        pl.BlockSpec((d, 2 * dff), lambda i: (0, 0), **resident),
        pl.BlockSpec((dff, d), lambda i: (0, 0), **resident),
    ]
    if mixer is not None:
        z2, w_o = mixer
        args += [z2, w_o]
        in_specs += [pl.BlockSpec((tm, z2.shape[1]), lambda i: (i, 0)),
                     pl.BlockSpec(w_o.shape, lambda i: (0, 0), **resident)]
    return pl.pallas_call(
        functools.partial(_ffn_kernel, sub=sub, fc=fc, fused_mixer=mixer is not None),
        grid=(t // tm,),
        in_specs=in_specs,
        out_specs=pl.BlockSpec((tm, d), lambda i: (i, 0)),
        out_shape=jax.ShapeDtypeStruct((t, d), F32),
        scratch_shapes=[pltpu.VMEM((tm, d), BF16), pltpu.VMEM((tm, d), F32)],
        compiler_params=_cparams(("parallel",)),
        name="macaron_ffn_mixer_out" if mixer is not None else "macaron_ffn",
    )(*args)


def _rwkv_pre_kernel(*refs, tiles_per_seq, has_vres):
    if has_vres:
        (x_ref, xp_ref, g_ref, mod_ref, mu_ref, wr_ref, wk_ref, wv_ref, w0_ref, w1_ref, w2_ref,
         a0_ref, a1_ref, a2_ref, g1_ref, g2_ref, v0_ref, v1_ref, v2_ref, vf_ref,
         r_ref, k_ref, v_ref, ld_ref, a_ref, gt_ref, h_ref, dx_ref) = refs
    else:
        (x_ref, xp_ref, g_ref, mod_ref, mu_ref, wr_ref, wk_ref, wv_ref, w0_ref, w1_ref, w2_ref,
         a0_ref, a1_ref, a2_ref, g1_ref, g2_ref,
         r_ref, k_ref, v_ref, ld_ref, a_ref, gt_ref, h_ref, dx_ref) = refs
    shift = mod_ref[0, 3:4, :]
    scale = mod_ref[0, 4:5, :]
    g = g_ref[...]
    h = _adaln(x_ref[...], g, shift, scale)
    first = (pl.program_id(0) % tiles_per_seq) == 0
    hp = _adaln(xp_ref[...], g, shift, scale)
    hp = jnp.where(first, 0.0, hp)
    h_ref[...] = h
    dx_ref[...] = _shift_rows(h, hp, 1) - h

    def mixed(j):
        return (h_ref[...] + dx_ref[...] * mu_ref[j:j + 1, :]).astype(BF16)

    wl = w0_ref[...] + _dot(jnp.tanh(_dot(mixed(1), w1_ref[...])).astype(BF16), w2_ref[...])
    a_pre = a0_ref[...] + _dot(_dot(mixed(4), a1_ref[...]).astype(BF16), a2_ref[...])
    r_ref[...] = _dot(mixed(0), wr_ref[...]).astype(r_ref.dtype)
    wl = -_softplus(-wl) - 0.5
    ld_ref[...] = -jnp.exp(wl)
    g_mid = _sigmoid(_dot(mixed(5), g1_ref[...])).astype(BF16)
    k_ref[...] = _dot(mixed(2), wk_ref[...]).astype(k_ref.dtype)
    a_ref[...] = _sigmoid(a_pre).astype(a_ref.dtype)
    gt_ref[...] = _dot(g_mid, g2_ref[...]).astype(gt_ref.dtype)
    xv = mixed(3)
    if has_vres:
        mixv = _sigmoid(v0_ref[...] + _dot(_dot(xv, v1_ref[...]).astype(BF16), v2_ref[...]))
    v = _dot(xv, wv_ref[...])
    if has_vres:
        v = v + (vf_ref[...].astype(F32) - v) * mixv
    v_ref[...] = v.astype(v_ref.dtype)


def _pad_cols(w, n):
    return jnp.pad(w, ((0, 0), (0, n - w.shape[1])))


def _pad_rows(w, n):
    return jnp.pad(w, ((0, n - w.shape[0]), (0, 0)))


def _round_up(n, m):
    return (n + m - 1) // m * m


def _lora_pair(w_a, w_b):
    n = _round_up(w_a.shape[1], LANE)
    return _pad_cols(w_a, n).astype(BF16), _pad_rows(w_b, n).astype(BF16)


def _rwkv_pre(x2, g, mod, mu, w_rkv, w0, w1, w2, a0, a1, a2, g1, g2, vres, v_first, *, seq, tm=512):
    t, d = x2.shape
    tm = min(tm, seq)
    tiles_per_seq = seq // tm
    has_vres = vres is not None
    const = lambda i: (0, 0)
    res = dict(pipeline_mode=pl.Buffered(1))
    row = lambda i: (i, 0)
    w1p, w2p = _lora_pair(w1, w2)
    a1p, a2p = _lora_pair(a1, a2)
    g1p, g2p = _lora_pair(g1, g2)
    args = [x2, x2, g.reshape(1, d), mod, _pad_rows(mu, 8),
            w_rkv[0].astype(BF16), w_rkv[1].astype(BF16), w_rkv[2].astype(BF16),
            w0.reshape(1, d), w1p, w2p, a0.reshape(1, d), a1p, a2p, g1p, g2p]
    in_specs = [
        pl.BlockSpec((tm, d), row),
        pl.BlockSpec((8, d), lambda i: (jnp.maximum(i * (tm // 8) - 1, 0), 0)),
        pl.BlockSpec((1, d), const),
        pl.BlockSpec((1, N_SUB * 3, d), lambda i: (i // tiles_per_seq, 0, 0)),
        pl.BlockSpec((8, d), const),
        pl.BlockSpec((d, d), const, **res), pl.BlockSpec((d, d), const, **res),
        pl.BlockSpec((d, d), const, **res),
        pl.BlockSpec((1, d), const),
        pl.BlockSpec(w1p.shape, const, **res), pl.BlockSpec(w2p.shape, const, **res),
        pl.BlockSpec((1, d), const),
        pl.BlockSpec(a1p.shape, const, **res), pl.BlockSpec(a2p.shape, const, **res),
        pl.BlockSpec(g1p.shape, const, **res), pl.BlockSpec(g2p.shape, const, **res),
    ]
    if has_vres:
        v0, v1, v2 = vres
        v1p, v2p = _lora_pair(v1, v2)
        args += [v0.reshape(1, d), v1p, v2p, v_first]
        in_specs += [pl.BlockSpec((1, d), const), pl.BlockSpec(v1p.shape, const, **res),
                     pl.BlockSpec(v2p.shape, const, **res), pl.BlockSpec((tm, d), row)]
    out_dtypes = (BF16, BF16, BF16, F32, BF16, BF16)
    return pl.pallas_call(
        functools.partial(_rwkv_pre_kernel, tiles_per_seq=tiles_per_seq, has_vres=has_vres),
        grid=(t // tm,),
        in_specs=in_specs,
        out_specs=[pl.BlockSpec((tm, d), row)] * 6,
        out_shape=[jax.ShapeDtypeStruct((t, d), dt_) for dt_ in out_dtypes],
        scratch_shapes=[pltpu.VMEM((tm, d), F32), pltpu.VMEM((tm, d), F32)],
        compiler_params=_cparams(("parallel",)),
        name="rwkv7_projections",
    )(*args)


def _rwkv_scan_kernel(*refs):
    st_ref = refs[-1]

    @pl.when(pl.program_id(2) == 0)
    def _():
        st_ref[...] = jnp.zeros_like(st_ref)

    nb = refs[0].shape[0]
    ng = st_ref.shape[0] // nb
    _run_interleaved([_rwkv_chain(bi, gi, *refs) for bi in range(nb) for gi in range(ng)])


def _rwkv_chain(bi, gi, r_ref, k_ref, v_ref, ld_ref, a_ref, gt_ref, kk_ref, ka_ref, rk_ref,
                lnw_ref, lnb_ref, z_ref, st_ref):
    L = RW_CHUNK
    W = RW_GROUP * HEAD
    R = RW_GROUP * L
    lo, hi = gi * W, (gi + 1) * W
    si = bi * (st_ref.shape[0] // r_ref.shape[0]) + gi

    r = r_ref[bi, :, lo:hi].astype(F32)
    k = k_ref[bi, :, lo:hi].astype(F32)
    v = v_ref[bi, :, lo:hi].astype(F32)
    ld = ld_ref[bi, :, lo:hi]
    a = a_ref[bi, :, lo:hi].astype(F32)

    head_ones = jnp.where((_iota((W, W), 0) >> 6) == (_iota((W, W), 1) >> 6), 1.0, 0.0).astype(BF16)

    def head_sum(x, passes=1):
        hi_part = x.astype(BF16)
        out = _dot(hi_part, head_ones)
        if passes == 2:
            out = out + _dot((x - hi_part.astype(F32)).astype(BF16), head_ones)
        return out

    kkn = k * kk_ref[:, lo:hi]
    kk_ss = head_sum(kkn * kkn)
    tril = jnp.where(_iota((L, L), 0) >= _iota((L, L), 1), 1.0, 0.0).astype(BF16)
    cum = _dot_exact_lhs(tril, ld)
    yield
    kk = kkn / jnp.maximum(jnp.sqrt(kk_ss), 1e-12)
    kmod = k * (1.0 + (a - 1.0) * ka_ref[:, lo:hi])
    c_end = cum[L - 1:L, :]
    e_pos = jnp.exp(cum)
    e_neg = jnp.exp(-cum)
    e_prev = jnp.exp(cum - ld)
    e_end = jnp.exp(c_end - cum)
    kka = kk * a

    rows = _iota((R, W), 0)
    cols = _iota((R, W), 1)
    same_head = (rows >> 6) == (cols >> 6)

    def stack(x):
        xb = x.astype(BF16)
        return jnp.where(same_head, jnp.concatenate([xb] * RW_GROUP, axis=0), jnp.zeros((), BF16))

    ar_s = jnp.concatenate([stack(-kk * e_prev), stack(r * e_pos)], axis=0)
    bk_s = jnp.concatenate([stack(kka * e_neg), stack(kmod * e_neg)], axis=0)
    v_s = stack(v)
    bp_s = stack(kka * e_end)
    kp_s = stack(kmod * e_end)

    mr = _iota((R, R), 0)
    mc = _iota((R, R), 1)
    strict = (mr & (L - 1)) > (mc & (L - 1))
    incl = (mr & (L - 1)) >= (mc & (L - 1))
    prod = _dot_nt(ar_s, bk_s)
    st = st_ref[si]
    from_state = _dot_nt(ar_s, st.astype(BF16))
    yield
    a_ab = jnp.where(strict, prod[0:R, 0:R], 0.0)
    a_ak = jnp.where(strict, prod[0:R, R:2 * R], 0.0)
    a_rb = jnp.where(incl, prod[R:2 * R, 0:R], 0.0)
    a_rk = jnp.where(incl, prod[R:2 * R, R:2 * R], 0.0)

    def mm(x, y):
        return _dot(x.astype(BF16), y.astype(BF16))

    eye = jnp.where(mr == mc, 1.0, 0.0)
    a_d = jnp.where((mr >> 3) == (mc >> 3), a_ab, 0.0)
    a_d2 = mm(a_d, a_d)
    rhs = from_state[0:R] + mm(a_ak, v_s)
    bonus = head_sum(r * kmod * rk_ref[:, lo:hi]) * v
    yield
    a_d4 = mm(a_d2, a_d2)
    tinv = mm(eye + a_d, eye + a_d2)
    yield
    tinv = mm(tinv, eye + a_d4)
    yield
    for sh in (3, 4, 5):
        off =((mr >> (sh + 1)) == (mc >> (sh + 1))) & ((mr >> sh) != (mc >> sh))
        half = mm(tinv, jnp.where(off, a_ab, 0.0))
        yield
        tinv = tinv + mm(half, tinv)
        yield

    u_b = mm(tinv, rhs).astype(BF16)
    yield
    uv = jnp.concatenate([u_b, v_s], axis=0)
    y_s = from_state[R:2 * R] + _dot(
        jnp.concatenate([a_rb.astype(BF16), a_rk.astype(BF16)], axis=1), uv)
    st_ref[si] = st * jnp.exp(c_end) + _dot_tn(uv, jnp.concatenate([bp_s, kp_s], axis=0))
    yield
    y = y_s[0:L]
    for i in range(1, RW_GROUP):
        y = y + y_s[i * L:(i + 1) * L]
    mean = head_sum(y, passes=2) * (1.0 / HEAD)
    yield
    yc = y - mean
    var = head_sum(yc * yc) * (1.0 / HEAD)
    yield
    yn = yc * lax.rsqrt(var + RW_GN_EPS) * lnw_ref[:, lo:hi] + lnb_ref[:, lo:hi]
    z_ref[bi, :, lo:hi] = ((yn + bonus) * gt_ref[bi, :, lo:hi].astype(F32)).astype(z_ref.dtype)


def _rwkv_scan(r, k, v, ld, a, gt, k_k, k_a, r_k, ln_w, ln_b, *, lanes=1024, nb=2):
    bsz, seq, d = r.shape
    w = RW_GROUP * HEAD
    lanes = min(lanes, d)
    act = pl.BlockSpec((nb, RW_CHUNK, lanes), lambda b, g, c: (b, c, g))
    par = pl.BlockSpec((1, lanes), lambda b, g, c: (0, g))
    return pl.pallas_call(
        _rwkv_scan_kernel,
        grid=(bsz // nb, d // lanes, seq // RW_CHUNK),
        in_specs=[act] * 6 + [par] * 5,
        out_specs=act,
        out_shape=jax.ShapeDtypeStruct((bsz, seq, d), BF16),
        scratch_shapes=[pltpu.VMEM((nb * (lanes // w), w, w), F32)],
        compiler_params=_cparams(("parallel", "parallel", "arbitrary")),
        name="rwkv7_chunk_scan",
    )(r, k, v, ld, a, gt, k_k.reshape(1, d), k_a.reshape(1, d), r_k.reshape(1, d),
      ln_w.reshape(1, d), ln_b.reshape(1, d))


def _mamba_in_kernel(x_ref, g_ref, mod_ref, wz_ref, wx_ref, wb_ref, wc_ref, wdt_ref,
                     z_ref, xs_ref, b_ref, c_ref, dt_ref, h_ref):
    h_ref[...] = _adaln(x_ref[...], g_ref[...], mod_ref[0, 3:4, :], mod_ref[0, 4:5, :]).astype(BF16)
    z_ref[...] = _dot(h_ref[...], wz_ref[...]).astype(z_ref.dtype)
    xs_ref[...] = _dot(h_ref[...], wx_ref[...]).astype(xs_ref.dtype)
    b_ref[...] = _dot(h_ref[...], wb_ref[...]).astype(b_ref.dtype)
    c_ref[...] = _dot(h_ref[...], wc_ref[...]).astype(c_ref.dtype)
    dt_ref[...] = _dot(h_ref[...], wdt_ref[...])


def _mamba_in(x2, g, mod, w_in, *, seq, d_inner, tm=512):
    t, d = x2.shape
    gn = MB_GROUPS * MB_STATE
    tm = min(tm, seq)
    tiles_per_seq = seq // tm
    wz = w_in[:, :d_inner].astype(BF16)
    wx = w_in[:, d_inner:2 * d_inner].astype(BF16)
    wb = w_in[:, 2 * d_inner:2 * d_inner + gn].astype(BF16)
    wc = w_in[:, 2 * d_inner + gn:2 * d_inner + 2 * gn].astype(BF16)
    wdt = _pad_cols(w_in[:, 2 * d_inner + 2 * gn:], LANE).astype(BF16)
    const = lambda i: (0, 0)
    row = lambda i: (i, 0)
    res = dict(pipeline_mode=pl.Buffered(1))
    widths = (d_inner, d_inner, gn, gn, LANE)
    return pl.pallas_call(
        _mamba_in_kernel,
        grid=(t // tm,),
        in_specs=[pl.BlockSpec((tm, d), row), pl.BlockSpec((1, d), const),
                  pl.BlockSpec((1, N_SUB * 3, d), lambda i: (i // tiles_per_seq, 0, 0))]
        + [pl.BlockSpec((d, n), const, **res) for n in widths],
        out_specs=[pl.BlockSpec((tm, n), row) for n in widths],
        out_shape=[jax.ShapeDtypeStruct((t, n), dt_)
                   for n, dt_ in zip(widths, (BF16, BF16, BF16, BF16, F32))],
        scratch_shapes=[pltpu.VMEM((tm, d), BF16)],
        compiler_params=_cparams(("parallel",)),
        name="mamba2_in_proj",
    )(x2, g.reshape(1, d), mod, wz, wx, wb, wc, wdt)


def _ssd_kernel(xs_ref, b_ref, c_ref, z_ref, dt_ref, cwx_ref, cwb_ref, cwc_ref, cbx_ref, cbb_ref,
                cbc_ref, dtb_ref, alog_ref, dsk_ref, ng_ref, y_ref,
                st_ref, sx_ref, sb_ref, sc_ref):
    L = MB_CHUNK
    stages = ((sx_ref, xs_ref), (sb_ref, b_ref), (sc_ref, c_ref))

    @pl.when(pl.program_id(1) == 0)
    def _():
        st_ref[...] = jnp.zeros_like(st_ref)
        for s_ref, _ in stages:
            s_ref[0:8, :] = jnp.zeros((8, s_ref.shape[1]), F32)

    for s_ref, cur_ref in stages:
        s_ref[8:8 + L, :] = cur_ref[0].astype(F32)

    dt = _softplus(dt_ref[0] + dtb_ref[...])
    da = dt * (-jnp.exp(alog_ref[...]))
    tril = jnp.where(_iota((L, L), 0) >= _iota((L, L), 1), 1.0, 0.0).astype(BF16)
    acum = _dot_exact_lhs(tril, da)
    shared = (_split3(dt), _split3(acum))
    refs = (sx_ref, sb_ref, sc_ref, z_ref, cwx_ref, cwb_ref, cwc_ref, cbx_ref, cbb_ref, cbc_ref,
            dsk_ref, ng_ref, y_ref, st_ref)
    _run_interleaved([_ssd_chain(g, shared, *refs) for g in range(st_ref.shape[0])])

    for s_ref, _ in stages:
        s_ref[0:8, :] = s_ref[L:L + 8, :]


def _ssd_chain(g, shared, sx_ref, sb_ref, sc_ref, z_ref, cwx_ref, cwb_ref, cwc_ref, cbx_ref,
               cbb_ref, cbc_ref, dsk_ref, ng_ref, y_ref, st_ref):
    L = MB_CHUNK
    n = MB_STATE
    wx = st_ref.shape[2]
    hg = wx // HEAD
    dt3, ac3 = shared
    xl, xh = g * wx, (g + 1) * wx
    nl, nh = g * n, (g + 1) * n

    def conv_silu(s_ref, w_ref, bias_ref, lo, hi):
        acc = bias_ref[:, lo:hi]
        for j in range(MB_CONV):
            off = 8 - (MB_CONV - 1) + j
            acc = acc + s_ref[off:off + L, lo:hi] * w_ref[j:j + 1, lo:hi]
        return _silu(acc)

    xs = conv_silu(sx_ref, cwx_ref, cbx_ref, xl, xh)
    bm = conv_silu(sb_ref, cwb_ref, cbb_ref, nl, nh).astype(BF16)
    cm = conv_silu(sc_ref, cwc_ref, cbc_ref, nl, nh).astype(BF16)

    head0 = hg * g
    sel_x = jnp.where(_iota((LANE, wx), 0) == head0 + (_iota((LANE, wx), 1) >> 6), 1.0, 0.0).astype(BF16)
    sel_r = jnp.where(_iota((8, LANE), 1) == head0 + _iota((8, LANE), 0), 1.0, 0.0).astype(BF16)
    sel_c = jnp.where(_iota((LANE, hg * L), 0) == head0 + (_iota((LANE, hg * L), 1) >> 7), 1.0, 0.0).astype(BF16)
    both = [jnp.concatenate([d_p, a_p], axis=0) for d_p, a_p in zip(dt3, ac3)]
    in_x = _dot(both[0], sel_x) + _dot(both[1], sel_x) + _dot(both[2], sel_x)
    dt_x, ac_x = in_x[0:L], in_x[L:2 * L]
    ac_row = _dot_nt(sel_r, ac3[0]) + _dot_nt(sel_r, ac3[1]) + _dot_nt(sel_r, ac3[2])
    ac_col = _dot(ac3[0], sel_c) + _dot(ac3[1], sel_c) + _dot(ac3[2], sel_c)
    cb = _dot_nt(cm, bm)
    st = st_ref[g]
    y_off = _dot(cm, st.astype(BF16))
    yield

    xdt = xs * dt_x
    causal = _iota((L, L), 0) >= _iota((L, L), 1)
    lane_head = _iota((L, wx), 1) >> 6
    xdt_b = xdt.astype(BF16)
    g_parts = []
    x_parts = []
    for j in range(hg):
        seg = jnp.where(causal, ac_col[:, j * L:(j + 1) * L] - ac_row[j:j + 1, :], -jnp.inf)
        g_parts.append((cb * jnp.exp(seg)).astype(BF16))
        x_parts.append(jnp.where(lane_head == j, xdt_b, jnp.zeros((), BF16)))
    y_diag = _dot(jnp.concatenate(g_parts, axis=1), jnp.concatenate(x_parts, axis=0))
    ac_last = ac_x[L - 1:L, :]
    xdec = (xdt * jnp.exp(ac_last - ac_x)).astype(BF16)
    st_ref[g] = st * jnp.exp(ac_last) + _dot_tn(bm, xdec)
    yield

    y = y_diag + y_off * jnp.exp(ac_x) + xs * dsk_ref[:, xl:xh]
    y = y * _silu(z_ref[0, :, xl:xh].astype(F32))
    ms = jnp.mean(y * y, axis=-1, keepdims=True)
    y_ref[0, :, xl:xh] = (y * lax.rsqrt(ms + NORM_EPS) * ng_ref[:, xl:xh]).astype(y_ref.dtype)


def _ssd(xs, bm, cm, z, dt, conv_w, conv_b, dt_bias, a_log, d_skip, norm_g):
    bsz, seq, d_inner = xs.shape
    n = MB_STATE
    gn = MB_GROUPS * n
    wx = d_inner // MB_GROUPS
    heads = d_inner // HEAD
    cw = _pad_rows(conv_w, 8)
    cwx, cwb, cwc = cw[:, :d_inner], cw[:, d_inner:d_inner + gn], cw[:, d_inner + gn:]
    cb = conv_b.reshape(1, -1)
    cbx, cbb, cbc = cb[:, :d_inner], cb[:, d_inner:d_inner + gn], cb[:, d_inner + gn:]
    dtb = _pad_cols(dt_bias.reshape(1, heads), LANE)
    alog = _pad_cols(a_log.reshape(1, heads), LANE)
    dsk = jnp.repeat(d_skip, HEAD).reshape(1, d_inner)
    tok = lambda w: pl.BlockSpec((1, MB_CHUNK, w), lambda b, c: (b, c, 0))
    par = lambda r, w: pl.BlockSpec((r, w), lambda b, c: (0, 0))
    return pl.pallas_call(
        _ssd_kernel,
        grid=(bsz, seq // MB_CHUNK),
        in_specs=[tok(d_inner), tok(gn), tok(gn), tok(d_inner), tok(LANE),
                  par(8, d_inner), par(8, gn), par(8, gn), par(1, d_inner), par(1, gn), par(1, gn),
                  par(1, LANE), par(1, LANE), par(1, d_inner), par(1, d_inner)],
        out_specs=tok(d_inner),
        out_shape=jax.ShapeDtypeStruct((bsz, seq, d_inner), BF16),
        scratch_shapes=[pltpu.VMEM((MB_GROUPS, n, wx), F32),
                        pltpu.VMEM((8 + MB_CHUNK,d_inner), F32),
                        pltpu.VMEM((8 + MB_CHUNK,gn), F32),
                        pltpu.VMEM((8 + MB_CHUNK,gn), F32)],
        compiler_params=_cparams(("parallel", "arbitrary")),
        name="mamba2_conv_ssd",
    )(xs, bm, cm, z, dt, cwx, cwb, cwc, cbx, cbb, cbc, dtb, alog, dsk, norm_g.reshape(1, d_inner))


def _swa_qkv_kernel(x_ref, g_ref, mod_ref, wq_ref, wk_ref, wv_ref, q_ref, k_ref, v_ref, h_ref):
    h_ref[...] = _adaln(x_ref[...], g_ref[...], mod_ref[0, 3:4, :], mod_ref[0, 4:5, :]).astype(BF16)
    q_ref[...] = _dot(h_ref[...], wq_ref[...]).astype(q_ref.dtype)
    k_ref[...] = _dot(h_ref[...], wk_ref[...]).astype(k_ref.dtype)
    v_ref[...] = _dot(h_ref[...], wv_ref[...]).astype(v_ref.dtype)


def _swa_qkv(x2, g, mod, w_qkv, *, seq, tm=512):
    t, d = x2.shape
    nq = d
    nk = SW_KV_HEADS * HEAD
    tm = min(tm, seq)
    tiles_per_seq = seq // tm
    wq = w_qkv[:, :nq].astype(BF16)
    wk = w_qkv[:, nq:nq + nk].astype(BF16)
    wv = w_qkv[:, nq + nk:].astype(BF16)
    const = lambda i: (0, 0)
    row = lambda i: (i, 0)
    res = dict(pipeline_mode=pl.Buffered(1))
    widths = (nq, nk, nk)
    return pl.pallas_call(
        _swa_qkv_kernel,
        grid=(t // tm,),
        in_specs=[pl.BlockSpec((tm, d), row), pl.BlockSpec((1, d), const),
                  pl.BlockSpec((1, N_SUB * 3, d), lambda i: (i // tiles_per_seq, 0, 0))]
        + [pl.BlockSpec((d, n), const, **res) for n in widths],
        out_specs=[pl.BlockSpec((tm, n), row) for n in widths],
        out_shape=[jax.ShapeDtypeStruct((t, n), BF16) for n in widths],
        scratch_shapes=[pltpu.VMEM((tm, d), BF16)],
        compiler_params=_cparams(("parallel",)),
        name="swa_qkv_proj",
    )(x2, g.reshape(1, d), mod, wq, wk, wv)


def _swa_kernel(q_ref, kc_ref, kp_ref, vc_ref, vp_ref, qn_ref, kn_ref, sink_ref, o_ref):
    T = SW_BLOCK
    wk = kc_ref.shape[2]
    first_key = jnp.where(pl.program_id(1) > 0, 0, T)

    def head_norm(x, g):
        w = x.shape[1]
        head_ones = jnp.where((_iota((w, w), 0) >> 6) == (_iota((w, w), 1) >> 6), 1.0, 0.0).astype(BF16)
        ms = _dot((x * x).astype(BF16), head_ones) * (1.0 / HEAD)
        return x * lax.rsqrt(ms + NORM_EPS) * g

    kcat = jnp.concatenate([kp_ref[0], kc_ref[0]], axis=0).astype(F32)
    kcat = head_norm(kcat, kn_ref[...]).astype(BF16)
    vcat = jnp.concatenate([vp_ref[0], vc_ref[0]], axis=0).astype(BF16)

    qi = _iota((T, 2 * T), 0)
    si = _iota((T, 2 * T), 1)
    rel = qi + T - si
    mask = (rel >= 0) & (rel < T) & (si >= first_key)
    shared = (kcat, vcat, mask, si == qi, head_norm)
    _run_interleaved([_swa_chain(kv, shared, q_ref, qn_ref, sink_ref, o_ref) for kv in range(wk // HEAD)])


def _swa_chain(kv, shared, q_ref, qn_ref, sink_ref, o_ref):
    T = SW_BLOCK
    kcat, vcat, mask, sink_slot, head_norm = shared
    wk = kcat.shape[1]
    gq = q_ref.shape[2] // wk
    wq = gq * HEAD
    scale = HEAD ** -0.5
    lo, hi = kv * wq, (kv + 1) * wq

    rep = jnp.where(_iota((wk, wq), 0) == kv * HEAD + (_iota((wk, wq), 1) & (HEAD - 1)), 1.0, 0.0).astype(BF16)
    k_rep = _dot(kcat, rep).astype(BF16)
    v_rep = _dot(vcat, rep).astype(BF16)
    q = head_norm(q_ref[0, :, lo:hi].astype(F32), qn_ref[...]).astype(BF16)
    yield
    lane_head = _iota((T, wq), 1) >> 6
    scores = []
    for j in range(gq):
        q_j = jnp.where(lane_head == j, q, jnp.zeros((), BF16))
        scores.append(_dot_nt(q_j, k_rep) * scale)
        yield
    ones = jnp.ones((2 * T, wq), BF16)
    probs = []
    for j in range(gq):
        sink = sink_ref[0:1, kv * gq + j:kv * gq + j + 1]
        s = jnp.where(sink_slot, sink, jnp.where(mask, scores[j], NEG_INF))
        m = jnp.max(s, axis=-1, keepdims=True)
        yield
        probs.append(jnp.exp(s - m).astype(BF16))
    o = jnp.zeros((T, wq), F32)
    for j in range(gq):
        denom = _dot(probs[j], ones)
        o_j = _dot(jnp.where(sink_slot, jnp.zeros((), BF16), probs[j]), v_rep)
        yield
        o = o + jnp.where(lane_head == j, o_j / denom, 0.0)
    o_ref[0, :, lo:hi] = o.astype(o_ref.dtype)


def _swa(q, k, v, q_norm, k_norm, sinks):
    bsz, seq, dq = q.shape
    dk = k.shape[2]
    nq = dq // HEAD
    cur = lambda b, i: (b, i, 0)
    prev = lambda b, i: (b, jnp.maximum(i - 1, 0), 0)
    const = lambda b, i: (0, 0)
    qn = jnp.tile(q_norm, dq // SW_KV_HEADS // HEAD).reshape(1, -1)
    kn = jnp.tile(k_norm, dk // HEAD).reshape(1, dk)
    return pl.pallas_call(
        _swa_kernel,
        grid=(bsz, seq // SW_BLOCK),
        in_specs=[pl.BlockSpec((1, SW_BLOCK, dq), cur),
                  pl.BlockSpec((1, SW_BLOCK, dk), cur), pl.BlockSpec((1, SW_BLOCK, dk), prev),
                  pl.BlockSpec((1, SW_BLOCK, dk), cur), pl.BlockSpec((1, SW_BLOCK, dk), prev),
                  pl.BlockSpec(qn.shape, const), pl.BlockSpec((1, dk), const),
                  pl.BlockSpec((1, nq), const)],
        out_specs=pl.BlockSpec((1, SW_BLOCK, dq), cur),
        out_shape=jax.ShapeDtypeStruct((bsz, seq, dq), BF16),
        compiler_params=_cparams(("parallel", "parallel")),
        name="swa_sink_attention",
    )(q, k, k, v, v, qn, kn, sinks.reshape(1, nq))


def kernel(x, c, ada_w, ada_b, norm_g, ffn_w_in, ffn_w_out, rw_mu, rw_w_rkv, rw_w_o, rw_w0, rw_w1, rw_w2, rw_a0, rw_a1, rw_a2, rw_g1, rw_g2, rw_k_k, rw_k_a, rw_r_k, rw_ln_w, rw_ln_b, rw_v0, rw_v1, rw_v2, mb_w_in, mb_conv_w, mb_conv_b, mb_dt_bias, mb_A_log, mb_D, mb_norm_g, mb_w_out, sw_w_qkv, sw_q_norm, sw_k_norm, sw_sinks, sw_w_o):
    bsz, seq, d = x.shape
    depth = ada_w.shape[0]
    t = bsz * seq
    mods = _modulation(c, ada_w, ada_b)
    x2 = x.reshape(t, d)
    v_first = None
    for i in range(depth):
        mod = mods[i]
        x2 = _ffn(x2, norm_g[i, 0], mod, ffn_w_in[i, 0].astype(BF16), ffn_w_out[i, 0].astype(BF16),
                  sub=0, seq=seq)
        kind, j = i % 3, i // 3
        if kind == 0:
            vres = None if v_first is None else (rw_v0[j - 1], rw_v1[j - 1], rw_v2[j - 1])
            r, k, v, ld, a, gt = _rwkv_pre(
                x2, norm_g[i, 1], mod, rw_mu[j], rw_w_rkv[j], rw_w0[j], rw_w1[j], rw_w2[j],
                rw_a0[j], rw_a1[j], rw_a2[j], rw_g1[j], rw_g2[j], vres, v_first, seq=seq)
            if v_first is None:
                v_first = v
            sh = (bsz, seq, d)
            z = _rwkv_scan(r.reshape(sh), k.reshape(sh), v.reshape(sh), ld.reshape(sh), a.reshape(sh),
                           gt.reshape(sh), rw_k_k[j], rw_k_a[j], rw_r_k[j], rw_ln_w[j], rw_ln_b[j])
            mixer = (z.reshape(t, d), rw_w_o[j].astype(BF16))
        elif kind == 1:
            d_inner = mb_w_out.shape[1]
            z, xs, bm, cm, dt = _mamba_in(x2, norm_g[i, 1], mod, mb_w_in[j], seq=seq, d_inner=d_inner)
            y = _ssd(xs.reshape(bsz, seq, -1), bm.reshape(bsz, seq, -1), cm.reshape(bsz, seq, -1),
                     z.reshape(bsz, seq, -1), dt.reshape(bsz, seq, -1), mb_conv_w[j], mb_conv_b[j],
                     mb_dt_bias[j], mb_A_log[j], mb_D[j], mb_norm_g[j])
            mixer = (y.reshape(t, d_inner), mb_w_out[j].astype(BF16))
        else:
            q, k, v = _swa_qkv(x2, norm_g[i, 1], mod, sw_w_qkv[j], seq=seq)
            o = _swa(q.reshape(bsz, seq, -1), k.reshape(bsz, seq, -1), v.reshape(bsz, seq, -1),
                     sw_q_norm[j], sw_k_norm[j], sw_sinks[j])
            mixer = (o.reshape(t, -1), sw_w_o[j].astype(BF16))
        x2 = _ffn(x2, norm_g[i, 2], mod, ffn_w_in[i, 1].astype(BF16), ffn_w_out[i, 1].astype(BF16),
                  sub=2, seq=seq, mixer=mixer)
    return x2.reshape(bsz, seq, d)
```

```python
import functools

import jax
import jax.numpy as jnp
from jax import lax
from jax.experimental import pallas as pl
from jax.experimental.pallas import tpu as pltpu

F32 = jnp.float32
BF16 = jnp.bfloat16

NORM_EPS = 1e-6
MACARON_W = 0.5
N_SUB = 3
HEAD = 64
RW_GN_EPS = 64e-5
RW_CHUNK = 64
RW_GROUP = 2
MB_GROUPS = 8
MB_STATE = 128
MB_CONV = 4
MB_CHUNK = 128
SW_KV_HEADS = 4
SW_BLOCK = 128
NEG_INF = -1e30
LANE = 128
VMEM_LIMIT = 56 * 1024 * 1024


def _cparams(sem):
    return pltpu.CompilerParams(dimension_semantics=sem, vmem_limit_bytes=VMEM_LIMIT)


def _dot(a, b):
    return jnp.dot(a, b, preferred_element_type=F32)


def _dot_nt(a, b):
    return lax.dot_general(a, b, (((1,), (1,)), ((), ())), preferred_element_type=F32)


def _dot_tn(a, b):
    return lax.dot_general(a, b, (((0,), (0,)), ((), ())), preferred_element_type=F32)


def _split3(x):
    hi = x.astype(BF16)
    r1 = x - hi.astype(F32)
    mid = r1.astype(BF16)
    lo = (r1 - mid.astype(F32)).astype(BF16)
    return hi, mid, lo


def _dot_exact_lhs(m, x):
    hi, mid, lo = _split3(x)
    return _dot(m, hi) + _dot(m, mid) + _dot(m, lo)


def _sigmoid(x):
    return 0.5 + 0.5 * jnp.tanh(0.5 * x)


def _silu(x):
    h = 0.5 * x
    return h + h * jnp.tanh(h)


def _softplus(x):
    return jnp.maximum(x, 0.0) + jnp.log(1.0 + jnp.exp(-jnp.abs(x)))


def _adaln(x, g, shift, scale):
    ms = jnp.mean(x * x, axis=-1, keepdims=True)
    return (x * lax.rsqrt(ms + NORM_EPS)) * g * (1.0 + scale) + shift


def _iota(shape, axis):
    return lax.broadcasted_iota(jnp.int32, shape, axis)


def _shift_rows(cur, tail, s):
    rolled = pltpu.roll(cur, s, 0)
    head = jnp.where(_iota(tail.shape, 0) < s, pltpu.roll(tail, s, 0), rolled[0:8])
    return jnp.concatenate([head, rolled[8:]], axis=0)


def _run_interleaved(chains):
    while chains:
        chains = [ch for ch in chains if next(ch, "done") != "done"]


def _mod_kernel(c_ref, w_ref, b_ref, o_ref):
    ca = _silu(c_ref[...])
    a_hi, a_mid, a_lo = _split3(ca)
    w_hi, w_mid, w_lo = _split3(w_ref[0])
    acc = _dot(a_hi, w_hi) + (_dot(a_hi, w_mid) + _dot(a_mid, w_hi))
    acc = acc + (_dot(a_hi, w_lo) + _dot(a_mid, w_mid) + _dot(a_lo, w_hi))
    o_ref[0] = acc + b_ref[0]


def _modulation(c, ada_w, ada_b):
    depth, d, n = ada_w.shape
    bsz = c.shape[0]
    tn = 1152
    out = pl.pallas_call(
        _mod_kernel,
        grid=(depth, n // tn),
        in_specs=[
            pl.BlockSpec((bsz, d), lambda l, j: (0, 0)),
            pl.BlockSpec((1, d, tn), lambda l, j: (l, 0, j)),
            pl.BlockSpec((1, 1, tn), lambda l, j: (l, 0, j)),
        ],
        out_specs=pl.BlockSpec((1, bsz, tn), lambda l, j: (l, 0, j)),
        out_shape=jax.ShapeDtypeStruct((depth, bsz, n), F32),
        compiler_params=_cparams(("parallel", "parallel")),
        name="adaln_modulation",
    )(c, ada_w, ada_b.reshape(depth, 1, n))
    return out.reshape(depth, bsz, N_SUB * 3, d)


def _ffn_kernel(*refs, sub, fc, fused_mixer):
    if fused_mixer:
        x_ref, g_ref, mod_ref, win_ref, wout_ref, z_ref, wo_ref, o_ref, h_ref, acc_ref = refs
        o_ref[...] = x_ref[...] + mod_ref[0, 5:6, :] * _dot(z_ref[...], wo_ref[...])
        res_ref = o_ref
    else:
        x_ref, g_ref, mod_ref, win_ref, wout_ref, o_ref, h_ref, acc_ref = refs
        res_ref = x_ref
    dff = wout_ref.shape[0]
    shift = mod_ref[0, 3 * sub:3 * sub + 1, :]
    scale = mod_ref[0, 3 * sub + 1:3 * sub + 2, :]
    gate = mod_ref[0, 3 * sub + 2:3 * sub + 3, :]
    h_ref[...] = _adaln(res_ref[...], g_ref[...], shift, scale).astype(BF16)
    for c in range(dff // fc):
        h = h_ref[...]
        a = _dot(h, win_ref[:, c * fc:(c + 1) * fc])
        b = _dot(h, win_ref[:, dff + c * fc:dff + (c + 1) * fc])
        act = (_silu(a) * b).astype(BF16)
        part = _dot(act, wout_ref[c * fc:(c + 1) * fc, :])
        if c == 0:
            acc_ref[...] = part
        else:
            acc_ref[...] += part
    o_ref[...] = res_ref[...] + (MACARON_W * gate) * acc_ref[...]


def _ffn(x2, g, mod, w_in, w_out, *, sub, seq, mixer=None, tm=None, fc=256):
    t, d = x2.shape
    dff = w_out.shape[0]
    if tm is None:
        tm = 1024 if mixer is None else 512
    tm = min(tm, seq)
    tiles_per_seq = seq // tm
    resident = dict(pipeline_mode=pl.Buffered(1))
    args = [x2, g.reshape(1, d), mod, w_in, w_out]
    in_specs = [
        pl.BlockSpec((tm, d), lambda i: (i, 0)),
        pl.BlockSpec((1, d), lambda i: (0, 0)),
        pl.BlockSpec((1, N_SUB * 3, d), lambda i: (i // tiles_per_seq, 0, 0)),
        pl.BlockSpec((d, 2 * dff), lambda i: (0, 0), **resident),
        pl.BlockSpec((dff, d), lambda i: (0, 0), **resident),
    ]
    if mixer is not None:
        z2, w_o = mixer
        args += [z2, w_o]
        in_specs += [pl.BlockSpec((tm, z2.shape[1]), lambda i: (i, 0)),
                     pl.BlockSpec(w_o.shape, lambda i: (0, 0), **resident)]
    return pl.pallas_call(
        functools.partial(_ffn_kernel, sub=sub, fc=fc, fused_mixer=mixer is not None),
        grid=(t // tm,),
        in_specs=in_specs,
        out_specs=pl.BlockSpec((tm, d), lambda i: (i, 0)),
        out_shape=jax.ShapeDtypeStruct((t, d), F32),
        scratch_shapes=[pltpu.VMEM((tm, d), BF16), pltpu.VMEM((tm, d), F32)],
        compiler_params=_cparams(("parallel",)),
        name="macaron_ffn_mixer_out" if mixer is not None else "macaron_ffn",
    )(*args)


def _rwkv_pre_kernel(*refs, tiles_per_seq, has_vres):
    if has_vres:
        (x_ref, xp_ref, g_ref, mod_ref, mu_ref, wr_ref, wk_ref, wv_ref, w0_ref, w1_ref, w2_ref,
         a0_ref, a1_ref, a2_ref, g1_ref, g2_ref, v0_ref, v1_ref, v2_ref, vf_ref,
         r_ref, k_ref, v_ref, ld_ref, a_ref, gt_ref, h_ref, dx_ref) = refs
    else:
        (x_ref, xp_ref, g_ref, mod_ref, mu_ref, wr_ref, wk_ref, wv_ref, w0_ref, w1_ref, w2_ref,
         a0_ref, a1_ref, a2_ref, g1_ref, g2_ref,
         r_ref, k_ref, v_ref, ld_ref, a_ref, gt_ref, h_ref, dx_ref) = refs
    shift = mod_ref[0, 3:4, :]
    scale = mod_ref[0, 4:5, :]
    g = g_ref[...]
    h = _adaln(x_ref[...], g, shift, scale)
    first = (pl.program_id(0) % tiles_per_seq) == 0
    hp = _adaln(xp_ref[...], g, shift, scale)
    hp = jnp.where(first, 0.0, hp)
    h_ref[...] = h
    dx_ref[...] = _shift_rows(h, hp, 1) - h

    def chain(lo, hi, delay):
        for _ in range(delay):
            yield

        def mixed(j):
            return (h_ref[lo:hi, :] + dx_ref[lo:hi, :] * mu_ref[j:j + 1, :]).astype(BF16)

        w_mid = jnp.tanh(_dot(mixed(1), w1_ref[...])).astype(BF16)
        a_mid = _dot(mixed(4), a1_ref[...]).astype(BF16)
        yield
        wl = w0_ref[...] + _dot(w_mid, w2_ref[...])
        a_pre = a0_ref[...] + _dot(a_mid, a2_ref[...])
        g_mid = _sigmoid(_dot(mixed(5), g1_ref[...])).astype(BF16)
        yield
        r_ref[lo:hi, :] = _dot(mixed(0), wr_ref[...]).astype(r_ref.dtype)
        yield
        ld_ref[lo:hi, :] = -jnp.exp(-_softplus(-wl) - 0.5)
        k_ref[lo:hi, :] = _dot(mixed(2), wk_ref[...]).astype(k_ref.dtype)
        yield
        a_ref[lo:hi, :] = _sigmoid(a_pre).astype(a_ref.dtype)
        gt_ref[lo:hi, :] = _dot(g_mid, g2_ref[...]).astype(gt_ref.dtype)
        xv = mixed(3)
        if has_vres:
            mixv = _sigmoid(v0_ref[...] + _dot(_dot(xv, v1_ref[...]).astype(BF16), v2_ref[...]))
        yield
        v = _dot(xv, wv_ref[...])
        if has_vres:
            v = v + (vf_ref[lo:hi, :].astype(F32) - v) * mixv
        v_ref[lo:hi, :] = v.astype(v_ref.dtype)

    tm = h_ref.shape[0]
    nsplit = 2 if tm % 32 == 0 else 1
    rows = tm // nsplit
    _run_interleaved([chain(i * rows, (i + 1) * rows, 2 * i) for i in range(nsplit)])


def _pad_cols(w, n):
    return jnp.pad(w, ((0, 0), (0, n - w.shape[1])))


def _pad_rows(w, n):
    return jnp.pad(w, ((0, n - w.shape[0]), (0, 0)))


def _round_up(n, m):
    return (n + m - 1) // m * m


def _lora_pair(w_a, w_b):
    n = _round_up(w_a.shape[1], LANE)
    return _pad_cols(w_a, n).astype(BF16), _pad_rows(w_b, n).astype(BF16)


def _rwkv_pre(x2, g, mod, mu, w_rkv, w0, w1, w2, a0, a1, a2, g1, g2, vres, v_first, *, seq, tm=512):
    t, d = x2.shape
    tm = min(tm, seq)
    tiles_per_seq = seq // tm
    has_vres = vres is not None
    const = lambda i: (0, 0)
    res = dict(pipeline_mode=pl.Buffered(1))
    row = lambda i: (i, 0)
    w1p, w2p = _lora_pair(w1, w2)
    a1p, a2p = _lora_pair(a1, a2)
    g1p, g2p = _lora_pair(g1, g2)
    args = [x2, x2, g.reshape(1, d), mod, _pad_rows(mu, 8),
            w_rkv[0].astype(BF16), w_rkv[1].astype(BF16), w_rkv[2].astype(BF16),
            w0.reshape(1, d), w1p, w2p, a0.reshape(1, d), a1p, a2p, g1p, g2p]
    in_specs = [
        pl.BlockSpec((tm, d), row),
        pl.BlockSpec((8, d), lambda i: (jnp.maximum(i * (tm // 8) - 1, 0), 0)),
        pl.BlockSpec((1, d), const),
        pl.BlockSpec((1, N_SUB * 3, d), lambda i: (i // tiles_per_seq, 0, 0)),
        pl.BlockSpec((8, d), const),
        pl.BlockSpec((d, d), const, **res), pl.BlockSpec((d, d), const, **res),
        pl.BlockSpec((d, d), const, **res),
        pl.BlockSpec((1, d), const),
        pl.BlockSpec(w1p.shape, const, **res), pl.BlockSpec(w2p.shape, const, **res),
        pl.BlockSpec((1, d), const),
        pl.BlockSpec(a1p.shape, const, **res), pl.BlockSpec(a2p.shape, const, **res),
        pl.BlockSpec(g1p.shape, const, **res), pl.BlockSpec(g2p.shape, const, **res),
    ]
    if has_vres:
        v0, v1, v2 = vres
        v1p, v2p = _lora_pair(v1, v2)
        args += [v0.reshape(1, d), v1p, v2p, v_first]
        in_specs += [pl.BlockSpec((1, d), const), pl.BlockSpec(v1p.shape, const, **res),
                     pl.BlockSpec(v2p.shape, const, **res), pl.BlockSpec((tm, d), row)]
    out_dtypes = (BF16, BF16, BF16, F32, BF16, BF16)
    return pl.pallas_call(
        functools.partial(_rwkv_pre_kernel, tiles_per_seq=tiles_per_seq, has_vres=has_vres),
        grid=(t // tm,),
        in_specs=in_specs,
        out_specs=[pl.BlockSpec((tm, d), row)] * 6,
        out_shape=[jax.ShapeDtypeStruct((t, d), dt_) for dt_ in out_dtypes],
        scratch_shapes=[pltpu.VMEM((tm, d), F32), pltpu.VMEM((tm, d), F32)],
        compiler_params=_cparams(("parallel",)),
        name="rwkv7_projections",
    )(*args)


def _rwkv_scan_kernel(*refs):
    st_ref = refs[-1]

    @pl.when(pl.program_id(2) == 0)
    def _():
        st_ref[...] = jnp.zeros_like(st_ref)

    nb = refs[0].shape[0]
    ng = st_ref.shape[0] // nb
    _run_interleaved([_rwkv_chain(bi, gi, *refs) for bi in range(nb) for gi in range(ng)])


def _rwkv_chain(bi, gi, r_ref, k_ref, v_ref, ld_ref, a_ref, gt_ref, kk_ref, ka_ref, rk_ref,
                lnw_ref, lnb_ref, z_ref, st_ref):
    L = RW_CHUNK
    W = RW_GROUP * HEAD
    R = RW_GROUP * L
    lo, hi = gi * W, (gi + 1) * W
    si = bi * (st_ref.shape[0] // r_ref.shape[0]) + gi

    r = r_ref[bi, :, lo:hi].astype(F32)
    k = k_ref[bi, :, lo:hi].astype(F32)
    v = v_ref[bi, :, lo:hi].astype(F32)
    ld = ld_ref[bi, :, lo:hi]
    a = a_ref[bi, :, lo:hi].astype(F32)

    head_ones = jnp.where((_iota((W, W), 0) >> 6) == (_iota((W, W), 1) >> 6), 1.0, 0.0).astype(BF16)

    def head_sum(x, passes=1):
        hi_part = x.astype(BF16)
        out = _dot(hi_part, head_ones)
        if passes == 2:
            out = out + _dot((x - hi_part.astype(F32)).astype(BF16), head_ones)
        return out

    kkn = k * kk_ref[:, lo:hi]
    kk_ss = head_sum(kkn * kkn)
    tril = jnp.where(_iota((L, L), 0) >= _iota((L, L), 1), 1.0, 0.0).astype(BF16)
    cum = _dot_exact_lhs(tril, ld)
    yield
    kk = kkn / jnp.maximum(jnp.sqrt(kk_ss), 1e-12)
    kmod = k * (1.0 + (a - 1.0) * ka_ref[:, lo:hi])
    c_end = cum[L - 1:L, :]
    e_pos = jnp.exp(cum)
    e_neg = jnp.exp(-cum)
    e_prev = jnp.exp(cum - ld)
    e_end = jnp.exp(c_end - cum)
    kka = kk * a

    rows = _iota((R, W), 0)
    cols = _iota((R, W), 1)
    same_head = (rows >> (L.bit_length() - 1)) == (cols >> 6)

    def stack(x):
        xb = x.astype(BF16)
        return jnp.where(same_head, jnp.concatenate([xb] * RW_GROUP, axis=0), jnp.zeros((), BF16))

    ar_s = jnp.concatenate([stack(-kk * e_prev), stack(r * e_pos)], axis=0)
    bk_s = jnp.concatenate([stack(kka * e_neg), stack(kmod * e_neg)], axis=0)
    v_s = stack(v)
    bp_s = stack(kka * e_end)
    kp_s = stack(kmod * e_end)

    mr = _iota((R, R), 0)
    mc = _iota((R, R), 1)
    strict = (mr & (L - 1)) > (mc & (L - 1))
    incl = (mr & (L - 1)) >= (mc & (L - 1))
    prod = _dot_nt(ar_s, bk_s)
    st = st_ref[si]
    from_state = _dot_nt(ar_s, st.astype(BF16))
    yield
    a_ab = jnp.where(strict, prod[0:R, 0:R], 0.0)
    a_ak = jnp.where(strict, prod[0:R, R:2 * R], 0.0)
    a_rb = jnp.where(incl, prod[R:2 * R, 0:R], 0.0)
    a_rk = jnp.where(incl, prod[R:2 * R, R:2 * R], 0.0)

    def mm(x, y):
        return _dot(x.astype(BF16), y.astype(BF16))

    eye = jnp.where(mr == mc, 1.0, 0.0)
    a_d = jnp.where((mr >> 3) == (mc >> 3), a_ab, 0.0)
    a_d2 = mm(a_d, a_d)
    rhs = from_state[0:R] + mm(a_ak, v_s)
    bonus = head_sum(r * kmod * rk_ref[:, lo:hi]) * v
    yield
    a_d4 = mm(a_d2, a_d2)
    tinv = mm(eye + a_d, eye + a_d2)
    yield
    tinv = mm(tinv, eye + a_d4)
    yield
    for sh in range(3, L.bit_length() - 1):
        off =((mr >> (sh + 1)) == (mc >> (sh + 1))) & ((mr >> sh) != (mc >> sh))
        half = mm(tinv, jnp.where(off, a_ab, 0.0))
        yield
        tinv = tinv + mm(half, tinv)
        yield

    u_b = mm(tinv, rhs).astype(BF16)
    yield
    uv = jnp.concatenate([u_b, v_s], axis=0)
    y_s = from_state[R:2 * R] + _dot(
        jnp.concatenate([a_rb.astype(BF16), a_rk.astype(BF16)], axis=1), uv)
    st_ref[si] = st * jnp.exp(c_end) + _dot_tn(uv, jnp.concatenate([bp_s, kp_s], axis=0))
    yield
    y = y_s[0:L]
    for i in range(1, RW_GROUP):
        y = y + y_s[i * L:(i + 1) * L]
    mean = head_sum(y, passes=2) * (1.0 / HEAD)
    yield
    yc = y - mean
    var = head_sum(yc * yc) * (1.0 / HEAD)
    yield
    yn = yc * lax.rsqrt(var + RW_GN_EPS) * lnw_ref[:, lo:hi] + lnb_ref[:, lo:hi]
    z_ref[bi, :, lo:hi] = ((yn + bonus) * gt_ref[bi, :, lo:hi].astype(F32)).astype(z_ref.dtype)


def _rwkv_scan(r, k, v, ld, a, gt, k_k, k_a, r_k, ln_w, ln_b, *, lanes=1024, nb=2):
    bsz, seq, d = r.shape
    w = RW_GROUP * HEAD
    lanes = min(lanes, d)
    nb = min(nb, bsz)
    act = pl.BlockSpec((nb, RW_CHUNK, lanes), lambda b, g, c: (b, c, g))
    par = pl.BlockSpec((1, lanes), lambda b, g, c: (0, g))
    return pl.pallas_call(
        _rwkv_scan_kernel,
        grid=(bsz // nb, d // lanes, seq // RW_CHUNK),
        in_specs=[act] * 6 + [par] * 5,
        out_specs=act,
        out_shape=jax.ShapeDtypeStruct((bsz, seq, d), BF16),
        scratch_shapes=[pltpu.VMEM((nb * (lanes // w), w, w), F32)],
        compiler_params=_cparams(("parallel", "parallel", "arbitrary")),
        name="rwkv7_chunk_scan",
    )(r, k, v, ld, a, gt, k_k.reshape(1, d), k_a.reshape(1, d), r_k.reshape(1, d),
      ln_w.reshape(1, d), ln_b.reshape(1, d))


def _mamba_in_kernel(x_ref, g_ref, mod_ref, wz_ref, wx_ref, wb_ref, wc_ref, wdt_ref,
                     z_ref, xs_ref, b_ref, c_ref, dt_ref, h_ref):
    h_ref[...] = _adaln(x_ref[...], g_ref[...], mod_ref[0, 3:4, :], mod_ref[0, 4:5, :]).astype(BF16)
    z_ref[...] = _dot(h_ref[...], wz_ref[...]).astype(z_ref.dtype)
    xs_ref[...] = _dot(h_ref[...], wx_ref[...]).astype(xs_ref.dtype)
    b_ref[...] = _dot(h_ref[...], wb_ref[...]).astype(b_ref.dtype)
    c_ref[...] = _dot(h_ref[...], wc_ref[...]).astype(c_ref.dtype)
    dt_ref[...] = _dot(h_ref[...], wdt_ref[...])


def _mamba_in(x2, g, mod, w_in, *, seq, d_inner, tm=512):
    t, d = x2.shape
    gn = MB_GROUPS * MB_STATE
    tm = min(tm, seq)
    tiles_per_seq = seq // tm
    wz = w_in[:, :d_inner].astype(BF16)
    wx = w_in[:, d_inner:2 * d_inner].astype(BF16)
    wb = w_in[:, 2 * d_inner:2 * d_inner + gn].astype(BF16)
    wc = w_in[:, 2 * d_inner + gn:2 * d_inner + 2 * gn].astype(BF16)
    wdt = _pad_cols(w_in[:, 2 * d_inner + 2 * gn:], LANE).astype(BF16)
    const = lambda i: (0, 0)
    row = lambda i: (i, 0)
    res = dict(pipeline_mode=pl.Buffered(1))
    widths = (d_inner, d_inner, gn, gn, LANE)
    return pl.pallas_call(
        _mamba_in_kernel,
        grid=(t // tm,),
        in_specs=[pl.BlockSpec((tm, d), row), pl.BlockSpec((1, d), const),
                  pl.BlockSpec((1, N_SUB * 3, d), lambda i: (i // tiles_per_seq, 0, 0))]
        + [pl.BlockSpec((d, n), const, **res) for n in widths],
        out_specs=[pl.BlockSpec((tm, n), row) for n in widths],
        out_shape=[jax.ShapeDtypeStruct((t, n), dt_)
                   for n, dt_ in zip(widths, (BF16, BF16, BF16, BF16, F32))],
        scratch_shapes=[pltpu.VMEM((tm, d), BF16)],
        compiler_params=_cparams(("parallel",)),
        name="mamba2_in_proj",
    )(x2, g.reshape(1, d), mod, wz, wx, wb, wc, wdt)


def _ssd_kernel(xs_ref, b_ref, c_ref, z_ref, dt_ref, cwx_ref, cwb_ref, cwc_ref, cbx_ref, cbb_ref,
                cbc_ref, dtb_ref, alog_ref, dsk_ref, ng_ref, y_ref,
                st_ref, sx_ref, sb_ref, sc_ref):
    L = MB_CHUNK
    stages = ((sx_ref, xs_ref), (sb_ref, b_ref), (sc_ref, c_ref))

    @pl.when(pl.program_id(1) == 0)
    def _():
        st_ref[...] = jnp.zeros_like(st_ref)
        for s_ref, _ in stages:
            s_ref[0:8, :] = jnp.zeros((8, s_ref.shape[1]), F32)

    for s_ref, cur_ref in stages:
        s_ref[8:8 + L, :] = cur_ref[0].astype(F32)

    dt = _softplus(dt_ref[0] + dtb_ref[...])
    da = dt * (-jnp.exp(alog_ref[...]))
    tril = jnp.where(_iota((L, L), 0) >= _iota((L, L), 1), 1.0, 0.0).astype(BF16)
    acum = _dot_exact_lhs(tril, da)
    shared = (_split3(dt), _split3(acum))
    refs = (sx_ref, sb_ref, sc_ref, z_ref, cwx_ref, cwb_ref, cwc_ref, cbx_ref, cbb_ref, cbc_ref,
            dsk_ref, ng_ref, y_ref, st_ref)
    _run_interleaved([_ssd_chain(g, shared, *refs) for g in range(st_ref.shape[0])])

    for s_ref, _ in stages:
        s_ref[0:8, :] = s_ref[L:L + 8, :]


def _ssd_chain(g, shared, sx_ref, sb_ref, sc_ref, z_ref, cwx_ref, cwb_ref, cwc_ref, cbx_ref,
               cbb_ref, cbc_ref, dsk_ref, ng_ref, y_ref, st_ref):
    L = MB_CHUNK
    n = MB_STATE
    wx = st_ref.shape[2]
    hg = wx // HEAD
    dt3, ac3 = shared
    xl, xh = g * wx, (g + 1) * wx
    nl, nh = g * n, (g + 1) * n

    def conv_silu(s_ref, w_ref, bias_ref, lo, hi):
        staged = s_ref[:, lo:hi]
        prev = pltpu.roll(staged, 1, 0)
        w = [w_ref[j:j + 1, lo:hi] for j in range(MB_CONV)]
        older = pltpu.roll(staged * w[1] + prev * w[0], 2, 0)
        acc = bias_ref[:, lo:hi] + staged * w[3] + prev * w[2] + older
        return _silu(acc[8:8 + L])

    xs = conv_silu(sx_ref, cwx_ref, cbx_ref, xl, xh)
    bm = conv_silu(sb_ref, cwb_ref, cbb_ref, nl, nh).astype(BF16)
    cm = conv_silu(sc_ref, cwc_ref, cbc_ref, nl, nh).astype(BF16)

    head0 = hg * g
    sel_x = jnp.where(_iota((LANE, wx), 0) == head0 + (_iota((LANE, wx), 1) >> 6), 1.0, 0.0).astype(BF16)
    sel_r = jnp.where(_iota((8, LANE), 1) == head0 + _iota((8, LANE), 0), 1.0, 0.0).astype(BF16)
    sel_c = jnp.where(_iota((LANE, hg * L), 0) == head0 + (_iota((LANE, hg * L), 1) >> 7), 1.0, 0.0).astype(BF16)
    both = [jnp.concatenate([d_p, a_p], axis=0) for d_p, a_p in zip(dt3, ac3)]
    in_x = _dot(both[0], sel_x) + _dot(both[1], sel_x) + _dot(both[2], sel_x)
    dt_x, ac_x = in_x[0:L], in_x[L:2 * L]
    ac_row = _dot_nt(sel_r, ac3[0]) + _dot_nt(sel_r, ac3[1]) + _dot_nt(sel_r, ac3[2])
    ac_col = _dot(ac3[0], sel_c) + _dot(ac3[1], sel_c) + _dot(ac3[2], sel_c)
    cb = _dot_nt(cm, bm)
    st = st_ref[g]
    y_off = _dot(cm, st.astype(BF16))
    yield

    xdt = xs * dt_x
    causal = _iota((L, L), 0) >= _iota((L, L), 1)
    lane_head = _iota((L, wx), 1) >> 6
    xdt_b = xdt.astype(BF16)
    g_parts = []
    x_parts = []
    for j in range(hg):
        seg = jnp.where(causal, ac_col[:, j * L:(j + 1) * L] - ac_row[j:j + 1, :], -jnp.inf)
        g_parts.append((cb * jnp.exp(seg)).astype(BF16))
        x_parts.append(jnp.where(lane_head == j, xdt_b, jnp.zeros((), BF16)))
    y_diag = _dot(jnp.concatenate(g_parts, axis=1), jnp.concatenate(x_parts, axis=0))
    ac_last = ac_x[L - 1:L, :]
    xdec = (xdt * jnp.exp(ac_last - ac_x)).astype(BF16)
    st_ref[g] = st * jnp.exp(ac_last) + _dot_tn(bm, xdec)
    yield

    y = y_diag + y_off * jnp.exp(ac_x) + xs * dsk_ref[:, xl:xh]
    y = y * _silu(z_ref[0, :, xl:xh].astype(F32))
    ms = jnp.mean(y * y, axis=-1, keepdims=True)
    y_ref[0, :, xl:xh] = (y * lax.rsqrt(ms + NORM_EPS) * ng_ref[:, xl:xh]).astype(y_ref.dtype)


def _ssd(xs, bm, cm, z, dt, conv_w, conv_b, dt_bias, a_log, d_skip, norm_g):
    bsz, seq, d_inner = xs.shape
    n = MB_STATE
    gn = MB_GROUPS * n
    wx = d_inner // MB_GROUPS
    heads = d_inner // HEAD
    assert conv_w.shape[0] == MB_CONV == 4, "the in-kernel conv is written for 4 taps"
    cw = _pad_rows(conv_w, 8)
    cwx, cwb, cwc = cw[:, :d_inner], cw[:, d_inner:d_inner + gn], cw[:, d_inner + gn:]
    cb = conv_b.reshape(1, -1)
    cbx, cbb, cbc = cb[:, :d_inner], cb[:, d_inner:d_inner + gn], cb[:, d_inner + gn:]
    dtb = _pad_cols(dt_bias.reshape(1, heads), LANE)
    alog = _pad_cols(a_log.reshape(1, heads), LANE)
    dsk = jnp.repeat(d_skip, HEAD).reshape(1, d_inner)
    tok = lambda w: pl.BlockSpec((1, MB_CHUNK, w), lambda b, c: (b, c, 0))
    par = lambda r, w: pl.BlockSpec((r, w), lambda b, c: (0, 0))
    return pl.pallas_call(
        _ssd_kernel,
        grid=(bsz, seq // MB_CHUNK),
        in_specs=[tok(d_inner), tok(gn), tok(gn), tok(d_inner), tok(LANE),
                  par(8, d_inner), par(8, gn), par(8, gn), par(1, d_inner), par(1, gn), par(1, gn),
                  par(1, LANE), par(1, LANE), par(1, d_inner), par(1, d_inner)],
        out_specs=tok(d_inner),
        out_shape=jax.ShapeDtypeStruct((bsz, seq, d_inner), BF16),
        scratch_shapes=[pltpu.VMEM((MB_GROUPS, n, wx), F32),
                        pltpu.VMEM((8 + MB_CHUNK,d_inner), F32),
                        pltpu.VMEM((8 + MB_CHUNK,gn), F32),
                        pltpu.VMEM((8 + MB_CHUNK,gn), F32)],
        compiler_params=_cparams(("parallel", "arbitrary")),
        name="mamba2_conv_ssd",
    )(xs, bm, cm, z, dt, cwx, cwb, cwc, cbx, cbb, cbc, dtb, alog, dsk, norm_g.reshape(1, d_inner))


def _swa_qkv_kernel(x_ref, g_ref, mod_ref, wq_ref, wk_ref, wv_ref, q_ref, k_ref, v_ref, h_ref):
    h_ref[...] = _adaln(x_ref[...], g_ref[...], mod_ref[0, 3:4, :], mod_ref[0, 4:5, :]).astype(BF16)
    q_ref[...] = _dot(h_ref[...], wq_ref[...]).astype(q_ref.dtype)
    k_ref[...] = _dot(h_ref[...], wk_ref[...]).astype(k_ref.dtype)
    v_ref[...] = _dot(h_ref[...], wv_ref[...]).astype(v_ref.dtype)


def _swa_qkv(x2, g, mod, w_qkv, *, seq, tm=512):
    t, d = x2.shape
    nq = d
    nk = SW_KV_HEADS * HEAD
    tm = min(tm, seq)
    tiles_per_seq = seq // tm
    wq = w_qkv[:, :nq].astype(BF16)
    wk = w_qkv[:, nq:nq + nk].astype(BF16)
    wv = w_qkv[:, nq + nk:].astype(BF16)
    const = lambda i: (0, 0)
    row = lambda i: (i, 0)
    res = dict(pipeline_mode=pl.Buffered(1))
    widths = (nq, nk, nk)
    return pl.pallas_call(
        _swa_qkv_kernel,
        grid=(t // tm,),
        in_specs=[pl.BlockSpec((tm, d), row), pl.BlockSpec((1, d), const),
                  pl.BlockSpec((1, N_SUB * 3, d), lambda i: (i // tiles_per_seq, 0, 0))]
        + [pl.BlockSpec((d, n), const, **res) for n in widths],
        out_specs=[pl.BlockSpec((tm, n), row) for n in widths],
        out_shape=[jax.ShapeDtypeStruct((t, n), BF16) for n in widths],
        scratch_shapes=[pltpu.VMEM((tm, d), BF16)],
        compiler_params=_cparams(("parallel",)),
        name="swa_qkv_proj",
    )(x2, g.reshape(1, d), mod, wq, wk, wv)


def _swa_kernel(q_ref, kc_ref, kp_ref, vc_ref, vp_ref, qn_ref, kn_ref, sink_ref, o_ref):
    T = SW_BLOCK
    wk = kc_ref.shape[2]
    first_key = jnp.where(pl.program_id(1) > 0, 0, T)

    def head_norm(x, g):
        w = x.shape[1]
        head_ones = jnp.where((_iota((w, w), 0) >> 6) == (_iota((w, w), 1) >> 6), 1.0, 0.0).astype(BF16)
        ms = _dot((x * x).astype(BF16), head_ones) * (1.0 / HEAD)
        return x * lax.rsqrt(ms + NORM_EPS) * g

    kcat = jnp.concatenate([kp_ref[0], kc_ref[0]], axis=0).astype(F32)
    kcat = head_norm(kcat, kn_ref[...]).astype(BF16)
    vcat = jnp.concatenate([vp_ref[0], vc_ref[0]], axis=0).astype(BF16)

    qi = _iota((T, 2 * T), 0)
    si = _iota((T, 2 * T), 1)
    rel = qi + T - si
    mask = (rel >= 0) & (rel < T) & (si >= first_key)
    shared = (kcat, vcat, mask, si == qi, head_norm)
    _run_interleaved([_swa_chain(kv, shared, q_ref, qn_ref, sink_ref, o_ref) for kv in range(wk // HEAD)])


def _swa_chain(kv, shared, q_ref, qn_ref, sink_ref, o_ref):
    T = SW_BLOCK
    kcat, vcat, mask, sink_slot, head_norm = shared
    wk = kcat.shape[1]
    gq = q_ref.shape[2] // wk
    wq = gq * HEAD
    scale = HEAD ** -0.5
    lo, hi = kv * wq, (kv + 1) * wq

    rep = jnp.where(_iota((wk, wq), 0) == kv * HEAD + (_iota((wk, wq), 1) & (HEAD - 1)), 1.0, 0.0).astype(BF16)
    k_rep = _dot(kcat, rep).astype(BF16)
    v_rep = _dot(vcat, rep).astype(BF16)
    q = head_norm(q_ref[0, :, lo:hi].astype(F32), qn_ref[...]).astype(BF16)
    yield
    lane_head = _iota((T, wq), 1) >> 6
    scores = []
    for j in range(gq):
        q_j = jnp.where(lane_head == j, q, jnp.zeros((), BF16))
        scores.append(_dot_nt(q_j, k_rep) * scale)
        yield
    ones = jnp.ones((2 * T, wq), BF16)
    probs = []
    for j in range(gq):
        sink = sink_ref[0:1, kv * gq + j:kv * gq + j + 1]
        s = jnp.where(sink_slot, sink, jnp.where(mask, scores[j], NEG_INF))
        m = jnp.max(s, axis=-1, keepdims=True)
        yield
        probs.append(jnp.exp(s - m).astype(BF16))
    o = jnp.zeros((T, wq), F32)
    for j in range(gq):
        denom = _dot(probs[j], ones)
        o_j = _dot(jnp.where(sink_slot, jnp.zeros((), BF16), probs[j]), v_rep)
        yield
        o = o + jnp.where(lane_head == j, o_j / denom, 0.0)
    o_ref[0, :, lo:hi] = o.astype(o_ref.dtype)


def _swa(q, k, v, q_norm, k_norm, sinks):
    bsz, seq, dq = q.shape
    dk = k.shape[2]
    nq = dq // HEAD
    cur = lambda b, i: (b, i, 0)
    prev = lambda b, i: (b, jnp.maximum(i - 1, 0), 0)
    const = lambda b, i: (0, 0)
    qn = jnp.tile(q_norm, dq // SW_KV_HEADS // HEAD).reshape(1, -1)
    kn = jnp.tile(k_norm, dk // HEAD).reshape(1, dk)
    return pl.pallas_call(
        _swa_kernel,
        grid=(bsz, seq // SW_BLOCK),
        in_specs=[pl.BlockSpec((1, SW_BLOCK, dq), cur),
                  pl.BlockSpec((1, SW_BLOCK, dk), cur), pl.BlockSpec((1, SW_BLOCK, dk), prev),
                  pl.BlockSpec((1, SW_BLOCK, dk), cur), pl.BlockSpec((1, SW_BLOCK, dk), prev),
                  pl.BlockSpec(qn.shape, const), pl.BlockSpec((1, dk), const),
                  pl.BlockSpec((1, nq), const)],
        out_specs=pl.BlockSpec((1, SW_BLOCK, dq), cur),
        out_shape=jax.ShapeDtypeStruct((bsz, seq, dq), BF16),
        compiler_params=_cparams(("parallel", "parallel")),
        name="swa_sink_attention",
    )(q, k, k, v, v, qn, kn, sinks.reshape(1, nq))


def kernel(x, c, ada_w, ada_b, norm_g, ffn_w_in, ffn_w_out, rw_mu, rw_w_rkv, rw_w_o, rw_w0, rw_w1, rw_w2, rw_a0, rw_a1, rw_a2, rw_g1, rw_g2, rw_k_k, rw_k_a, rw_r_k, rw_ln_w, rw_ln_b, rw_v0, rw_v1, rw_v2, mb_w_in, mb_conv_w, mb_conv_b, mb_dt_bias, mb_A_log, mb_D, mb_norm_g, mb_w_out, sw_w_qkv, sw_q_norm, sw_k_norm, sw_sinks, sw_w_o):
    bsz, seq, d = x.shape
    depth = ada_w.shape[0]
    t = bsz * seq
    mods = _modulation(c, ada_w, ada_b)
    x2 = x.reshape(t, d)
    v_first = None
    for i in range(depth):
        mod = mods[i]
        x2 = _ffn(x2, norm_g[i, 0], mod, ffn_w_in[i, 0].astype(BF16), ffn_w_out[i, 0].astype(BF16),
                  sub=0, seq=seq)
        kind, j = i % 3, i // 3
        if kind == 0:
            vres = None if v_first is None else (rw_v0[j - 1], rw_v1[j - 1], rw_v2[j - 1])
            r, k, v, ld, a, gt = _rwkv_pre(
                x2, norm_g[i, 1], mod, rw_mu[j], rw_w_rkv[j], rw_w0[j], rw_w1[j], rw_w2[j],
                rw_a0[j], rw_a1[j], rw_a2[j], rw_g1[j], rw_g2[j], vres, v_first, seq=seq)
            if v_first is None:
                v_first = v
            sh = (bsz, seq, d)
            z = _rwkv_scan(r.reshape(sh), k.reshape(sh), v.reshape(sh), ld.reshape(sh), a.reshape(sh),
                           gt.reshape(sh), rw_k_k[j], rw_k_a[j], rw_r_k[j], rw_ln_w[j], rw_ln_b[j])
            mixer = (z.reshape(t, d), rw_w_o[j].astype(BF16))
        elif kind == 1:
            d_inner = mb_w_out.shape[1]
            z, xs, bm, cm, dt = _mamba_in(x2, norm_g[i, 1], mod, mb_w_in[j], seq=seq, d_inner=d_inner)
            y = _ssd(xs.reshape(bsz, seq, -1), bm.reshape(bsz, seq, -1), cm.reshape(bsz, seq, -1),
                     z.reshape(bsz, seq, -1), dt.reshape(bsz, seq, -1), mb_conv_w[j], mb_conv_b[j],
                     mb_dt_bias[j], mb_A_log[j], mb_D[j], mb_norm_g[j])
            mixer = (y.reshape(t, d_inner), mb_w_out[j].astype(BF16))
        else:
            q, k, v = _swa_qkv(x2, norm_g[i, 1], mod, sw_w_qkv[j], seq=seq)
            o = _swa(q.reshape(bsz, seq, -1), k.reshape(bsz, seq, -1), v.reshape(bsz, seq, -1),
                     sw_q_norm[j], sw_k_norm[j], sw_sinks[j])
            mixer = (o.reshape(t, -1), sw_w_o[j].astype(BF16))
        x2 = _ffn(x2, norm_g[i, 2], mod, ffn_w_in[i, 1].astype(BF16), ffn_w_out[i, 1].astype(BF16),
                  sub=2, seq=seq, mixer=mixer)
    return x2.reshape(bsz, seq, d)
```

```python
import functools

import jax
import jax.numpy as jnp
from jax import lax
from jax.experimental import pallas as pl
from jax.experimental.pallas import tpu as pltpu

F32 = jnp.float32
BF16 = jnp.bfloat16

NORM_EPS = 1e-6
MACARON_W = 0.5
N_SUB = 3
HEAD = 64
RW_GN_EPS = 64e-5
RW_CHUNK = 64
RW_GROUP = 2
MB_GROUPS = 8
MB_STATE = 128
MB_CONV = 4
MB_CHUNK = 128
SW_KV_HEADS = 4
SW_BLOCK = 128
NEG_INF = -1e30
LANE = 128
VMEM_LIMIT = 56 * 1024 * 1024


def _cparams(sem):
    return pltpu.CompilerParams(dimension_semantics=sem, vmem_limit_bytes=VMEM_LIMIT)


def _dot(a, b):
    return jnp.dot(a, b, preferred_element_type=F32)


def _dot_nt(a, b):
    return lax.dot_general(a, b, (((1,), (1,)), ((), ())), preferred_element_type=F32)


def _dot_tn(a, b):
    return lax.dot_general(a, b, (((0,), (0,)), ((), ())), preferred_element_type=F32)


def _split3(x):
    hi = x.astype(BF16)
    r1 = x - hi.astype(F32)
    mid = r1.astype(BF16)
    lo = (r1 - mid.astype(F32)).astype(BF16)
    return hi, mid, lo


def _dot_exact_lhs(m, x):
    hi, mid, lo = _split3(x)
    return _dot(m, hi) + _dot(m, mid) + _dot(m, lo)


def _sigmoid(x):
    return 0.5 + 0.5 * jnp.tanh(0.5 * x)


def _silu(x):
    h = 0.5 * x
    return h + h * jnp.tanh(h)


def _softplus(x):
    return jnp.maximum(x, 0.0) + jnp.log(1.0 + jnp.exp(-jnp.abs(x)))


def _adaln(x, g, shift, scale):
    ms = jnp.mean(x * x, axis=-1, keepdims=True)
    return (x * lax.rsqrt(ms + NORM_EPS)) * g * (1.0 + scale) + shift


def _iota(shape, axis):
    return lax.broadcasted_iota(jnp.int32, shape, axis)


def _shift_rows(cur, tail, s):
    rolled = pltpu.roll(cur, s, 0)
    head = jnp.where(_iota(tail.shape, 0) < s, pltpu.roll(tail, s, 0), rolled[0:8])
    return jnp.concatenate([head, rolled[8:]], axis=0)


def _run_interleaved(chains):
    while chains:
        chains = [ch for ch in chains if next(ch, "done") != "done"]


def _mod_kernel(c_ref, w_ref, b_ref, o_ref):
    ca = _silu(c_ref[...])
    a_hi, a_mid, a_lo = _split3(ca)
    w_hi, w_mid, w_lo = _split3(w_ref[0])
    acc = _dot(a_hi, w_hi) + (_dot(a_hi, w_mid) + _dot(a_mid, w_hi))
    acc = acc + (_dot(a_hi, w_lo) + _dot(a_mid, w_mid) + _dot(a_lo, w_hi))
    o_ref[0] = acc + b_ref[0]


def _modulation(c, ada_w, ada_b):
    depth, d, n = ada_w.shape
    bsz = c.shape[0]
    tn = 1152
    out = pl.pallas_call(
        _mod_kernel,
        grid=(depth, n // tn),
        in_specs=[
            pl.BlockSpec((bsz, d), lambda l, j: (0, 0)),
            pl.BlockSpec((1, d, tn), lambda l, j: (l, 0, j)),
            pl.BlockSpec((1, 1, tn), lambda l, j: (l, 0, j)),
        ],
        out_specs=pl.BlockSpec((1, bsz, tn), lambda l, j: (l, 0, j)),
        out_shape=jax.ShapeDtypeStruct((depth, bsz, n), F32),
        compiler_params=_cparams(("parallel", "parallel")),
        name="adaln_modulation",
    )(c, ada_w, ada_b.reshape(depth, 1, n))
    return out.reshape(depth, bsz, N_SUB * 3, d)


def _ffn_kernel(*refs, sub, fc, fused_mixer):
    if fused_mixer:
        x_ref, g_ref, mod_ref, win_ref, wout_ref, z_ref, wo_ref, o_ref, h_ref, acc_ref = refs
        res_ref = o_ref
    else:
        x_ref, g_ref, mod_ref, win_ref, wout_ref, o_ref, h_ref, acc_ref = refs
        res_ref = x_ref
    dff = wout_ref.shape[0]
    shift = mod_ref[0, 3 * sub:3 * sub + 1, :]
    scale = mod_ref[0, 3 * sub + 1:3 * sub + 2, :]
    gate = mod_ref[0, 3 * sub + 2:3 * sub + 3, :]

    if fused_mixer:
        o_ref[...] = x_ref[...] + mod_ref[0, 5:6, :] * _dot(z_ref[...], wo_ref[...])
    h_ref[...] = _adaln(res_ref[...], g_ref[...], shift, scale).astype(BF16)
    for c in range(dff // fc):
        h = h_ref[...]
        a = _dot(h, win_ref[:, c * fc:(c + 1) * fc])
        b = _dot(h, win_ref[:, dff + c * fc:dff + (c + 1) * fc])
        act = (_silu(a) * b).astype(BF16)
        part = _dot(act, wout_ref[c * fc:(c + 1) * fc, :])
        if c == 0:
            acc_ref[...] = part
        else:
            acc_ref[...] += part
    o_ref[...] = res_ref[...] + (MACARON_W * gate) * acc_ref[...]


def _ffn(x2, g, mod, w_in, w_out, *, sub, seq, mixer=None, tm=None, fc=256):
    t, d = x2.shape
    dff = w_out.shape[0]
    if tm is None:
        tm = 1024 if mixer is None else 512
    tm = min(tm, seq)
    tiles_per_seq = seq // tm
    resident = dict(pipeline_mode=pl.Buffered(1))
    args = [x2, g.reshape(1, d), mod, w_in, w_out]
    in_specs = [
        pl.BlockSpec((tm, d), lambda i: (i, 0)),
        pl.BlockSpec((1, d), lambda i: (0, 0)),
        pl.BlockSpec((1, N_SUB * 3, d), lambda i: (i // tiles_per_seq, 0, 0)),
        pl.BlockSpec((d, 2 * dff), lambda i: (0, 0), **resident),
        pl.BlockSpec((dff, d), lambda i: (0, 0), **resident),
    ]
    if mixer is not None:
        z2, w_o = mixer
        args += [z2, w_o]
        in_specs += [pl.BlockSpec((tm, z2.shape[1]), lambda i: (i, 0)),
                     pl.BlockSpec(w_o.shape, lambda i: (0, 0), **resident)]
    return pl.pallas_call(
        functools.partial(_ffn_kernel, sub=sub, fc=fc, fused_mixer=mixer is not None),
        grid=(t // tm,),
        in_specs=in_specs,
        out_specs=pl.BlockSpec((tm, d), lambda i: (i, 0)),
        out_shape=jax.ShapeDtypeStruct((t, d), F32),
        scratch_shapes=[pltpu.VMEM((tm, d), BF16), pltpu.VMEM((tm, d), F32)],
        compiler_params=_cparams(("parallel",)),
        name="macaron_ffn_mixer_out" if mixer is not None else "macaron_ffn",
    )(*args)


def _rwkv_pre_kernel(*refs, tiles_per_seq, has_vres):
    if has_vres:
        (x_ref, xp_ref, g_ref, mod_ref, mu_ref, wr_ref, wk_ref, wv_ref, w0_ref, w1_ref, w2_ref,
         a0_ref, a1_ref, a2_ref, g1_ref, g2_ref, v0_ref, v1_ref, v2_ref, vf_ref,
         r_ref, k_ref, v_ref, ld_ref, a_ref, gt_ref, h_ref, dx_ref) = refs
    else:
        (x_ref, xp_ref, g_ref, mod_ref, mu_ref, wr_ref, wk_ref, wv_ref, w0_ref, w1_ref, w2_ref,
         a0_ref, a1_ref, a2_ref, g1_ref, g2_ref,
         r_ref, k_ref, v_ref, ld_ref, a_ref, gt_ref, h_ref, dx_ref) = refs
    shift = mod_ref[0, 3:4, :]
    scale = mod_ref[0, 4:5, :]
    g = g_ref[...]
    h = _adaln(x_ref[...], g, shift, scale)
    first = (pl.program_id(0) % tiles_per_seq) == 0
    hp = _adaln(xp_ref[...], g, shift, scale)
    hp = jnp.where(first, 0.0, hp)
    h_ref[...] = h
    dx_ref[...] = _shift_rows(h, hp, 1) - h

    def chain(lo, hi, delay):
        for _ in range(delay):
            yield

        def mixed(j):
            return (h_ref[lo:hi, :] + dx_ref[lo:hi, :] * mu_ref[j:j + 1, :]).astype(BF16)

        w_mid = jnp.tanh(_dot(mixed(1), w1_ref[...])).astype(BF16)
        a_mid = _dot(mixed(4), a1_ref[...]).astype(BF16)
        yield
        wl = w0_ref[...] + _dot(w_mid, w2_ref[...])
        a_pre = a0_ref[...] + _dot(a_mid, a2_ref[...])
        g_mid = _sigmoid(_dot(mixed(5), g1_ref[...])).astype(BF16)
        yield
        r_ref[lo:hi, :] = _dot(mixed(0), wr_ref[...]).astype(r_ref.dtype)
        yield
        ld_ref[lo:hi, :] = -jnp.exp(-_softplus(-wl) - 0.5)
        k_ref[lo:hi, :] = _dot(mixed(2), wk_ref[...]).astype(k_ref.dtype)
        yield
        a_ref[lo:hi, :] = _sigmoid(a_pre).astype(a_ref.dtype)
        gt_ref[lo:hi, :] = _dot(g_mid, g2_ref[...]).astype(gt_ref.dtype)
        xv = mixed(3)
        if has_vres:
            mixv = _sigmoid(v0_ref[...] + _dot(_dot(xv, v1_ref[...]).astype(BF16), v2_ref[...]))
        yield
        v = _dot(xv, wv_ref[...])
        if has_vres:
            v = v + (vf_ref[lo:hi, :].astype(F32) - v) * mixv
        v_ref[lo:hi, :] = v.astype(v_ref.dtype)

    tm = h_ref.shape[0]
    nsplit = 2 if tm % 32 == 0 else 1
    rows = tm // nsplit
    _run_interleaved([chain(i * rows, (i + 1) * rows, 2 * i) for i in range(nsplit)])


def _pad_cols(w, n):
    return jnp.pad(w, ((0, 0), (0, n - w.shape[1])))


def _pad_rows(w, n):
    return jnp.pad(w, ((0, n - w.shape[0]), (0, 0)))


def _round_up(n, m):
    return (n + m - 1) // m * m


def _lora_pair(w_a, w_b):
    n = _round_up(w_a.shape[1], LANE)
    return _pad_cols(w_a, n).astype(BF16), _pad_rows(w_b, n).astype(BF16)


def _rwkv_pre(x2, g, mod, mu, w_rkv, w0, w1, w2, a0, a1, a2, g1, g2, vres, v_first, *, seq, tm=512):
    t, d = x2.shape
    tm = min(tm, seq)
    tiles_per_seq = seq // tm
    has_vres = vres is not None
    const = lambda i: (0, 0)
    res = dict(pipeline_mode=pl.Buffered(1))
    row = lambda i: (i, 0)
    w1p, w2p = _lora_pair(w1, w2)
    a1p, a2p = _lora_pair(a1, a2)
    g1p, g2p = _lora_pair(g1, g2)
    args = [x2, x2, g.reshape(1, d), mod, _pad_rows(mu, 8),
            w_rkv[0].astype(BF16), w_rkv[1].astype(BF16), w_rkv[2].astype(BF16),
            w0.reshape(1, d), w1p, w2p, a0.reshape(1, d), a1p, a2p, g1p, g2p]
    in_specs = [
        pl.BlockSpec((tm, d), row),
        pl.BlockSpec((8, d), lambda i: (jnp.maximum(i * (tm // 8) - 1, 0), 0)),
        pl.BlockSpec((1, d), const),
        pl.BlockSpec((1, N_SUB * 3, d), lambda i: (i // tiles_per_seq, 0, 0)),
        pl.BlockSpec((8, d), const),
        pl.BlockSpec((d, d), const, **res), pl.BlockSpec((d, d), const, **res),
        pl.BlockSpec((d, d), const, **res),
        pl.BlockSpec((1, d), const),
        pl.BlockSpec(w1p.shape, const, **res), pl.BlockSpec(w2p.shape, const, **res),
        pl.BlockSpec((1, d), const),
        pl.BlockSpec(a1p.shape, const, **res), pl.BlockSpec(a2p.shape, const, **res),
        pl.BlockSpec(g1p.shape, const, **res), pl.BlockSpec(g2p.shape, const, **res),
    ]
    if has_vres:
        v0, v1, v2 = vres
        v1p, v2p = _lora_pair(v1, v2)
        args += [v0.reshape(1, d), v1p, v2p, v_first]
        in_specs += [pl.BlockSpec((1, d), const), pl.BlockSpec(v1p.shape, const, **res),
                     pl.BlockSpec(v2p.shape, const, **res), pl.BlockSpec((tm, d), row)]
    out_dtypes = (BF16, BF16, BF16, F32, BF16, BF16)
    return pl.pallas_call(
        functools.partial(_rwkv_pre_kernel, tiles_per_seq=tiles_per_seq, has_vres=has_vres),
        grid=(t // tm,),
        in_specs=in_specs,
        out_specs=[pl.BlockSpec((tm, d), row)] * 6,
        out_shape=[jax.ShapeDtypeStruct((t, d), dt_) for dt_ in out_dtypes],
        scratch_shapes=[pltpu.VMEM((tm, d), F32), pltpu.VMEM((tm, d), F32)],
        compiler_params=_cparams(("parallel",)),
        name="rwkv7_projections",
    )(*args)


def _rwkv_scan_kernel(*refs):
    st_ref = refs[-1]

    @pl.when(pl.program_id(2) == 0)
    def _():
        st_ref[...] = jnp.zeros_like(st_ref)

    nb = refs[0].shape[0]
    ng = st_ref.shape[0] // nb
    _run_interleaved([_rwkv_chain(bi, gi, *refs) for bi in range(nb) for gi in range(ng)])


def _rwkv_chain(bi, gi, r_ref, k_ref, v_ref, ld_ref, a_ref, gt_ref, kk_ref, ka_ref, rk_ref,
                lnw_ref, lnb_ref, z_ref, st_ref):
    L = RW_CHUNK
    W = RW_GROUP * HEAD
    R = RW_GROUP * L
    lo, hi = gi * W, (gi + 1) * W
    si = bi * (st_ref.shape[0] // r_ref.shape[0]) + gi

    r = r_ref[bi, :, lo:hi].astype(F32)
    k = k_ref[bi, :, lo:hi].astype(F32)
    v = v_ref[bi, :, lo:hi].astype(F32)
    ld = ld_ref[bi, :, lo:hi]
    a = a_ref[bi, :, lo:hi].astype(F32)

    head_ones = jnp.where((_iota((W, W), 0) >> 6) == (_iota((W, W), 1) >> 6), 1.0, 0.0).astype(BF16)

    def head_sum(x, passes=1):
        hi_part = x.astype(BF16)
        out = _dot(hi_part, head_ones)
        if passes == 2:
            out = out + _dot((x - hi_part.astype(F32)).astype(BF16), head_ones)
        return out

    kkn = k * kk_ref[:, lo:hi]
    kk_ss = head_sum(kkn * kkn)
    tril = jnp.where(_iota((L, L), 0) >= _iota((L, L), 1), 1.0, 0.0).astype(BF16)
    cum = _dot_exact_lhs(tril, ld)
    yield
    kk = kkn / jnp.maximum(jnp.sqrt(kk_ss), 1e-12)
    kmod = k * (1.0 + (a - 1.0) * ka_ref[:, lo:hi])
    c_end = cum[L - 1:L, :]
    e_pos = jnp.exp(cum)
    e_neg = jnp.exp(-cum)
    e_prev = jnp.exp(cum - ld)
    e_end = jnp.exp(c_end - cum)
    kka = kk * a

    rows = _iota((R, W), 0)
    cols = _iota((R, W), 1)
    same_head = (rows >> 6) == (cols >> 6)

    def bd(x):
        xb = x.astype(BF16)
        return jnp.where(same_head, jnp.concatenate([xb] * RW_GROUP, axis=0), jnp.zeros((), BF16))

    ar = jnp.concatenate([-kk * e_prev, r * e_pos], axis=0).astype(BF16)
    bk_s = jnp.concatenate([bd(kka * e_neg), bd(kmod * e_neg)], axis=0)
    v_s = bd(v)
    prod = _dot_nt(ar, bk_s)
    st = st_ref[si]
    from_state = _dot_nt(ar, st.astype(BF16))
    yield
    mt = _iota((L, R), 0)
    ms = _iota((L, R), 1) & (L - 1)
    strict = mt > ms
    incl = mt >= ms
    a_ab = jnp.where(strict, prod[0:L, 0:R], 0.0)
    a_ak = jnp.where(strict, prod[0:L, R:2 * R], 0.0)
    a_rb = jnp.where(incl, prod[L:2 * L, 0:R], 0.0)
    a_rk = jnp.where(incl, prod[L:2 * L, R:2 * R], 0.0)

    def mm(x, y):
        return _dot(x.astype(BF16), bd(y))

    eye = jnp.where(mt == ms, 1.0, 0.0)
    a_d = jnp.where((mt >> 3) == (ms >> 3), a_ab, 0.0)
    a_d2 = mm(a_d, a_d)
    rhs = from_state[0:L] + _dot(a_ak.astype(BF16), v_s)
    bonus = head_sum(r * kmod * rk_ref[:, lo:hi]) * v
    yield
    a_d4 = mm(a_d2, a_d2)
    tinv = mm(eye + a_d, eye + a_d2)
    yield
    tinv = mm(tinv, eye + a_d4)
    yield
    for sh in range(3, L.bit_length() - 1):
        off = ((mt >> (sh + 1)) == (ms >> (sh + 1))) & ((mt >> sh) != (ms >> sh))
        half = mm(tinv, jnp.where(off, a_ab, 0.0))
        yield
        tinv = tinv + mm(half, tinv)
        yield

    u = mm(tinv, rhs)
    yield
    y = from_state[L:2 * L] + _dot(
        jnp.concatenate([a_rb.astype(BF16), a_rk.astype(BF16)], axis=1),
        jnp.concatenate([bd(u), v_s], axis=0))
    new_terms = _dot_tn(jnp.concatenate([u, v], axis=0).astype(BF16),
                        jnp.concatenate([kka * e_end, kmod * e_end], axis=0).astype(BF16))
    st_ref[si] = st * jnp.exp(c_end) + jnp.where(same_head, new_terms, 0.0)
    yield
    mean = head_sum(y, passes=2) * (1.0 / HEAD)
    yield
    yc = y - mean
    var = head_sum(yc * yc) * (1.0 / HEAD)
    yield
    yn = yc * lax.rsqrt(var + RW_GN_EPS) * lnw_ref[:, lo:hi] + lnb_ref[:, lo:hi]
    z_ref[bi, :, lo:hi] = ((yn + bonus) * gt_ref[bi, :, lo:hi].astype(F32)).astype(z_ref.dtype)


def _rwkv_scan(r, k, v, ld, a, gt, k_k, k_a, r_k, ln_w, ln_b, *, lanes=1024, nb=2):
    bsz, seq, d = r.shape
    assert RW_CHUNK == HEAD, "the side-by-side per-head matrix layout needs chunk length == head width"
    w = RW_GROUP * HEAD
    lanes = min(lanes, d)
    nb = min(nb, bsz)
    act = pl.BlockSpec((nb, RW_CHUNK, lanes), lambda b, g, c: (b, c, g))
    par = pl.BlockSpec((1, lanes), lambda b, g, c: (0, g))
    return pl.pallas_call(
        _rwkv_scan_kernel,
        grid=(bsz // nb, d // lanes, seq // RW_CHUNK),
        in_specs=[act] * 6 + [par] * 5,
        out_specs=act,
        out_shape=jax.ShapeDtypeStruct((bsz, seq, d), BF16),
        scratch_shapes=[pltpu.VMEM((nb * (lanes // w), w, w), F32)],
        compiler_params=_cparams(("parallel", "parallel", "arbitrary")),
        name="rwkv7_chunk_scan",
    )(r, k, v, ld, a, gt, k_k.reshape(1, d), k_a.reshape(1, d), r_k.reshape(1, d),
      ln_w.reshape(1, d), ln_b.reshape(1, d))


def _mamba_in_kernel(x_ref, g_ref, mod_ref, wz_ref, wx_ref, wb_ref, wc_ref, wdt_ref,
                     z_ref, xs_ref, b_ref, c_ref, dt_ref, h_ref):
    h_ref[...] = _adaln(x_ref[...], g_ref[...], mod_ref[0, 3:4, :], mod_ref[0, 4:5, :]).astype(BF16)
    z_ref[...] = _dot(h_ref[...], wz_ref[...]).astype(z_ref.dtype)
    xs_ref[...] = _dot(h_ref[...], wx_ref[...]).astype(xs_ref.dtype)
    b_ref[...] = _dot(h_ref[...], wb_ref[...]).astype(b_ref.dtype)
    c_ref[...] = _dot(h_ref[...], wc_ref[...]).astype(c_ref.dtype)
    dt_ref[...] = _dot(h_ref[...], wdt_ref[...])


def _mamba_in(x2, g, mod, w_in, *, seq, d_inner, tm=512):
    t, d = x2.shape
    gn = MB_GROUPS * MB_STATE
    tm = min(tm, seq)
    tiles_per_seq = seq // tm
    wz = w_in[:, :d_inner].astype(BF16)
    wx = w_in[:, d_inner:2 * d_inner].astype(BF16)
    wb = w_in[:, 2 * d_inner:2 * d_inner + gn].astype(BF16)
    wc = w_in[:, 2 * d_inner + gn:2 * d_inner + 2 * gn].astype(BF16)
    wdt = _pad_cols(w_in[:, 2 * d_inner + 2 * gn:], LANE).astype(BF16)
    const = lambda i: (0, 0)
    row = lambda i: (i, 0)
    res = dict(pipeline_mode=pl.Buffered(1))
    widths = (d_inner, d_inner, gn, gn, LANE)
    return pl.pallas_call(
        _mamba_in_kernel,
        grid=(t // tm,),
        in_specs=[pl.BlockSpec((tm, d), row), pl.BlockSpec((1, d), const),
                  pl.BlockSpec((1, N_SUB * 3, d), lambda i: (i // tiles_per_seq, 0, 0))]
        + [pl.BlockSpec((d, n), const, **res) for n in widths],
        out_specs=[pl.BlockSpec((tm, n), row) for n in widths],
        out_shape=[jax.ShapeDtypeStruct((t, n), dt_)
                   for n, dt_ in zip(widths, (BF16, BF16, BF16, BF16, F32))],
        scratch_shapes=[pltpu.VMEM((tm, d), BF16)],
        compiler_params=_cparams(("parallel",)),
        name="mamba2_in_proj",
    )(x2, g.reshape(1, d), mod, wz, wx, wb, wc, wdt)


def _ssd_kernel(xs_ref, b_ref, c_ref, z_ref, dt_ref, cwx_ref, cwb_ref, cwc_ref, cbx_ref, cbb_ref,
                cbc_ref, dtb_ref, alog_ref, dsk_ref, ng_ref, y_ref,
                st_ref, sx_ref, sb_ref, sc_ref):
    L = MB_CHUNK
    stages = ((sx_ref, xs_ref), (sb_ref, b_ref), (sc_ref, c_ref))

    @pl.when(pl.program_id(1) == 0)
    def _():
        st_ref[...] = jnp.zeros_like(st_ref)
        for s_ref, _ in stages:
            s_ref[0:8, :] = jnp.zeros((8, s_ref.shape[1]), F32)

    for s_ref, cur_ref in stages:
        s_ref[8:8 + L, :] = cur_ref[0].astype(F32)

    dt = _softplus(dt_ref[0] + dtb_ref[...])
    da = dt * (-jnp.exp(alog_ref[...]))
    tril = jnp.where(_iota((L, L), 0) >= _iota((L, L), 1), 1.0, 0.0).astype(BF16)
    acum = _dot_exact_lhs(tril, da)
    shared = (_split3(dt), _split3(acum))
    refs = (sx_ref, sb_ref, sc_ref, z_ref, cwx_ref, cwb_ref, cwc_ref, cbx_ref, cbb_ref, cbc_ref,
            dsk_ref, ng_ref, y_ref, st_ref)
    _run_interleaved([_ssd_chain(g, shared, *refs) for g in range(st_ref.shape[0])])

    for s_ref, _ in stages:
        s_ref[0:8, :] = s_ref[L:L + 8, :]


def _ssd_chain(g, shared, sx_ref, sb_ref, sc_ref, z_ref, cwx_ref, cwb_ref, cwc_ref, cbx_ref,
               cbb_ref, cbc_ref, dsk_ref, ng_ref, y_ref, st_ref):
    L = MB_CHUNK
    n = MB_STATE
    wx = st_ref.shape[2]
    hg = wx // HEAD
    dt3, ac3 = shared
    xl, xh = g * wx, (g + 1) * wx
    nl, nh = g * n, (g + 1) * n

    def conv_silu(s_ref, w_ref, bias_ref, lo, hi):
        staged = s_ref[:, lo:hi]
        prev = pltpu.roll(staged, 1, 0)
        w = [w_ref[j:j + 1, lo:hi] for j in range(MB_CONV)]
        older = pltpu.roll(staged * w[1] + prev * w[0], 2, 0)
        acc = bias_ref[:, lo:hi] + staged * w[3] + prev * w[2] + older
        return _silu(acc[8:8 + L])

    xs = conv_silu(sx_ref, cwx_ref, cbx_ref, xl, xh)
    bm = conv_silu(sb_ref, cwb_ref, cbb_ref, nl, nh).astype(BF16)
    cm = conv_silu(sc_ref, cwc_ref, cbc_ref, nl, nh).astype(BF16)

    head0 = hg * g
    sel_x = jnp.where(_iota((LANE, wx), 0) == head0 + (_iota((LANE, wx), 1) >> 6), 1.0, 0.0).astype(BF16)
    sel_r = jnp.where(_iota((8, LANE), 1) == head0 + _iota((8, LANE), 0), 1.0, 0.0).astype(BF16)
    sel_c = jnp.where(_iota((LANE, hg * L), 0) == head0 + (_iota((LANE, hg * L), 1) >> 7), 1.0, 0.0).astype(BF16)
    both = [jnp.concatenate([d_p, a_p], axis=0) for d_p, a_p in zip(dt3, ac3)]
    in_x = _dot(both[0], sel_x) + _dot(both[1], sel_x) + _dot(both[2], sel_x)
    dt_x, ac_x = in_x[0:L], in_x[L:2 * L]
    ac_row = _dot_nt(sel_r, ac3[0]) + _dot_nt(sel_r, ac3[1]) + _dot_nt(sel_r, ac3[2])
    ac_col = _dot(ac3[0], sel_c) + _dot(ac3[1], sel_c) + _dot(ac3[2], sel_c)
    cb = _dot_nt(cm, bm)
    st = st_ref[g]
    y_off = _dot(cm, st.astype(BF16))
    yield

    xdt = xs * dt_x
    causal = _iota((L, L), 0) >= _iota((L, L), 1)
    lane_head = _iota((L, wx), 1) >> 6
    xdt_b = xdt.astype(BF16)
    g_parts = []
    x_parts = []
    for j in range(hg):
        seg = jnp.where(causal, ac_col[:, j * L:(j + 1) * L] - ac_row[j:j + 1, :], -jnp.inf)
        g_parts.append((cb * jnp.exp(seg)).astype(BF16))
        x_parts.append(jnp.where(lane_head == j, xdt_b, jnp.zeros((), BF16)))
    y_diag = _dot(jnp.concatenate(g_parts, axis=1), jnp.concatenate(x_parts, axis=0))
    ac_last = ac_x[L - 1:L, :]
    xdec = (xdt * jnp.exp(ac_last - ac_x)).astype(BF16)
    st_ref[g] = st * jnp.exp(ac_last) + _dot_tn(bm, xdec)
    yield

    y = y_diag + y_off * jnp.exp(ac_x) + xs * dsk_ref[:, xl:xh]
    y = y * _silu(z_ref[0, :, xl:xh].astype(F32))
    ms = jnp.mean(y * y, axis=-1, keepdims=True)
    y_ref[0, :, xl:xh] = (y * lax.rsqrt(ms + NORM_EPS) * ng_ref[:, xl:xh]).astype(y_ref.dtype)


def _ssd(xs, bm, cm, z, dt, conv_w, conv_b, dt_bias, a_log, d_skip, norm_g):
    bsz, seq, d_inner = xs.shape
    n = MB_STATE
    gn = MB_GROUPS * n
    wx = d_inner // MB_GROUPS
    heads = d_inner // HEAD
    assert conv_w.shape[0] == MB_CONV == 4, "the in-kernel conv is written for 4 taps"
    cw = _pad_rows(conv_w, 8)
    cwx, cwb, cwc = cw[:, :d_inner], cw[:, d_inner:d_inner + gn], cw[:, d_inner + gn:]
    cb = conv_b.reshape(1, -1)
    cbx, cbb, cbc = cb[:, :d_inner], cb[:, d_inner:d_inner + gn], cb[:, d_inner + gn:]
    dtb = _pad_cols(dt_bias.reshape(1, heads), LANE)
    alog = _pad_cols(a_log.reshape(1, heads), LANE)
    dsk = jnp.repeat(d_skip, HEAD).reshape(1, d_inner)
    tok = lambda w: pl.BlockSpec((1, MB_CHUNK, w), lambda b, c: (b, c, 0))
    par = lambda r, w: pl.BlockSpec((r, w), lambda b, c: (0, 0))
    return pl.pallas_call(
        _ssd_kernel,
        grid=(bsz, seq // MB_CHUNK),
        in_specs=[tok(d_inner), tok(gn), tok(gn), tok(d_inner), tok(LANE),
                  par(8, d_inner), par(8, gn), par(8, gn), par(1, d_inner), par(1, gn), par(1, gn),
                  par(1, LANE), par(1, LANE), par(1, d_inner), par(1, d_inner)],
        out_specs=tok(d_inner),
        out_shape=jax.ShapeDtypeStruct((bsz, seq, d_inner), BF16),
        scratch_shapes=[pltpu.VMEM((MB_GROUPS, n, wx), F32),
                        pltpu.VMEM((8 + MB_CHUNK,d_inner), F32),
                        pltpu.VMEM((8 + MB_CHUNK,gn), F32),
                        pltpu.VMEM((8 + MB_CHUNK,gn), F32)],
        compiler_params=_cparams(("parallel", "arbitrary")),
        name="mamba2_conv_ssd",
    )(xs, bm, cm, z, dt, cwx, cwb, cwc, cbx, cbb, cbc, dtb, alog, dsk, norm_g.reshape(1, d_inner))


def _swa_qkv_kernel(x_ref, g_ref, mod_ref, wq_ref, wk_ref, wv_ref, q_ref, k_ref, v_ref, h_ref):
    h_ref[...] = _adaln(x_ref[...], g_ref[...], mod_ref[0, 3:4, :], mod_ref[0, 4:5, :]).astype(BF16)
    q_ref[...] = _dot(h_ref[...], wq_ref[...]).astype(q_ref.dtype)
    k_ref[...] = _dot(h_ref[...], wk_ref[...]).astype(k_ref.dtype)
    v_ref[...] = _dot(h_ref[...], wv_ref[...]).astype(v_ref.dtype)


def _swa_qkv(x2, g, mod, w_qkv, *, seq, tm=512):
    t, d = x2.shape
    nq = d
    nk = SW_KV_HEADS * HEAD
    tm = min(tm, seq)
    tiles_per_seq = seq // tm
    wq = w_qkv[:, :nq].astype(BF16)
    wk = w_qkv[:, nq:nq + nk].astype(BF16)
    wv = w_qkv[:, nq + nk:].astype(BF16)
    const = lambda i: (0, 0)
    row = lambda i: (i, 0)
    res = dict(pipeline_mode=pl.Buffered(1))
    widths = (nq, nk, nk)
    return pl.pallas_call(
        _swa_qkv_kernel,
        grid=(t // tm,),
        in_specs=[pl.BlockSpec((tm, d), row), pl.BlockSpec((1, d), const),
                  pl.BlockSpec((1, N_SUB * 3, d), lambda i: (i // tiles_per_seq, 0, 0))]
        + [pl.BlockSpec((d, n), const, **res) for n in widths],
        out_specs=[pl.BlockSpec((tm, n), row) for n in widths],
        out_shape=[jax.ShapeDtypeStruct((t, n), BF16) for n in widths],
        scratch_shapes=[pltpu.VMEM((tm, d), BF16)],
        compiler_params=_cparams(("parallel",)),
        name="swa_qkv_proj",
    )(x2, g.reshape(1, d), mod, wq, wk, wv)


def _swa_kernel(q_ref, kc_ref, kp_ref, vc_ref, vp_ref, qn_ref, kn_ref, sink_ref, o_ref):
    T = SW_BLOCK
    wk = kc_ref.shape[2]
    first_key = jnp.where(pl.program_id(1) > 0, 0, T)

    def head_norm(x, g):
        w = x.shape[1]
        head_ones = jnp.where((_iota((w, w), 0) >> 6) == (_iota((w, w), 1) >> 6), 1.0, 0.0).astype(BF16)
        ms = _dot((x * x).astype(BF16), head_ones) * (1.0 / HEAD)
        return x * lax.rsqrt(ms + NORM_EPS) * g

    kcat = jnp.concatenate([kp_ref[0], kc_ref[0]], axis=0).astype(F32)
    kcat = head_norm(kcat, kn_ref[...]).astype(BF16)
    vcat = jnp.concatenate([vp_ref[0], vc_ref[0]], axis=0).astype(BF16)

    qi = _iota((T, 2 * T), 0)
    si = _iota((T, 2 * T), 1)
    rel = qi + T - si
    mask = (rel >= 0) & (rel < T) & (si >= first_key)
    shared = (kcat, vcat, mask, si == qi, head_norm)
    _run_interleaved([_swa_chain(kv, shared, q_ref, qn_ref, sink_ref, o_ref) for kv in range(wk // HEAD)])


def _swa_chain(kv, shared, q_ref, qn_ref, sink_ref, o_ref):
    T = SW_BLOCK
    kcat, vcat, mask, sink_slot, head_norm = shared
    wk = kcat.shape[1]
    gq = q_ref.shape[2] // wk
    wq = gq * HEAD
    scale = HEAD ** -0.5
    lo, hi = kv * wq, (kv + 1) * wq

    rep = jnp.where(_iota((wk, wq), 0) == kv * HEAD + (_iota((wk, wq), 1) & (HEAD - 1)), 1.0, 0.0).astype(BF16)
    k_rep = _dot(kcat, rep).astype(BF16)
    v_rep = _dot(vcat, rep).astype(BF16)
    q = head_norm(q_ref[0, :, lo:hi].astype(F32), qn_ref[...]).astype(BF16)
    yield
    lane_head = _iota((T, wq), 1) >> 6
    scores = []
    for j in range(gq):
        q_j = jnp.where(lane_head == j, q, jnp.zeros((), BF16))
        scores.append(_dot_nt(q_j, k_rep) * scale)
        yield
    ones = jnp.ones((2 * T, wq), BF16)
    probs = []
    for j in range(gq):
        sink = sink_ref[0:1, kv * gq + j:kv * gq + j + 1]
        s = jnp.where(sink_slot, sink, jnp.where(mask, scores[j], NEG_INF))
        m = jnp.max(s, axis=-1, keepdims=True)
        yield
        probs.append(jnp.exp(s - m).astype(BF16))
    o = jnp.zeros((T, wq), F32)
    for j in range(gq):
        denom = _dot(probs[j], ones)
        o_j = _dot(jnp.where(sink_slot, jnp.zeros((), BF16), probs[j]), v_rep)
        yield
        o = o + jnp.where(lane_head == j, o_j / denom, 0.0)
    o_ref[0, :, lo:hi] = o.astype(o_ref.dtype)


def _swa(q, k, v, q_norm, k_norm, sinks):
    bsz, seq, dq = q.shape
    dk = k.shape[2]
    nq = dq // HEAD
    cur = lambda b, i: (b, i, 0)
    prev = lambda b, i: (b, jnp.maximum(i - 1, 0), 0)
    const = lambda b, i: (0, 0)
    qn = jnp.tile(q_norm, dq // SW_KV_HEADS // HEAD).reshape(1, -1)
    kn = jnp.tile(k_norm, dk // HEAD).reshape(1, dk)
    return pl.pallas_call(
        _swa_kernel,
        grid=(bsz, seq // SW_BLOCK),
        in_specs=[pl.BlockSpec((1, SW_BLOCK, dq), cur),
                  pl.BlockSpec((1, SW_BLOCK, dk), cur), pl.BlockSpec((1, SW_BLOCK, dk), prev),
                  pl.BlockSpec((1, SW_BLOCK, dk), cur), pl.BlockSpec((1, SW_BLOCK, dk), prev),
                  pl.BlockSpec(qn.shape, const), pl.BlockSpec((1, dk), const),
                  pl.BlockSpec((1, nq), const)],
        out_specs=pl.BlockSpec((1, SW_BLOCK, dq), cur),
        out_shape=jax.ShapeDtypeStruct((bsz, seq, dq), BF16),
        compiler_params=_cparams(("parallel", "parallel")),
        name="swa_sink_attention",
    )(q, k, k, v, v, qn, kn, sinks.reshape(1, nq))


def kernel(x, c, ada_w, ada_b, norm_g, ffn_w_in, ffn_w_out, rw_mu, rw_w_rkv, rw_w_o, rw_w0, rw_w1, rw_w2, rw_a0, rw_a1, rw_a2, rw_g1, rw_g2, rw_k_k, rw_k_a, rw_r_k, rw_ln_w, rw_ln_b, rw_v0, rw_v1, rw_v2, mb_w_in, mb_conv_w, mb_conv_b, mb_dt_bias, mb_A_log, mb_D, mb_norm_g, mb_w_out, sw_w_qkv, sw_q_norm, sw_k_norm, sw_sinks, sw_w_o):
    bsz, seq, d = x.shape
    depth = ada_w.shape[0]
    t = bsz * seq
    mods = _modulation(c, ada_w, ada_b)
    x2 = x.reshape(t, d)
    v_first = None
    for i in range(depth):
        mod = mods[i]
        x2 = _ffn(x2, norm_g[i, 0], mod, ffn_w_in[i, 0].astype(BF16), ffn_w_out[i, 0].astype(BF16),
                  sub=0, seq=seq)
        kind, j = i % 3, i // 3
        if kind == 0:
            vres = None if v_first is None else (rw_v0[j - 1], rw_v1[j - 1], rw_v2[j - 1])
            r, k, v, ld, a, gt = _rwkv_pre(
                x2, norm_g[i, 1], mod, rw_mu[j], rw_w_rkv[j], rw_w0[j], rw_w1[j], rw_w2[j],
                rw_a0[j], rw_a1[j], rw_a2[j], rw_g1[j], rw_g2[j], vres, v_first, seq=seq)
            if v_first is None:
                v_first = v
            sh = (bsz, seq, d)
            z = _rwkv_scan(r.reshape(sh), k.reshape(sh), v.reshape(sh), ld.reshape(sh), a.reshape(sh),
                           gt.reshape(sh), rw_k_k[j], rw_k_a[j], rw_r_k[j], rw_ln_w[j], rw_ln_b[j])
            mixer = (z.reshape(t, d), rw_w_o[j].astype(BF16))
        elif kind == 1:
            d_inner = mb_w_out.shape[1]
            z, xs, bm, cm, dt = _mamba_in(x2, norm_g[i, 1], mod, mb_w_in[j], seq=seq, d_inner=d_inner)
            y = _ssd(xs.reshape(bsz, seq, -1), bm.reshape(bsz, seq, -1), cm.reshape(bsz, seq, -1),
                     z.reshape(bsz, seq, -1), dt.reshape(bsz, seq, -1), mb_conv_w[j], mb_conv_b[j],
                     mb_dt_bias[j], mb_A_log[j], mb_D[j], mb_norm_g[j])
            mixer = (y.reshape(t, d_inner), mb_w_out[j].astype(BF16))
        else:
            q, k, v = _swa_qkv(x2, norm_g[i, 1], mod, sw_w_qkv[j], seq=seq)
            o = _swa(q.reshape(bsz, seq, -1), k.reshape(bsz, seq, -1), v.reshape(bsz, seq, -1),
                     sw_q_norm[j], sw_k_norm[j], sw_sinks[j])
            mixer = (o.reshape(t, -1), sw_w_o[j].astype(BF16))
        x2 = _ffn(x2, norm_g[i, 2], mod, ffn_w_in[i, 1].astype(BF16), ffn_w_out[i, 1].astype(BF16),
                  sub=2, seq=seq, mixer=mixer)
    return x2.reshape(bsz, seq, d)
```

```python
import functools

import jax
import jax.numpy as jnp
from jax import lax
from jax.experimental import pallas as pl
from jax.experimental.pallas import tpu as pltpu

F32 = jnp.float32
BF16 = jnp.bfloat16

NORM_EPS = 1e-6
MACARON_W = 0.5
N_SUB = 3
HEAD = 64
RW_GN_EPS = 64e-5
RW_CHUNK = 64
RW_GROUP = 2
MB_GROUPS = 8
MB_STATE = 128
MB_CONV = 4
MB_CHUNK = 128
SW_KV_HEADS = 4
SW_BLOCK = 128
NEG_INF = -1e30
LOG2E = 1.4426950408889634
LANE = 128
VMEM_LIMIT = 56 * 1024 * 1024


def _cparams(sem):
    return pltpu.CompilerParams(dimension_semantics=sem, vmem_limit_bytes=VMEM_LIMIT)


def _dot(a, b):
    return jnp.dot(a, b, preferred_element_type=F32)


def _dot_nt(a, b):
    return lax.dot_general(a, b, (((1,), (1,)), ((), ())), preferred_element_type=F32)


def _dot_tn(a, b):
    return lax.dot_general(a, b, (((0,), (0,)), ((), ())), preferred_element_type=F32)


def _split3(x):
    hi = x.astype(BF16)
    r1 = x - hi.astype(F32)
    mid = r1.astype(BF16)
    lo = (r1 - mid.astype(F32)).astype(BF16)
    return hi, mid, lo


def _dot_exact_lhs(m, x):
    hi, mid, lo = _split3(x)
    return _dot(m, hi) + _dot(m, mid) + _dot(m, lo)


def _sigmoid(x):
    return 0.5 + 0.5 * jnp.tanh(0.5 * x)


def _silu(x):
    h = 0.5 * x
    return h + h * jnp.tanh(h)


def _softplus(x):
    return jnp.maximum(x, 0.0) + jnp.log(1.0 + jnp.exp(-jnp.abs(x)))


def _adaln(x, g, shift, scale):
    ms = jnp.mean(x * x, axis=-1, keepdims=True)
    return (x * lax.rsqrt(ms + NORM_EPS)) * (g * (1.0 + scale)) + shift


def _iota(shape, axis):
    return lax.broadcasted_iota(jnp.int32, shape, axis)


def _shift_rows(cur, tail, s):
    rolled = pltpu.roll(cur, s, 0)
    head = jnp.where(_iota(tail.shape, 0) < s, pltpu.roll(tail, s, 0), rolled[0:8])
    return jnp.concatenate([head, rolled[8:]], axis=0)


def _run_interleaved(chains, shared_rhs=None):
    done = object()
    reqs = [next(ch, done) for ch in chains]
    while chains:
        live = [(ch, rq) for ch, rq in zip(chains, reqs) if rq is not done]
        asked = [rq[0] for _, rq in live if rq is not None]
        answers = iter(())
        if asked:
            passes = max(rq[1] for _, rq in live if rq is not None)
            x = jnp.concatenate(asked, axis=0)
            hi_part = x.astype(BF16)
            prod = _dot(hi_part, shared_rhs)
            if passes == 2:
                prod = prod + _dot((x - hi_part.astype(F32)).astype(BF16), shared_rhs)
            n = asked[0].shape[0]
            answers = iter([prod[i * n:(i + 1) * n] for i in range(len(asked))])
        chains, reqs = [], []
        for ch, rq in live:
            try:
                reqs.append(ch.send(next(answers) if rq is not None else None))
                chains.append(ch)
            except StopIteration:
                pass


def _mod_kernel(c_ref, w_ref, b_ref, o_ref):
    ca = _silu(c_ref[...])
    a_hi, a_mid, a_lo = _split3(ca)
    w_hi, w_mid, w_lo = _split3(w_ref[0])
    acc = _dot(a_hi, w_hi) + (_dot(a_hi, w_mid) + _dot(a_mid, w_hi))
    acc = acc + (_dot(a_hi, w_lo) + _dot(a_mid, w_mid) + _dot(a_lo, w_hi))
    o_ref[0] = acc + b_ref[0]


def _modulation(c, ada_w, ada_b):
    depth, d, n = ada_w.shape
    bsz = c.shape[0]
    tn = 1152
    out = pl.pallas_call(
        _mod_kernel,
        grid=(depth, n // tn),
        in_specs=[
            pl.BlockSpec((bsz, d), lambda l, j: (0, 0)),
            pl.BlockSpec((1, d, tn), lambda l, j: (l, 0, j)),
            pl.BlockSpec((1, 1, tn), lambda l, j: (l, 0, j)),
        ],
        out_specs=pl.BlockSpec((1, bsz, tn), lambda l, j: (l, 0, j)),
        out_shape=jax.ShapeDtypeStruct((depth, bsz, n), F32),
        compiler_params=_cparams(("parallel", "parallel")),
        name="adaln_modulation",
    )(c, ada_w, ada_b.reshape(depth, 1, n))
    return out.reshape(depth, bsz, N_SUB * 3, d)


def _ffn_kernel(*refs, sub, fc, fused_mixer):
    if fused_mixer:
        x_ref, g_ref, mod_ref, win_ref, wout_ref, z_ref, wo_ref, o_ref, h_ref, acc_ref = refs
        res_ref = o_ref
    else:
        x_ref, g_ref, mod_ref, win_ref, wout_ref, o_ref, h_ref, acc_ref = refs
        res_ref = x_ref
    dff = wout_ref.shape[0]
    shift = mod_ref[0, 3 * sub:3 * sub + 1, :]
    scale = mod_ref[0, 3 * sub + 1:3 * sub + 2, :]
    gate = mod_ref[0, 3 * sub + 2:3 * sub + 3, :]

    if fused_mixer:
        o_ref[...] = x_ref[...] + mod_ref[0, 5:6, :] * _dot(z_ref[...], wo_ref[...])
    h_ref[...] = _adaln(res_ref[...], g_ref[...], shift, scale).astype(BF16)
    for c in range(dff // fc):
        h = h_ref[...]
        a = _dot(h, win_ref[:, c * fc:(c + 1) * fc])
        b = _dot(h, win_ref[:, dff + c * fc:dff + (c + 1) * fc])
        act = (_silu(a) * b).astype(BF16)
        part = _dot(act, wout_ref[c * fc:(c + 1) * fc, :])
        if c == 0:
            acc_ref[...] = part
        else:
            acc_ref[...] += part
    o_ref[...] = res_ref[...] + (MACARON_W * gate) * acc_ref[...]


def _ffn(x2, g, mod, w_in, w_out, *, sub, seq, mixer=None, tm=None, fc=256):
    t, d = x2.shape
    dff = w_out.shape[0]
    if tm is None:
        tm = 1024 if mixer is None else 512
    tm = min(tm, seq)
    tiles_per_seq = seq // tm
    resident = dict(pipeline_mode=pl.Buffered(1))
    args = [x2, g.reshape(1, d), mod, w_in, w_out]
    in_specs = [
        pl.BlockSpec((tm, d), lambda i: (i, 0)),
        pl.BlockSpec((1, d), lambda i: (0, 0)),
        pl.BlockSpec((1, N_SUB * 3, d), lambda i: (i // tiles_per_seq, 0, 0)),
        pl.BlockSpec((d, 2 * dff), lambda i: (0, 0), **resident),
        pl.BlockSpec((dff, d), lambda i: (0, 0), **resident),
    ]
    if mixer is not None:
        z2, w_o = mixer
        args += [z2, w_o]
        in_specs += [pl.BlockSpec((tm, z2.shape[1]), lambda i: (i, 0)),
                     pl.BlockSpec(w_o.shape, lambda i: (0, 0), **resident)]
    return pl.pallas_call(
        functools.partial(_ffn_kernel, sub=sub, fc=fc, fused_mixer=mixer is not None),
        grid=(t // tm,),
        in_specs=in_specs,
        out_specs=pl.BlockSpec((tm, d), lambda i: (i, 0)),
        out_shape=jax.ShapeDtypeStruct((t, d), F32),
        scratch_shapes=[pltpu.VMEM((tm, d), BF16), pltpu.VMEM((tm, d), F32)],
        compiler_params=_cparams(("parallel",)),
        name="macaron_ffn_mixer_out" if mixer is not None else "macaron_ffn",
    )(*args)


def _rwkv_pre_kernel(*refs, tiles_per_seq, has_vres):
    if has_vres:
        (x_ref, xp_ref, g_ref, mod_ref, mu_ref, wr_ref, wk_ref, wv_ref, w0_ref, w1_ref, w2_ref,
         a0_ref, a1_ref, a2_ref, g1_ref, g2_ref, v0_ref, v1_ref, v2_ref, vf_ref,
         r_ref, k_ref, v_ref, ld_ref, a_ref, gt_ref, h_ref, dx_ref) = refs
    else:
        (x_ref, xp_ref, g_ref, mod_ref, mu_ref, wr_ref, wk_ref, wv_ref, w0_ref, w1_ref, w2_ref,
         a0_ref, a1_ref, a2_ref, g1_ref, g2_ref,
         r_ref, k_ref, v_ref, ld_ref, a_ref, gt_ref, h_ref, dx_ref) = refs
    shift = mod_ref[0, 3:4, :]
    scale = mod_ref[0, 4:5, :]
    g = g_ref[...]
    h = _adaln(x_ref[...], g, shift, scale)
    first = (pl.program_id(0) % tiles_per_seq) == 0
    hp = _adaln(xp_ref[...], g, shift, scale)
    hp = jnp.where(first, 0.0, hp)
    h_ref[...] = h
    dx_ref[...] = _shift_rows(h, hp, 1) - h

    def chain(lo, hi, delay):
        for _ in range(delay):
            yield

        def mixed(j):
            return (h_ref[lo:hi, :] + dx_ref[lo:hi, :] * mu_ref[j:j + 1, :]).astype(BF16)

        w_mid = jnp.tanh(_dot(mixed(1), w1_ref[...])).astype(BF16)
        a_mid = _dot(mixed(4), a1_ref[...]).astype(BF16)
        yield
        wl = w0_ref[...] + _dot(w_mid, w2_ref[...])
        a_pre = a0_ref[...] + _dot(a_mid, a2_ref[...])
        g_mid = _sigmoid(_dot(mixed(5), g1_ref[...])).astype(BF16)
        yield
        r_ref[lo:hi, :] = _dot(mixed(0), wr_ref[...]).astype(r_ref.dtype)
        yield
        ld_ref[lo:hi, :] = -jnp.exp(-_softplus(-wl) - 0.5)
        k_ref[lo:hi, :] = _dot(mixed(2), wk_ref[...]).astype(k_ref.dtype)
        yield
        a_ref[lo:hi, :] = _sigmoid(a_pre).astype(a_ref.dtype)
        gt_ref[lo:hi, :] = _dot(g_mid, g2_ref[...]).astype(gt_ref.dtype)
        xv = mixed(3)
        if has_vres:
            mixv = _sigmoid(v0_ref[...] + _dot(_dot(xv, v1_ref[...]).astype(BF16), v2_ref[...]))
        yield
        v = _dot(xv, wv_ref[...])
        if has_vres:
            v = v + (vf_ref[lo:hi, :].astype(F32) - v) * mixv
        v_ref[lo:hi, :] = v.astype(v_ref.dtype)

    tm = h_ref.shape[0]
    nsplit = 2 if tm % 32 == 0 else 1
    rows = tm // nsplit
    _run_interleaved([chain(i * rows, (i + 1) * rows, 2 * i) for i in range(nsplit)])


def _pad_cols(w, n):
    return jnp.pad(w, ((0, 0), (0, n - w.shape[1])))


def _pad_rows(w, n):
    return jnp.pad(w, ((0, n - w.shape[0]), (0, 0)))


def _round_up(n, m):
    return (n + m - 1) // m * m


def _lora_pair(w_a, w_b):
    n = _round_up(w_a.shape[1], LANE)
    return _pad_cols(w_a, n).astype(BF16), _pad_rows(w_b, n).astype(BF16)


def _rwkv_pre(x2, g, mod, mu, w_rkv, w0, w1, w2, a0, a1, a2, g1, g2, vres, v_first, *, seq, tm=512):
    t, d = x2.shape
    tm = min(tm, seq)
    tiles_per_seq = seq // tm
    has_vres = vres is not None
    const = lambda i: (0, 0)
    res = dict(pipeline_mode=pl.Buffered(1))
    row = lambda i: (i, 0)
    w1p, w2p = _lora_pair(w1, w2)
    a1p, a2p = _lora_pair(a1, a2)
    g1p, g2p = _lora_pair(g1, g2)
    args = [x2, x2, g.reshape(1, d), mod, _pad_rows(mu, 8),
            w_rkv[0].astype(BF16), w_rkv[1].astype(BF16), w_rkv[2].astype(BF16),
            w0.reshape(1, d), w1p, w2p, a0.reshape(1, d), a1p, a2p, g1p, g2p]
    in_specs = [
        pl.BlockSpec((tm, d), row),
        pl.BlockSpec((8, d), lambda i: (jnp.maximum(i * (tm // 8) - 1, 0), 0)),
        pl.BlockSpec((1, d), const),
        pl.BlockSpec((1, N_SUB * 3, d), lambda i: (i // tiles_per_seq, 0, 0)),
        pl.BlockSpec((8, d), const),
        pl.BlockSpec((d, d), const, **res), pl.BlockSpec((d, d), const, **res),
        pl.BlockSpec((d, d), const, **res),
        pl.BlockSpec((1, d), const),
        pl.BlockSpec(w1p.shape, const, **res), pl.BlockSpec(w2p.shape, const, **res),
        pl.BlockSpec((1, d), const),
        pl.BlockSpec(a1p.shape, const, **res), pl.BlockSpec(a2p.shape, const, **res),
        pl.BlockSpec(g1p.shape, const, **res), pl.BlockSpec(g2p.shape, const, **res),
    ]
    if has_vres:
        v0, v1, v2 = vres
        v1p, v2p = _lora_pair(v1, v2)
        args += [v0.reshape(1, d), v1p, v2p, v_first]
        in_specs += [pl.BlockSpec((1, d), const), pl.BlockSpec(v1p.shape, const, **res),
                     pl.BlockSpec(v2p.shape, const, **res), pl.BlockSpec((tm, d), row)]
    out_dtypes = (BF16, BF16, BF16, F32, BF16, BF16)
    return pl.pallas_call(
        functools.partial(_rwkv_pre_kernel, tiles_per_seq=tiles_per_seq, has_vres=has_vres),
        grid=(t // tm,),
        in_specs=in_specs,
        out_specs=[pl.BlockSpec((tm, d), row)] * 6,
        out_shape=[jax.ShapeDtypeStruct((t, d), dt_) for dt_ in out_dtypes],
        scratch_shapes=[pltpu.VMEM((tm, d), F32), pltpu.VMEM((tm, d), F32)],
        compiler_params=_cparams(("parallel",)),
        name="rwkv7_projections",
    )(*args)


def _rwkv_scan_kernel(*refs):
    st_ref = refs[-1]

    @pl.when(pl.program_id(2) == 0)
    def _():
        st_ref[...] = jnp.zeros_like(st_ref)

    nb = refs[0].shape[0]
    ng = st_ref.shape[0] // nb
    w = st_ref.shape[1]
    head_ones = jnp.where((_iota((w, w), 0) >> 6) == (_iota((w, w), 1) >> 6), 1.0, 0.0).astype(BF16)
    _run_interleaved([_rwkv_chain(bi, gi, *refs) for bi in range(nb) for gi in range(ng)],
                     shared_rhs=head_ones)


def _rwkv_chain(bi, gi, r_ref, k_ref, v_ref, ld_ref, a_ref, gt_ref, kk_ref, ka_ref, rk_ref,
                lnw_ref, lnb_ref, z_ref, st_ref):
    L = RW_CHUNK
    W = RW_GROUP * HEAD
    R = RW_GROUP * L
    lo, hi = gi * W, (gi + 1) * W
    si = bi * (st_ref.shape[0] // r_ref.shape[0]) + gi

    r = r_ref[bi, :, lo:hi].astype(F32)
    k = k_ref[bi, :, lo:hi].astype(F32)
    v = v_ref[bi, :, lo:hi].astype(F32)
    ld = ld_ref[bi, :, lo:hi]
    a = a_ref[bi, :, lo:hi].astype(F32)

    kkn = k * kk_ref[:, lo:hi]
    tril = jnp.where(_iota((L, L), 0) >= _iota((L, L), 1), 1.0, 0.0).astype(BF16)
    cum = _dot_exact_lhs(tril, ld)
    kk_ss = yield kkn * kkn, 1
    kk = kkn / jnp.maximum(jnp.sqrt(kk_ss), 1e-12)
    kmod = k * (1.0 + (a - 1.0) * ka_ref[:, lo:hi])
    c_end = cum[L - 1:L, :]
    e_pos = jnp.exp(cum)
    e_neg = jnp.exp(-cum)
    e_prev = jnp.exp(cum - ld)
    e_end = jnp.exp(c_end - cum)
    kka = kk * a

    rows = _iota((R, W), 0)
    cols = _iota((R, W), 1)
    same_head = (rows >> 6) == (cols >> 6)

    def bd(x):
        xb = x.astype(BF16)
        return jnp.where(same_head, jnp.concatenate([xb] * RW_GROUP, axis=0), jnp.zeros((), BF16))

    ar = jnp.concatenate([-kk * e_prev, r * e_pos], axis=0).astype(BF16)
    bk_s = jnp.concatenate([bd(kka * e_neg), bd(kmod * e_neg)], axis=0)
    v_s = bd(v)
    prod = _dot_nt(ar, bk_s)
    st = st_ref[si]
    from_state = _dot_nt(ar, st.astype(BF16))
    yield
    mt = _iota((L, R), 0)
    ms = _iota((L, R), 1) & (L - 1)
    strict = mt > ms
    incl = mt >= ms
    a_ab = jnp.where(strict, prod[0:L, 0:R], 0.0)
    a_ak = jnp.where(strict, prod[0:L, R:2 * R], 0.0)
    a_rb = jnp.where(incl, prod[L:2 * L, 0:R], 0.0)
    a_rk = jnp.where(incl, prod[L:2 * L, R:2 * R], 0.0)

    def mm(x, y):
        return _dot(x.astype(BF16), bd(y))

    eye = jnp.where(mt == ms, 1.0, 0.0)
    a_d = jnp.where((mt >> 3) == (ms >> 3), a_ab, 0.0)
    a_d2 = mm(a_d, a_d)
    rhs = from_state[0:L] + _dot(a_ak.astype(BF16), v_s)
    bonus = (yield r * kmod * rk_ref[:, lo:hi], 1) * v
    a_d4 = mm(a_d2, a_d2)
    tinv = mm(eye + a_d, eye + a_d2)
    yield
    tinv = mm(tinv, eye + a_d4)
    yield
    for sh in range(3, L.bit_length() - 1):
        off = ((mt >> (sh + 1)) == (ms >> (sh + 1))) & ((mt >> sh) != (ms >> sh))
        half = mm(tinv, jnp.where(off, a_ab, 0.0))
        yield
        tinv = tinv + mm(half, tinv)
        yield

    u = mm(tinv, rhs)
    yield
    y = from_state[L:2 * L] + _dot(
        jnp.concatenate([a_rb.astype(BF16), a_rk.astype(BF16)], axis=1),
        jnp.concatenate([bd(u), v_s], axis=0))
    new_terms = _dot_tn(jnp.concatenate([u, v], axis=0).astype(BF16),
                        jnp.concatenate([kka * e_end, kmod * e_end], axis=0).astype(BF16))
    st_ref[si] = st * jnp.exp(c_end) + jnp.where(same_head, new_terms, 0.0)
    mean = (yield y, 2) * (1.0 / HEAD)
    yc = y - mean
    var = (yield yc * yc, 1) * (1.0 / HEAD)
    yn = yc * lax.rsqrt(var + RW_GN_EPS) * lnw_ref[:, lo:hi] + lnb_ref[:, lo:hi]
    z_ref[bi, :, lo:hi] = ((yn + bonus) * gt_ref[bi, :, lo:hi].astype(F32)).astype(z_ref.dtype)


def _rwkv_scan(r, k, v, ld, a, gt, k_k, k_a, r_k, ln_w, ln_b, *, lanes=1024, nb=4):
    bsz, seq, d = r.shape
    assert RW_CHUNK == HEAD, "the side-by-side per-head matrix layout needs chunk length == head width"
    w = RW_GROUP * HEAD
    lanes = min(lanes, d)
    nb = min(nb, bsz)
    act = pl.BlockSpec((nb, RW_CHUNK, lanes), lambda b, g, c: (b, c, g))
    par = pl.BlockSpec((1, lanes), lambda b, g, c: (0, g))
    return pl.pallas_call(
        _rwkv_scan_kernel,
        grid=(bsz // nb, d // lanes, seq // RW_CHUNK),
        in_specs=[act] * 6 + [par] * 5,
        out_specs=act,
        out_shape=jax.ShapeDtypeStruct((bsz, seq, d), BF16),
        scratch_shapes=[pltpu.VMEM((nb * (lanes // w), w, w), F32)],
        compiler_params=_cparams(("parallel", "parallel", "arbitrary")),
        name="rwkv7_chunk_scan",
    )(r, k, v, ld, a, gt, k_k.reshape(1, d), k_a.reshape(1, d), r_k.reshape(1, d),
      ln_w.reshape(1, d), ln_b.reshape(1, d))


def _mamba_in_kernel(x_ref, g_ref, mod_ref, wz_ref, wx_ref, wb_ref, wc_ref, wdt_ref,
                     z_ref, xs_ref, b_ref, c_ref, dt_ref, h_ref):
    h_ref[...] = _adaln(x_ref[...], g_ref[...], mod_ref[0, 3:4, :], mod_ref[0, 4:5, :]).astype(BF16)
    z_ref[...] = _dot(h_ref[...], wz_ref[...]).astype(z_ref.dtype)
    xs_ref[...] = _dot(h_ref[...], wx_ref[...]).astype(xs_ref.dtype)
    b_ref[...] = _dot(h_ref[...], wb_ref[...]).astype(b_ref.dtype)
    c_ref[...] = _dot(h_ref[...], wc_ref[...]).astype(c_ref.dtype)
    dt_ref[...] = _dot(h_ref[...], wdt_ref[...])


def _mamba_in(x2, g, mod, w_in, *, seq, d_inner, tm=512):
    t, d = x2.shape
    gn = MB_GROUPS * MB_STATE
    tm = min(tm, seq)
    tiles_per_seq = seq // tm
    wz = w_in[:, :d_inner].astype(BF16)
    wx = w_in[:, d_inner:2 * d_inner].astype(BF16)
    wb = w_in[:, 2 * d_inner:2 * d_inner + gn].astype(BF16)
    wc = w_in[:, 2 * d_inner + gn:2 * d_inner + 2 * gn].astype(BF16)
    wdt = _pad_cols(w_in[:, 2 * d_inner + 2 * gn:], LANE).astype(BF16)
    const = lambda i: (0, 0)
    row = lambda i: (i, 0)
    res = dict(pipeline_mode=pl.Buffered(1))
    widths = (d_inner, d_inner, gn, gn, LANE)
    return pl.pallas_call(
        _mamba_in_kernel,
        grid=(t // tm,),
        in_specs=[pl.BlockSpec((tm, d), row), pl.BlockSpec((1, d), const),
                  pl.BlockSpec((1, N_SUB * 3, d), lambda i: (i // tiles_per_seq, 0, 0))]
        + [pl.BlockSpec((d, n), const, **res) for n in widths],
        out_specs=[pl.BlockSpec((tm, n), row) for n in widths],
        out_shape=[jax.ShapeDtypeStruct((t, n), dt_)
                   for n, dt_ in zip(widths, (BF16, BF16, BF16, BF16, F32))],
        scratch_shapes=[pltpu.VMEM((tm, d), BF16)],
        compiler_params=_cparams(("parallel",)),
        name="mamba2_in_proj",
    )(x2, g.reshape(1, d), mod, wz, wx, wb, wc, wdt)


def _ssd_kernel(xs_ref, b_ref, c_ref, z_ref, dt_ref, cwx_ref, cwb_ref, cwc_ref, cbx_ref, cbb_ref,
                cbc_ref, dtb_ref, alog_ref, dsk_ref, ng_ref, y_ref,
                st_ref, sx_ref, sb_ref, sc_ref):
    L = MB_CHUNK
    stages = ((sx_ref, xs_ref), (sb_ref, b_ref), (sc_ref, c_ref))

    @pl.when(pl.program_id(1) == 0)
    def _():
        st_ref[...] = jnp.zeros_like(st_ref)
        for s_ref, _ in stages:
            s_ref[0:8, :] = jnp.zeros((8, s_ref.shape[1]), F32)

    for s_ref, cur_ref in stages:
        s_ref[8:8 + L, :] = cur_ref[0].astype(F32)

    dt = _softplus(dt_ref[0] + dtb_ref[...])
    da = dt * (-jnp.exp(alog_ref[...]))
    tril = jnp.where(_iota((L, L), 0) >= _iota((L, L), 1), 1.0, 0.0).astype(BF16)
    acum = _dot_exact_lhs(tril, da) * LOG2E
    shared = (_split3(dt), _split3(acum))
    refs = (sx_ref, sb_ref, sc_ref, z_ref, cwx_ref, cwb_ref, cwc_ref, cbx_ref, cbb_ref, cbc_ref,
            dsk_ref, ng_ref, y_ref, st_ref)
    _run_interleaved([_ssd_chain(g, shared, *refs) for g in range(st_ref.shape[0])])

    for s_ref, _ in stages:
        s_ref[0:8, :] = s_ref[L:L + 8, :]


def _ssd_chain(g, shared, sx_ref, sb_ref, sc_ref, z_ref, cwx_ref, cwb_ref, cwc_ref, cbx_ref,
               cbb_ref, cbc_ref, dsk_ref, ng_ref, y_ref, st_ref):
    L = MB_CHUNK
    n = MB_STATE
    wx = st_ref.shape[2]
    hg = wx // HEAD
    dt3, ac3 = shared
    xl, xh = g * wx, (g + 1) * wx
    nl, nh = g * n, (g + 1) * n

    def conv_silu(s_ref, w_ref, bias_ref, lo, hi):
        staged = s_ref[:, lo:hi]
        prev = pltpu.roll(staged, 1, 0)
        w = [w_ref[j:j + 1, lo:hi] for j in range(MB_CONV)]
        older = pltpu.roll(staged * w[1] + prev * w[0], 2, 0)
        acc = bias_ref[:, lo:hi] + staged * w[3] + prev * w[2] + older
        return _silu(acc[8:8 + L])

    xs = conv_silu(sx_ref, cwx_ref, cbx_ref, xl, xh)
    bm = conv_silu(sb_ref, cwb_ref, cbb_ref, nl, nh).astype(BF16)
    cm = conv_silu(sc_ref, cwc_ref, cbc_ref, nl, nh).astype(BF16)

    head0 = hg * g
    sel_x = jnp.where(_iota((LANE, wx), 0) == head0 + (_iota((LANE, wx), 1) >> 6), 1.0, 0.0).astype(BF16)
    sel_r = jnp.where(_iota((8, LANE), 1) == head0 + _iota((8, LANE), 0), 1.0, 0.0).astype(BF16)
    sel_c = jnp.where(_iota((LANE, hg * L), 0) == head0 + (_iota((LANE, hg * L), 1) >> 7), 1.0, 0.0).astype(BF16)
    both = [jnp.concatenate([d_p, a_p], axis=0) for d_p, a_p in zip(dt3, ac3)]
    in_x = _dot(both[0], sel_x) + _dot(both[1], sel_x) + _dot(both[2], sel_x)
    dt_x, ac_x = in_x[0:L], in_x[L:2 * L]
    ac_row = _dot_nt(sel_r, ac3[0]) + _dot_nt(sel_r, ac3[1]) + _dot_nt(sel_r, ac3[2])
    ac_col = _dot(ac3[0], sel_c) + _dot(ac3[1], sel_c) + _dot(ac3[2], sel_c)
    cb = _dot_nt(cm, bm)
    st = st_ref[g]
    y_off = _dot(cm, st.astype(BF16))
    yield

    xdt = xs * dt_x
    causal = _iota((L, L), 0) >= _iota((L, L), 1)
    lane_head = _iota((L, wx), 1) >> 6
    xdt_b = xdt.astype(BF16)
    g_parts = []
    x_parts = []
    for j in range(hg):
        seg = jnp.where(causal, ac_col[:, j * L:(j + 1) * L] - ac_row[j:j + 1, :], -jnp.inf)
        g_parts.append((cb * jnp.exp2(seg)).astype(BF16))
        x_parts.append(jnp.where(lane_head == j, xdt_b, jnp.zeros((), BF16)))
    y_diag = _dot(jnp.concatenate(g_parts, axis=1), jnp.concatenate(x_parts, axis=0))
    ac_last = ac_x[L - 1:L, :]
    xdec = (xdt * jnp.exp2(ac_last - ac_x)).astype(BF16)
    st_ref[g] = st * jnp.exp2(ac_last) + _dot_tn(bm, xdec)
    yield

    y = y_diag + y_off * jnp.exp2(ac_x) + xs * dsk_ref[:, xl:xh]
    y = y * _silu(z_ref[0, :, xl:xh].astype(F32))
    ms = jnp.mean(y * y, axis=-1, keepdims=True)
    y_ref[0, :, xl:xh] = (y * lax.rsqrt(ms + NORM_EPS) * ng_ref[:, xl:xh]).astype(y_ref.dtype)


def _ssd(xs, bm, cm, z, dt, conv_w, conv_b, dt_bias, a_log, d_skip, norm_g):
    bsz, seq, d_inner = xs.shape
    n = MB_STATE
    gn = MB_GROUPS * n
    wx = d_inner // MB_GROUPS
    heads = d_inner // HEAD
    assert conv_w.shape[0] == MB_CONV == 4, "the in-kernel conv is written for 4 taps"
    cw = _pad_rows(conv_w, 8)
    cwx, cwb, cwc = cw[:, :d_inner], cw[:, d_inner:d_inner + gn], cw[:, d_inner + gn:]
    cb = conv_b.reshape(1, -1)
    cbx, cbb, cbc = cb[:, :d_inner], cb[:, d_inner:d_inner + gn], cb[:, d_inner + gn:]
    dtb = _pad_cols(dt_bias.reshape(1, heads), LANE)
    alog = _pad_cols(a_log.reshape(1, heads), LANE)
    dsk = jnp.repeat(d_skip, HEAD).reshape(1, d_inner)
    tok = lambda w: pl.BlockSpec((1, MB_CHUNK, w), lambda b, c: (b, c, 0))
    par = lambda r, w: pl.BlockSpec((r, w), lambda b, c: (0, 0))
    return pl.pallas_call(
        _ssd_kernel,
        grid=(bsz, seq // MB_CHUNK),
        in_specs=[tok(d_inner), tok(gn), tok(gn), tok(d_inner), tok(LANE),
                  par(8, d_inner), par(8, gn), par(8, gn), par(1, d_inner), par(1, gn), par(1, gn),
                  par(1, LANE), par(1, LANE), par(1, d_inner), par(1, d_inner)],
        out_specs=tok(d_inner),
        out_shape=jax.ShapeDtypeStruct((bsz, seq, d_inner), BF16),
        scratch_shapes=[pltpu.VMEM((MB_GROUPS, n, wx), F32),
                        pltpu.VMEM((8 + MB_CHUNK,d_inner), F32),
                        pltpu.VMEM((8 + MB_CHUNK,gn), F32),
                        pltpu.VMEM((8 + MB_CHUNK,gn), F32)],
        compiler_params=_cparams(("parallel", "arbitrary")),
        name="mamba2_conv_ssd",
    )(xs, bm, cm, z, dt, cwx, cwb, cwc, cbx, cbb, cbc, dtb, alog, dsk, norm_g.reshape(1, d_inner))


def _swa_qkv_kernel(x_ref, g_ref, mod_ref, wq_ref, wk_ref, wv_ref, q_ref, k_ref, v_ref, h_ref):
    h_ref[...] = _adaln(x_ref[...], g_ref[...], mod_ref[0, 3:4, :], mod_ref[0, 4:5, :]).astype(BF16)
    q_ref[...] = _dot(h_ref[...], wq_ref[...]).astype(q_ref.dtype)
    k_ref[...] = _dot(h_ref[...], wk_ref[...]).astype(k_ref.dtype)
    v_ref[...] = _dot(h_ref[...], wv_ref[...]).astype(v_ref.dtype)


def _swa_qkv(x2, g, mod, w_qkv, *, seq, tm=512):
    t, d = x2.shape
    nq = d
    nk = SW_KV_HEADS * HEAD
    tm = min(tm, seq)
    tiles_per_seq = seq // tm
    wq = w_qkv[:, :nq].astype(BF16)
    wk = w_qkv[:, nq:nq + nk].astype(BF16)
    wv = w_qkv[:, nq + nk:].astype(BF16)
    const = lambda i: (0, 0)
    row = lambda i: (i, 0)
    res = dict(pipeline_mode=pl.Buffered(1))
    widths = (nq, nk, nk)
    return pl.pallas_call(
        _swa_qkv_kernel,
        grid=(t // tm,),
        in_specs=[pl.BlockSpec((tm, d), row), pl.BlockSpec((1, d), const),
                  pl.BlockSpec((1, N_SUB * 3, d), lambda i: (i // tiles_per_seq, 0, 0))]
        + [pl.BlockSpec((d, n), const, **res) for n in widths],
        out_specs=[pl.BlockSpec((tm, n), row) for n in widths],
        out_shape=[jax.ShapeDtypeStruct((t, n), BF16) for n in widths],
        scratch_shapes=[pltpu.VMEM((tm, d), BF16)],
        compiler_params=_cparams(("parallel",)),
        name="swa_qkv_proj",
    )(x2, g.reshape(1, d), mod, wq, wk, wv)


def _swa_kernel(q_ref, kc_ref, kp_ref, vc_ref, vp_ref, qn_ref, kn_ref, sink_ref, o_ref):
    T = SW_BLOCK
    wk = kc_ref.shape[2]
    first_key = jnp.where(pl.program_id(1) > 0, 0, T)

    def head_norm(x, g):
        w = x.shape[1]
        head_ones = jnp.where((_iota((w, w), 0) >> 6) == (_iota((w, w), 1) >> 6), 1.0, 0.0).astype(BF16)
        ms = _dot((x * x).astype(BF16), head_ones) * (1.0 / HEAD)
        return x * lax.rsqrt(ms + NORM_EPS) * g

    kcat = jnp.concatenate([kp_ref[0], kc_ref[0]], axis=0).astype(F32)
    kcat = head_norm(kcat, kn_ref[...]).astype(BF16)
    vcat = jnp.concatenate([vp_ref[0], vc_ref[0]], axis=0).astype(BF16)

    qi = _iota((T, 2 * T), 0)
    si = _iota((T, 2 * T), 1)
    rel = qi + T - si
    mask = (rel >= 0) & (rel < T) & (si >= first_key)
    shared = (kcat, vcat, mask, si == qi, head_norm)
    _run_interleaved([_swa_chain(kv, shared, q_ref, qn_ref, sink_ref, o_ref) for kv in range(wk // HEAD)])


def _swa_chain(kv, shared, q_ref, qn_ref, sink_ref, o_ref):
    T = SW_BLOCK
    kcat, vcat, mask, sink_slot, head_norm = shared
    wk = kcat.shape[1]
    gq = q_ref.shape[2] // wk
    wq = gq * HEAD
    scale = HEAD ** -0.5 * LOG2E
    lo, hi = kv * wq, (kv + 1) * wq

    rep = jnp.where(_iota((wk, wq), 0) == kv * HEAD + (_iota((wk, wq), 1) & (HEAD - 1)), 1.0, 0.0).astype(BF16)
    k_rep = _dot(kcat, rep).astype(BF16)
    v_rep = _dot(vcat, rep).astype(BF16)
    q = head_norm(q_ref[0, :, lo:hi].astype(F32), qn_ref[...]).astype(BF16)
    yield
    lane_head = _iota((T, wq), 1) >> 6
    scores = []
    for j in range(gq):
        q_j = jnp.where(lane_head == j, q, jnp.zeros((), BF16))
        scores.append(_dot_nt(q_j, k_rep) * scale)
        yield
    ones = jnp.ones((2 * T, wq), BF16)
    probs = []
    for j in range(gq):
        sink = sink_ref[0:1, kv * gq + j:kv * gq + j + 1] * LOG2E
        s = jnp.where(sink_slot, sink, jnp.where(mask, scores[j], NEG_INF))
        m = jnp.max(s, axis=-1, keepdims=True)
        yield
        probs.append(jnp.exp2(s - m).astype(BF16))
    o = jnp.zeros((T, wq), F32)
    for j in range(gq):
        denom = _dot(probs[j], ones)
        o_j = _dot(jnp.where(sink_slot, jnp.zeros((), BF16), probs[j]), v_rep)
        yield
        o = o + jnp.where(lane_head == j, o_j / denom, 0.0)
    o_ref[0, :, lo:hi] = o.astype(o_ref.dtype)


def _swa(q, k, v, q_norm, k_norm, sinks):
    bsz, seq, dq = q.shape
    dk = k.shape[2]
    nq = dq // HEAD
    cur = lambda b, i: (b, i, 0)
    prev = lambda b, i: (b, jnp.maximum(i - 1, 0), 0)
    const = lambda b, i: (0, 0)
    qn = jnp.tile(q_norm, dq // SW_KV_HEADS // HEAD).reshape(1, -1)
    kn = jnp.tile(k_norm, dk // HEAD).reshape(1, dk)
    return pl.pallas_call(
        _swa_kernel,
        grid=(bsz, seq // SW_BLOCK),
        in_specs=[pl.BlockSpec((1, SW_BLOCK, dq), cur),
                  pl.BlockSpec((1, SW_BLOCK, dk), cur), pl.BlockSpec((1, SW_BLOCK, dk), prev),
                  pl.BlockSpec((1, SW_BLOCK, dk), cur), pl.BlockSpec((1, SW_BLOCK, dk), prev),
                  pl.BlockSpec(qn.shape, const), pl.BlockSpec((1, dk), const),
                  pl.BlockSpec((1, nq), const)],
        out_specs=pl.BlockSpec((1, SW_BLOCK, dq), cur),
        out_shape=jax.ShapeDtypeStruct((bsz, seq, dq), BF16),
        compiler_params=_cparams(("parallel", "parallel")),
        name="swa_sink_attention",
    )(q, k, k, v, v, qn, kn, sinks.reshape(1, nq))


def kernel(x, c, ada_w, ada_b, norm_g, ffn_w_in, ffn_w_out, rw_mu, rw_w_rkv, rw_w_o, rw_w0, rw_w1, rw_w2, rw_a0, rw_a1, rw_a2, rw_g1, rw_g2, rw_k_k, rw_k_a, rw_r_k, rw_ln_w, rw_ln_b, rw_v0, rw_v1, rw_v2, mb_w_in, mb_conv_w, mb_conv_b, mb_dt_bias, mb_A_log, mb_D, mb_norm_g, mb_w_out, sw_w_qkv, sw_q_norm, sw_k_norm, sw_sinks, sw_w_o):
    bsz, seq, d = x.shape
    depth = ada_w.shape[0]
    t = bsz * seq
    mods = _modulation(c, ada_w, ada_b)
    x2 = x.reshape(t, d)
    v_first = None
    for i in range(depth):
        mod = mods[i]
        x2 = _ffn(x2, norm_g[i, 0], mod, ffn_w_in[i, 0].astype(BF16), ffn_w_out[i, 0].astype(BF16),
                  sub=0, seq=seq)
        kind, j = i % 3, i // 3
        if kind == 0:
            vres = None if v_first is None else (rw_v0[j - 1], rw_v1[j - 1], rw_v2[j - 1])
            r, k, v, ld, a, gt = _rwkv_pre(
                x2, norm_g[i, 1], mod, rw_mu[j], rw_w_rkv[j], rw_w0[j], rw_w1[j], rw_w2[j],
                rw_a0[j], rw_a1[j], rw_a2[j], rw_g1[j], rw_g2[j], vres, v_first, seq=seq)
            if v_first is None:
                v_first = v
            sh = (bsz, seq, d)
            z = _rwkv_scan(r.reshape(sh), k.reshape(sh), v.reshape(sh), ld.reshape(sh), a.reshape(sh),
                           gt.reshape(sh), rw_k_k[j], rw_k_a[j], rw_r_k[j], rw_ln_w[j], rw_ln_b[j])
            mixer = (z.reshape(t, d), rw_w_o[j].astype(BF16))
        elif kind == 1:
            d_inner = mb_w_out.shape[1]
            z, xs, bm, cm, dt = _mamba_in(x2, norm_g[i, 1], mod, mb_w_in[j], seq=seq, d_inner=d_inner)
            y = _ssd(xs.reshape(bsz, seq, -1), bm.reshape(bsz, seq, -1), cm.reshape(bsz, seq, -1),
                     z.reshape(bsz, seq, -1), dt.reshape(bsz, seq, -1), mb_conv_w[j], mb_conv_b[j],
                     mb_dt_bias[j], mb_A_log[j], mb_D[j], mb_norm_g[j])
            mixer = (y.reshape(t, d_inner), mb_w_out[j].astype(BF16))
        else:
            q, k, v = _swa_qkv(x2, norm_g[i, 1], mod, sw_w_qkv[j], seq=seq)
            o = _swa(q.reshape(bsz, seq, -1), k.reshape(bsz, seq, -1), v.reshape(bsz, seq, -1),
                     sw_q_norm[j], sw_k_norm[j], sw_sinks[j])
            mixer = (o.reshape(t, -1), sw_w_o[j].astype(BF16))
        x2 = _ffn(x2, norm_g[i, 2], mod, ffn_w_in[i, 1].astype(BF16), ffn_w_out[i, 1].astype(BF16),
                  sub=2, seq=seq, mixer=mixer)
    return x2.reshape(bsz, seq, d)
```

```python
import functools

import jax
import jax.numpy as jnp
from jax import lax
from jax.experimental import pallas as pl
from jax.experimental.pallas import tpu as pltpu

F32 = jnp.float32
BF16 = jnp.bfloat16

NORM_EPS = 1e-6
MACARON_W = 0.5
N_SUB = 3
HEAD = 64
RW_GN_EPS = 64e-5
RW_CHUNK = 64
RW_GROUP = 2
MB_GROUPS = 8
MB_STATE = 128
MB_CONV = 4
MB_CHUNK = 128
SW_KV_HEADS = 4
SW_BLOCK = 128
NEG_INF = -1e30
LOG2E = 1.4426950408889634
DECAY_SCALE = 0.6065306597126334
LANE = 128
VMEM_LIMIT = 56 * 1024 * 1024


def _cparams(sem):
    return pltpu.CompilerParams(dimension_semantics=sem, vmem_limit_bytes=VMEM_LIMIT)


def _dot(a, b):
    return jnp.dot(a, b, preferred_element_type=F32)


def _dot_nt(a, b):
    return lax.dot_general(a, b, (((1,), (1,)), ((), ())), preferred_element_type=F32)


def _dot_tn(a, b):
    return lax.dot_general(a, b, (((0,), (0,)), ((), ())), preferred_element_type=F32)


def _split3(x):
    hi = x.astype(BF16)
    r1 = x - hi.astype(F32)
    mid = r1.astype(BF16)
    lo = (r1 - mid.astype(F32)).astype(BF16)
    return hi, mid, lo


def _dot_exact_lhs(m, x):
    hi, mid, lo = _split3(x)
    return _dot(m, hi) + _dot(m, mid) + _dot(m, lo)


def _sigmoid(x):
    return 0.5 + 0.5 * jnp.tanh(0.5 * x)


def _silu(x):
    h = 0.5 * x
    return h + h * jnp.tanh(h)


def _softplus(x):
    return jnp.maximum(x, 0.0) + jnp.log(1.0 + jnp.exp(-jnp.abs(x)))


def _adaln(x, g, shift, scale):
    ms = jnp.mean(x * x, axis=-1, keepdims=True)
    return (x * lax.rsqrt(ms + NORM_EPS)) * (g * (1.0 + scale)) + shift


def _iota(shape, axis):
    return lax.broadcasted_iota(jnp.int32, shape, axis)


def _shift_rows(cur, tail, s):
    rolled = pltpu.roll(cur, s, 0)
    head = jnp.where(_iota(tail.shape, 0) < s, pltpu.roll(tail, s, 0), rolled[0:8])
    return jnp.concatenate([head, rolled[8:]], axis=0)


def _run_interleaved(chains, shared_rhs=None):
    done = object()
    reqs = [next(ch, done) for ch in chains]
    while chains:
        live = [(ch, rq) for ch, rq in zip(chains, reqs) if rq is not done]
        asked = [rq[0] for _, rq in live if rq is not None]
        answers = iter(())
        if asked:
            passes = max(rq[1] for _, rq in live if rq is not None)
            x = jnp.concatenate(asked, axis=0)
            hi_part = x.astype(BF16)
            prod = _dot(hi_part, shared_rhs)
            if passes == 2:
                prod = prod + _dot((x - hi_part.astype(F32)).astype(BF16), shared_rhs)
            n = asked[0].shape[0]
            answers = iter([prod[i * n:(i + 1) * n] for i in range(len(asked))])
        chains, reqs = [], []
        for ch, rq in live:
            try:
                reqs.append(ch.send(next(answers) if rq is not None else None))
                chains.append(ch)
            except StopIteration:
                pass


def _mod_kernel(c_ref, w_ref, b_ref, o_ref):
    ca = _silu(c_ref[...])
    a_hi, a_mid, a_lo = _split3(ca)
    w_hi, w_mid, w_lo = _split3(w_ref[0])
    acc = _dot(a_hi, w_hi) + (_dot(a_hi, w_mid) + _dot(a_mid, w_hi))
    acc = acc + (_dot(a_hi, w_lo) + _dot(a_mid, w_mid) + _dot(a_lo, w_hi))
    o_ref[0] = acc + b_ref[0]


def _modulation(c, ada_w, ada_b):
    depth, d, n = ada_w.shape
    bsz = c.shape[0]
    tn = 1152
    out = pl.pallas_call(
        _mod_kernel,
        grid=(depth, n // tn),
        in_specs=[
            pl.BlockSpec((bsz, d), lambda l, j: (0, 0)),
            pl.BlockSpec((1, d, tn), lambda l, j: (l, 0, j)),
            pl.BlockSpec((1, 1, tn), lambda l, j: (l, 0, j)),
        ],
        out_specs=pl.BlockSpec((1, bsz, tn), lambda l, j: (l, 0, j)),
        out_shape=jax.ShapeDtypeStruct((depth, bsz, n), F32),
        compiler_params=_cparams(("parallel", "parallel")),
        name="adaln_modulation",
    )(c, ada_w, ada_b.reshape(depth, 1, n))
    return out.reshape(depth, bsz, N_SUB * 3, d)


def _ffn_kernel(*refs, sub, fc, fused_mixer):
    if fused_mixer:
        x_ref, g_ref, mod_ref, win_ref, wout_ref, z_ref, wo_ref, o_ref, h_ref, acc_ref = refs
        res_ref = o_ref
    else:
        x_ref, g_ref, mod_ref, win_ref, wout_ref, o_ref, h_ref, acc_ref = refs
        res_ref = x_ref
    dff = wout_ref.shape[0]
    shift = mod_ref[0, 3 * sub:3 * sub + 1, :]
    scale = mod_ref[0, 3 * sub + 1:3 * sub + 2, :]
    gate = mod_ref[0, 3 * sub + 2:3 * sub + 3, :]

    if fused_mixer:
        o_ref[...] = x_ref[...] + mod_ref[0, 5:6, :] * _dot(z_ref[...], wo_ref[...])
    h_ref[...] = _adaln(res_ref[...], g_ref[...], shift, scale).astype(BF16)
    for c in range(dff // fc):
        h = h_ref[...]
        a = _dot(h, win_ref[:, c * fc:(c + 1) * fc])
        b = _dot(h, win_ref[:, dff + c * fc:dff + (c + 1) * fc])
        act = (_silu(a) * b).astype(BF16)
        part = _dot(act, wout_ref[c * fc:(c + 1) * fc, :])
        if c == 0:
            acc_ref[...] = part
        else:
            acc_ref[...] += part
    o_ref[...] = res_ref[...] + (MACARON_W * gate) * acc_ref[...]


def _ffn(x2, g, mod, w_in, w_out, *, sub, seq, mixer=None, tm=None, fc=256):
    t, d = x2.shape
    dff = w_out.shape[0]
    if tm is None:
        tm = 1024 if mixer is None else 512
    tm = min(tm, seq)
    tiles_per_seq = seq // tm
    resident = dict(pipeline_mode=pl.Buffered(1))
    args = [x2, g.reshape(1, d), mod, w_in, w_out]
    in_specs = [
        pl.BlockSpec((tm, d), lambda i: (i, 0)),
        pl.BlockSpec((1, d), lambda i: (0, 0)),
        pl.BlockSpec((1, N_SUB * 3, d), lambda i: (i // tiles_per_seq, 0, 0)),
        pl.BlockSpec((d, 2 * dff), lambda i: (0, 0), **resident),
        pl.BlockSpec((dff, d), lambda i: (0, 0), **resident),
    ]
    if mixer is not None:
        z2, w_o = mixer
        args += [z2, w_o]
        in_specs += [pl.BlockSpec((tm, z2.shape[1]), lambda i: (i, 0)),
                     pl.BlockSpec(w_o.shape, lambda i: (0, 0), **resident)]
    return pl.pallas_call(
        functools.partial(_ffn_kernel, sub=sub, fc=fc, fused_mixer=mixer is not None),
        grid=(t // tm,),
        in_specs=in_specs,
        out_specs=pl.BlockSpec((tm, d), lambda i: (i, 0)),
        out_shape=jax.ShapeDtypeStruct((t, d), F32),
        scratch_shapes=[pltpu.VMEM((tm, d), BF16), pltpu.VMEM((tm, d), F32)],
        compiler_params=_cparams(("parallel",)),
        name="macaron_ffn_mixer_out" if mixer is not None else "macaron_ffn",
    )(*args)


def _rwkv_pre_kernel(*refs, tiles_per_seq, has_vres):
    if has_vres:
        (x_ref, xp_ref, g_ref, mod_ref, mu_ref, wr_ref, wk_ref, wv_ref, w0_ref, w1_ref, w2_ref,
         a0_ref, a1_ref, a2_ref, g1_ref, g2_ref, v0_ref, v1_ref, v2_ref, vf_ref,
         r_ref, k_ref, v_ref, ld_ref, a_ref, gt_ref, h_ref, dx_ref) = refs
    else:
        (x_ref, xp_ref, g_ref, mod_ref, mu_ref, wr_ref, wk_ref, wv_ref, w0_ref, w1_ref, w2_ref,
         a0_ref, a1_ref, a2_ref, g1_ref, g2_ref,
         r_ref, k_ref, v_ref, ld_ref, a_ref, gt_ref, h_ref, dx_ref) = refs
    shift = mod_ref[0, 3:4, :]
    scale = mod_ref[0, 4:5, :]
    g = g_ref[...]
    h = _adaln(x_ref[...], g, shift, scale)
    first = (pl.program_id(0) % tiles_per_seq) == 0
    hp = _adaln(xp_ref[...], g, shift, scale)
    hp = jnp.where(first, 0.0, hp)
    h_ref[...] = h
    dx_ref[...] = _shift_rows(h, hp, 1) - h

    def chain(lo, hi, delay):
        for _ in range(delay):
            yield

        def mixed(j):
            return (h_ref[lo:hi, :] + dx_ref[lo:hi, :] * mu_ref[j:j + 1, :]).astype(BF16)

        w_mid = jnp.tanh(_dot(mixed(1), w1_ref[...])).astype(BF16)
        a_mid = _dot(mixed(4), a1_ref[...]).astype(BF16)
        yield
        wl = w0_ref[...] + _dot(w_mid, w2_ref[...])
        a_pre = a0_ref[...] + _dot(a_mid, a2_ref[...])
        g_mid = _sigmoid(_dot(mixed(5), g1_ref[...])).astype(BF16)
        yield
        r_ref[lo:hi, :] = _dot(mixed(0), wr_ref[...]).astype(r_ref.dtype)
        yield
        ld_ref[lo:hi, :] = (-DECAY_SCALE) * _sigmoid(wl)
        k_ref[lo:hi, :] = _dot(mixed(2), wk_ref[...]).astype(k_ref.dtype)
        yield
        a_ref[lo:hi, :] = _sigmoid(a_pre).astype(a_ref.dtype)
        gt_ref[lo:hi, :] = _dot(g_mid, g2_ref[...]).astype(gt_ref.dtype)
        xv = mixed(3)
        if has_vres:
            mixv = _sigmoid(v0_ref[...] + _dot(_dot(xv, v1_ref[...]).astype(BF16), v2_ref[...]))
        yield
        v = _dot(xv, wv_ref[...])
        if has_vres:
            v = v + (vf_ref[lo:hi, :].astype(F32) - v) * mixv
        v_ref[lo:hi, :] = v.astype(v_ref.dtype)

    tm = h_ref.shape[0]
    nsplit = 2 if tm % 32 == 0 else 1
    rows = tm // nsplit
    _run_interleaved([chain(i * rows, (i + 1) * rows, 2 * i) for i in range(nsplit)])


def _pad_cols(w, n):
    return jnp.pad(w, ((0, 0), (0, n - w.shape[1])))


def _pad_rows(w, n):
    return jnp.pad(w, ((0, n - w.shape[0]), (0, 0)))


def _round_up(n, m):
    return (n + m - 1) // m * m


def _lora_pair(w_a, w_b):
    n = _round_up(w_a.shape[1], LANE)
    return _pad_cols(w_a, n).astype(BF16), _pad_rows(w_b, n).astype(BF16)


def _rwkv_pre(x2, g, mod, mu, w_rkv, w0, w1, w2, a0, a1, a2, g1, g2, vres, v_first, *, seq, tm=512):
    t, d = x2.shape
    tm = min(tm, seq)
    tiles_per_seq = seq // tm
    has_vres = vres is not None
    const = lambda i: (0, 0)
    res = dict(pipeline_mode=pl.Buffered(1))
    row = lambda i: (i, 0)
    w1p, w2p = _lora_pair(w1, w2)
    a1p, a2p = _lora_pair(a1, a2)
    g1p, g2p = _lora_pair(g1, g2)
    args = [x2, x2, g.reshape(1, d), mod, _pad_rows(mu, 8),
            w_rkv[0].astype(BF16), w_rkv[1].astype(BF16), w_rkv[2].astype(BF16),
            w0.reshape(1, d), w1p, w2p, a0.reshape(1, d), a1p, a2p, g1p, g2p]
    in_specs = [
        pl.BlockSpec((tm, d), row),
        pl.BlockSpec((8, d), lambda i: (jnp.maximum(i * (tm // 8) - 1, 0), 0)),
        pl.BlockSpec((1, d), const),
        pl.BlockSpec((1, N_SUB * 3, d), lambda i: (i // tiles_per_seq, 0, 0)),
        pl.BlockSpec((8, d), const),
        pl.BlockSpec((d, d), const, **res), pl.BlockSpec((d, d), const, **res),
        pl.BlockSpec((d, d), const, **res),
        pl.BlockSpec((1, d), const),
        pl.BlockSpec(w1p.shape, const, **res), pl.BlockSpec(w2p.shape, const, **res),
        pl.BlockSpec((1, d), const),
        pl.BlockSpec(a1p.shape, const, **res), pl.BlockSpec(a2p.shape, const, **res),
        pl.BlockSpec(g1p.shape, const, **res), pl.BlockSpec(g2p.shape, const, **res),
    ]
    if has_vres:
        v0, v1, v2 = vres
        v1p, v2p = _lora_pair(v1, v2)
        args += [v0.reshape(1, d), v1p, v2p, v_first]
        in_specs += [pl.BlockSpec((1, d), const), pl.BlockSpec(v1p.shape, const, **res),
                     pl.BlockSpec(v2p.shape, const, **res), pl.BlockSpec((tm, d), row)]
    out_dtypes = (BF16, BF16, BF16, F32, BF16, BF16)
    return pl.pallas_call(
        functools.partial(_rwkv_pre_kernel, tiles_per_seq=tiles_per_seq, has_vres=has_vres),
        grid=(t // tm,),
        in_specs=in_specs,
        out_specs=[pl.BlockSpec((tm, d), row)] * 6,
        out_shape=[jax.ShapeDtypeStruct((t, d), dt_) for dt_ in out_dtypes],
        scratch_shapes=[pltpu.VMEM((tm, d), F32), pltpu.VMEM((tm, d), F32)],
        compiler_params=_cparams(("parallel",)),
        name="rwkv7_projections",
    )(*args)


def _rwkv_scan_kernel(*refs):
    st_ref = refs[-1]

    @pl.when(pl.program_id(2) == 0)
    def _():
        st_ref[...] = jnp.zeros_like(st_ref)

    nb = refs[0].shape[0]
    ng = st_ref.shape[0] // nb
    w = st_ref.shape[1]
    head_ones = jnp.where((_iota((w, w), 0) >> 6) == (_iota((w, w), 1) >> 6), 1.0, 0.0).astype(BF16)
    _run_interleaved([_rwkv_chain(bi, gi, *refs) for bi in range(nb) for gi in range(ng)],
                     shared_rhs=head_ones)


def _rwkv_chain(bi, gi, r_ref, k_ref, v_ref, ld_ref, a_ref, gt_ref, kk_ref, ka_ref, rk_ref,
                lnw_ref, lnb_ref, z_ref, st_ref):
    L = RW_CHUNK
    W = RW_GROUP * HEAD
    R = RW_GROUP * L
    lo, hi = gi * W, (gi + 1) * W
    si = bi * (st_ref.shape[0] // r_ref.shape[0]) + gi

    r = r_ref[bi, :, lo:hi].astype(F32)
    k = k_ref[bi, :, lo:hi].astype(F32)
    v = v_ref[bi, :, lo:hi].astype(F32)
    ld = ld_ref[bi, :, lo:hi]
    a = a_ref[bi, :, lo:hi].astype(F32)

    kkn = k * kk_ref[:, lo:hi]
    tril = jnp.where(_iota((L, L), 0) >= _iota((L, L), 1), 1.0, 0.0).astype(BF16)
    cum = _dot_exact_lhs(tril, ld)
    kk_ss = yield kkn * kkn, 1
    kk = kkn / jnp.maximum(jnp.sqrt(kk_ss), 1e-12)
    kmod = k * (1.0 + (a - 1.0) * ka_ref[:, lo:hi])
    c_end = cum[L - 1:L, :]
    e_pos = jnp.exp(cum)
    e_neg = jnp.exp(-cum)
    e_prev = jnp.exp(cum - ld)
    e_end = jnp.exp(c_end - cum)
    kka = kk * a

    rows = _iota((R, W), 0)
    cols = _iota((R, W), 1)
    same_head = (rows >> 6) == (cols >> 6)

    def bd(x):
        xb = x.astype(BF16)
        return jnp.where(same_head, jnp.concatenate([xb] * RW_GROUP, axis=0), jnp.zeros((), BF16))

    ar = jnp.concatenate([-kk * e_prev, r * e_pos], axis=0).astype(BF16)
    bk_s = jnp.concatenate([bd(kka * e_neg), bd(kmod * e_neg)], axis=0)
    v_s = bd(v)
    prod = _dot_nt(ar, bk_s)
    st = st_ref[si]
    from_state = _dot_nt(ar, st.astype(BF16))
    yield
    mt = _iota((L, R), 0)
    ms = _iota((L, R), 1) & (L - 1)
    strict = mt > ms
    incl = mt >= ms
    a_ab = jnp.where(strict, prod[0:L, 0:R], 0.0)
    a_ak = jnp.where(strict, prod[0:L, R:2 * R], 0.0)
    a_rb = jnp.where(incl, prod[L:2 * L, 0:R], 0.0)
    a_rk = jnp.where(incl, prod[L:2 * L, R:2 * R], 0.0)

    def mm(x, y):
        return _dot(x.astype(BF16), bd(y))

    eye = jnp.where(mt == ms, 1.0, 0.0)
    a_d = jnp.where((mt >> 3) == (ms >> 3), a_ab, 0.0)
    a_d2 = mm(a_d, a_d)
    rhs = from_state[0:L] + _dot(a_ak.astype(BF16), v_s)
    bonus = (yield r * kmod * rk_ref[:, lo:hi], 1) * v
    a_d4 = mm(a_d2, a_d2)
    tinv = mm(eye + a_d, eye + a_d2)
    yield
    tinv = mm(tinv, eye + a_d4)
    yield
    for sh in range(3, L.bit_length() - 1):
        off = ((mt >> (sh + 1)) == (ms >> (sh + 1))) & ((mt >> sh) != (ms >> sh))
        half = mm(tinv, jnp.where(off, a_ab, 0.0))
        yield
        tinv = tinv + mm(half, tinv)
        yield

    u = mm(tinv, rhs)
    yield
    y = from_state[L:2 * L] + _dot(
        jnp.concatenate([a_rb.astype(BF16), a_rk.astype(BF16)], axis=1),
        jnp.concatenate([bd(u), v_s], axis=0))
    new_terms = _dot_tn(jnp.concatenate([u, v], axis=0).astype(BF16),
                        jnp.concatenate([kka * e_end, kmod * e_end], axis=0).astype(BF16))
    st_ref[si] = st * jnp.exp(c_end) + jnp.where(same_head, new_terms, 0.0)
    mean = (yield y, 2) * (1.0 / HEAD)
    yc = y - mean
    var = (yield yc * yc, 1) * (1.0 / HEAD)
    yn = yc * lax.rsqrt(var + RW_GN_EPS) * lnw_ref[:, lo:hi] + lnb_ref[:, lo:hi]
    z_ref[bi, :, lo:hi] = ((yn + bonus) * gt_ref[bi, :, lo:hi].astype(F32)).astype(z_ref.dtype)


def _rwkv_scan(r, k, v, ld, a, gt, k_k, k_a, r_k, ln_w, ln_b, *, lanes=1024, nb=4):
    bsz, seq, d = r.shape
    assert RW_CHUNK == HEAD, "the side-by-side per-head matrix layout needs chunk length == head width"
    w = RW_GROUP * HEAD
    lanes = min(lanes, d)
    nb = min(nb, bsz)
    act = pl.BlockSpec((nb, RW_CHUNK, lanes), lambda b, g, c: (b, c, g))
    par = pl.BlockSpec((1, lanes), lambda b, g, c: (0, g))
    return pl.pallas_call(
        _rwkv_scan_kernel,
        grid=(bsz // nb, d // lanes, seq // RW_CHUNK),
        in_specs=[act] * 6 + [par] * 5,
        out_specs=act,
        out_shape=jax.ShapeDtypeStruct((bsz, seq, d), BF16),
        scratch_shapes=[pltpu.VMEM((nb * (lanes // w), w, w), F32)],
        compiler_params=_cparams(("parallel", "parallel", "arbitrary")),
        name="rwkv7_chunk_scan",
    )(r, k, v, ld, a, gt, k_k.reshape(1, d), k_a.reshape(1, d), r_k.reshape(1, d),
      ln_w.reshape(1, d), ln_b.reshape(1, d))


def _mamba_in_kernel(x_ref, g_ref, mod_ref, wz_ref, wx_ref, wb_ref, wc_ref, wdt_ref,
                     z_ref, xs_ref, b_ref, c_ref, dt_ref, h_ref):
    h_ref[...] = _adaln(x_ref[...], g_ref[...], mod_ref[0, 3:4, :], mod_ref[0, 4:5, :]).astype(BF16)
    z_ref[...] = _dot(h_ref[...], wz_ref[...]).astype(z_ref.dtype)
    xs_ref[...] = _dot(h_ref[...], wx_ref[...]).astype(xs_ref.dtype)
    b_ref[...] = _dot(h_ref[...], wb_ref[...]).astype(b_ref.dtype)
    c_ref[...] = _dot(h_ref[...], wc_ref[...]).astype(c_ref.dtype)
    dt_ref[...] = _dot(h_ref[...], wdt_ref[...])


def _mamba_in(x2, g, mod, w_in, *, seq, d_inner, tm=512):
    t, d = x2.shape
    gn = MB_GROUPS * MB_STATE
    tm = min(tm, seq)
    tiles_per_seq = seq // tm
    wz = w_in[:, :d_inner].astype(BF16)
    wx = w_in[:, d_inner:2 * d_inner].astype(BF16)
    wb = w_in[:, 2 * d_inner:2 * d_inner + gn].astype(BF16)
    wc = w_in[:, 2 * d_inner + gn:2 * d_inner + 2 * gn].astype(BF16)
    wdt = _pad_cols(w_in[:, 2 * d_inner + 2 * gn:], LANE).astype(BF16)
    const = lambda i: (0, 0)
    row = lambda i: (i, 0)
    res = dict(pipeline_mode=pl.Buffered(1))
    widths = (d_inner, d_inner, gn, gn, LANE)
    return pl.pallas_call(
        _mamba_in_kernel,
        grid=(t // tm,),
        in_specs=[pl.BlockSpec((tm, d), row), pl.BlockSpec((1, d), const),
                  pl.BlockSpec((1, N_SUB * 3, d), lambda i: (i // tiles_per_seq, 0, 0))]
        + [pl.BlockSpec((d, n), const, **res) for n in widths],
        out_specs=[pl.BlockSpec((tm, n), row) for n in widths],
        out_shape=[jax.ShapeDtypeStruct((t, n), dt_)
                   for n, dt_ in zip(widths, (BF16, BF16, BF16, BF16, F32))],
        scratch_shapes=[pltpu.VMEM((tm, d), BF16)],
        compiler_params=_cparams(("parallel",)),
        name="mamba2_in_proj",
    )(x2, g.reshape(1, d), mod, wz, wx, wb, wc, wdt)


def _ssd_kernel(xs_ref, b_ref, c_ref, z_ref, dt_ref, cwx_ref, cwb_ref, cwc_ref, cbx_ref, cbb_ref,
                cbc_ref, dtb_ref, alog_ref, dsk_ref, ng_ref, y_ref,
                st_ref, sx_ref, sb_ref, sc_ref):
    L = MB_CHUNK
    stages = ((sx_ref, xs_ref), (sb_ref, b_ref), (sc_ref, c_ref))

    @pl.when(pl.program_id(1) == 0)
    def _():
        st_ref[...] = jnp.zeros_like(st_ref)
        for s_ref, _ in stages:
            s_ref[0:8, :] = jnp.zeros((8, s_ref.shape[1]), F32)

    for s_ref, cur_ref in stages:
        s_ref[8:8 + L, :] = cur_ref[0].astype(F32)

    dt = _softplus(dt_ref[0] + dtb_ref[...])
    da = dt * (-jnp.exp(alog_ref[...]))
    tril = jnp.where(_iota((L, L), 0) >= _iota((L, L), 1), 1.0, 0.0).astype(BF16)
    acum = _dot_exact_lhs(tril, da) * LOG2E
    shared = (_split3(dt), _split3(acum))
    refs = (sx_ref, sb_ref, sc_ref, z_ref, cwx_ref, cwb_ref, cwc_ref, cbx_ref, cbb_ref, cbc_ref,
            dsk_ref, ng_ref, y_ref, st_ref)
    _run_interleaved([_ssd_chain(g, shared, *refs) for g in range(st_ref.shape[0])])

    for s_ref, _ in stages:
        s_ref[0:8, :] = s_ref[L:L + 8, :]


def _ssd_chain(g, shared, sx_ref, sb_ref, sc_ref, z_ref, cwx_ref, cwb_ref, cwc_ref, cbx_ref,
               cbb_ref, cbc_ref, dsk_ref, ng_ref, y_ref, st_ref):
    L = MB_CHUNK
    n = MB_STATE
    wx = st_ref.shape[2]
    hg = wx // HEAD
    dt3, ac3 = shared
    xl, xh = g * wx, (g + 1) * wx
    nl, nh = g * n, (g + 1) * n

    def conv_silu(s_ref, w_ref, bias_ref, lo, hi):
        staged = s_ref[:, lo:hi]
        prev = pltpu.roll(staged, 1, 0)
        w = [w_ref[j:j + 1, lo:hi] for j in range(MB_CONV)]
        older = pltpu.roll(staged * w[1] + prev * w[0], 2, 0)
        acc = bias_ref[:, lo:hi] + staged * w[3] + prev * w[2] + older
        return _silu(acc[8:8 + L])

    xs = conv_silu(sx_ref, cwx_ref, cbx_ref, xl, xh)
    bm = conv_silu(sb_ref, cwb_ref, cbb_ref, nl, nh).astype(BF16)
    cm = conv_silu(sc_ref, cwc_ref, cbc_ref, nl, nh).astype(BF16)

    head0 = hg * g
    sel_x = jnp.where(_iota((LANE, wx), 0) == head0 + (_iota((LANE, wx), 1) >> 6), 1.0, 0.0).astype(BF16)
    sel_r = jnp.where(_iota((8, LANE), 1) == head0 + _iota((8, LANE), 0), 1.0, 0.0).astype(BF16)
    sel_c = jnp.where(_iota((LANE, hg * L), 0) == head0 + (_iota((LANE, hg * L), 1) >> 7), 1.0, 0.0).astype(BF16)
    both = [jnp.concatenate([d_p, a_p], axis=0) for d_p, a_p in zip(dt3, ac3)]
    in_x = _dot(both[0], sel_x) + _dot(both[1], sel_x) + _dot(both[2], sel_x)
    dt_x, ac_x = in_x[0:L], in_x[L:2 * L]
    ac_row = _dot_nt(sel_r, ac3[0]) + _dot_nt(sel_r, ac3[1]) + _dot_nt(sel_r, ac3[2])
    ac_col = _dot(ac3[0], sel_c) + _dot(ac3[1], sel_c) + _dot(ac3[2], sel_c)
    cb = _dot_nt(cm, bm)
    st = st_ref[g]
    y_off = _dot(cm, st.astype(BF16))
    yield

    xdt = xs * dt_x
    causal = _iota((L, L), 0) >= _iota((L, L), 1)
    lane_head = _iota((L, wx), 1) >> 6
    xdt_b = xdt.astype(BF16)
    g_parts = []
    x_parts = []
    for j in range(hg):
        seg = jnp.where(causal, ac_col[:, j * L:(j + 1) * L] - ac_row[j:j + 1, :], -jnp.inf)
        g_parts.append((cb * jnp.exp2(seg)).astype(BF16))
        x_parts.append(jnp.where(lane_head == j, xdt_b, jnp.zeros((), BF16)))
    y_diag = _dot(jnp.concatenate(g_parts, axis=1), jnp.concatenate(x_parts, axis=0))
    ac_last = ac_x[L - 1:L, :]
    xdec = (xdt * jnp.exp2(ac_last - ac_x)).astype(BF16)
    st_ref[g] = st * jnp.exp2(ac_last) + _dot_tn(bm, xdec)
    yield

    y = y_diag + y_off * jnp.exp2(ac_x) + xs * dsk_ref[:, xl:xh]
    y = y * _silu(z_ref[0, :, xl:xh].astype(F32))
    ms = jnp.mean(y * y, axis=-1, keepdims=True)
    y_ref[0, :, xl:xh] = (y * lax.rsqrt(ms + NORM_EPS) * ng_ref[:, xl:xh]).astype(y_ref.dtype)


def _ssd(xs, bm, cm, z, dt, conv_w, conv_b, dt_bias, a_log, d_skip, norm_g):
    bsz, seq, d_inner = xs.shape
    n = MB_STATE
    gn = MB_GROUPS * n
    wx = d_inner // MB_GROUPS
    heads = d_inner // HEAD
    assert conv_w.shape[0] == MB_CONV == 4, "the in-kernel conv is written for 4 taps"
    cw = _pad_rows(conv_w, 8)
    cwx, cwb, cwc = cw[:, :d_inner], cw[:, d_inner:d_inner + gn], cw[:, d_inner + gn:]
    cb = conv_b.reshape(1, -1)
    cbx, cbb, cbc = cb[:, :d_inner], cb[:, d_inner:d_inner + gn], cb[:, d_inner + gn:]
    dtb = _pad_cols(dt_bias.reshape(1, heads), LANE)
    alog = _pad_cols(a_log.reshape(1, heads), LANE)
    dsk = jnp.repeat(d_skip, HEAD).reshape(1, d_inner)
    tok = lambda w: pl.BlockSpec((1, MB_CHUNK, w), lambda b, c: (b, c, 0))
    par = lambda r, w: pl.BlockSpec((r, w), lambda b, c: (0, 0))
    return pl.pallas_call(
        _ssd_kernel,
        grid=(bsz, seq // MB_CHUNK),
        in_specs=[tok(d_inner), tok(gn), tok(gn), tok(d_inner), tok(LANE),
                  par(8, d_inner), par(8, gn), par(8, gn), par(1, d_inner), par(1, gn), par(1, gn),
                  par(1, LANE), par(1, LANE), par(1, d_inner), par(1, d_inner)],
        out_specs=tok(d_inner),
        out_shape=jax.ShapeDtypeStruct((bsz, seq, d_inner), BF16),
        scratch_shapes=[pltpu.VMEM((MB_GROUPS, n, wx), F32),
                        pltpu.VMEM((8 + MB_CHUNK,d_inner), F32),
                        pltpu.VMEM((8 + MB_CHUNK,gn), F32),
                        pltpu.VMEM((8 + MB_CHUNK,gn), F32)],
        compiler_params=_cparams(("parallel", "arbitrary")),
        name="mamba2_conv_ssd",
    )(xs, bm, cm, z, dt, cwx, cwb, cwc, cbx, cbb, cbc, dtb, alog, dsk, norm_g.reshape(1, d_inner))


def _swa_qkv_kernel(x_ref, g_ref, mod_ref, wq_ref, wk_ref, wv_ref, q_ref, k_ref, v_ref, h_ref):
    h_ref[...] = _adaln(x_ref[...], g_ref[...], mod_ref[0, 3:4, :], mod_ref[0, 4:5, :]).astype(BF16)
    q_ref[...] = _dot(h_ref[...], wq_ref[...]).astype(q_ref.dtype)
    k_ref[...] = _dot(h_ref[...], wk_ref[...]).astype(k_ref.dtype)
    v_ref[...] = _dot(h_ref[...], wv_ref[...]).astype(v_ref.dtype)


def _swa_qkv(x2, g, mod, w_qkv, *, seq, tm=512):
    t, d = x2.shape
    nq = d
    nk = SW_KV_HEADS * HEAD
    tm = min(tm, seq)
    tiles_per_seq = seq // tm
    wq = w_qkv[:, :nq].astype(BF16)
    wk = w_qkv[:, nq:nq + nk].astype(BF16)
    wv = w_qkv[:, nq + nk:].astype(BF16)
    const = lambda i: (0, 0)
    row = lambda i: (i, 0)
    res = dict(pipeline_mode=pl.Buffered(1))
    widths = (nq, nk, nk)
    return pl.pallas_call(
        _swa_qkv_kernel,
        grid=(t // tm,),
        in_specs=[pl.BlockSpec((tm, d), row), pl.BlockSpec((1, d), const),
                  pl.BlockSpec((1, N_SUB * 3, d), lambda i: (i // tiles_per_seq, 0, 0))]
        + [pl.BlockSpec((d, n), const, **res) for n in widths],
        out_specs=[pl.BlockSpec((tm, n), row) for n in widths],
        out_shape=[jax.ShapeDtypeStruct((t, n), BF16) for n in widths],
        scratch_shapes=[pltpu.VMEM((tm, d), BF16)],
        compiler_params=_cparams(("parallel",)),
        name="swa_qkv_proj",
    )(x2, g.reshape(1, d), mod, wq, wk, wv)


def _swa_kernel(q_ref, kc_ref, kp_ref, vc_ref, vp_ref, qn_ref, kn_ref, sink_ref, o_ref):
    T = SW_BLOCK
    wk = kc_ref.shape[2]
    first_key = jnp.where(pl.program_id(1) > 0, 0, T)

    def head_norm(x, g):
        w = x.shape[1]
        head_ones = jnp.where((_iota((w, w), 0) >> 6) == (_iota((w, w), 1) >> 6), 1.0, 0.0).astype(BF16)
        ms = _dot((x * x).astype(BF16), head_ones) * (1.0 / HEAD)
        return x * lax.rsqrt(ms + NORM_EPS) * g

    kcat = jnp.concatenate([kp_ref[0], kc_ref[0]], axis=0).astype(F32)
    kcat = head_norm(kcat, kn_ref[...]).astype(BF16)
    vcat = jnp.concatenate([vp_ref[0], vc_ref[0]], axis=0).astype(BF16)

    qi = _iota((T, 2 * T), 0)
    si = _iota((T, 2 * T), 1)
    rel = qi + T - si
    mask = (rel >= 0) & (rel < T) & (si >= first_key)
    shared = (kcat, vcat, mask, si == qi, head_norm)
    _run_interleaved([_swa_chain(kv, shared, q_ref, qn_ref, sink_ref, o_ref) for kv in range(wk // HEAD)])


def _swa_chain(kv, shared, q_ref, qn_ref, sink_ref, o_ref):
    T = SW_BLOCK
    kcat, vcat, mask, sink_slot, head_norm = shared
    wk = kcat.shape[1]
    gq = q_ref.shape[2] // wk
    wq = gq * HEAD
    scale = HEAD ** -0.5 * LOG2E
    lo, hi = kv * wq, (kv + 1) * wq

    rep = jnp.where(_iota((wk, wq), 0) == kv * HEAD + (_iota((wk, wq), 1) & (HEAD - 1)), 1.0, 0.0).astype(BF16)
    k_rep = _dot(kcat, rep).astype(BF16)
    v_rep = _dot(vcat, rep).astype(BF16)
    q = head_norm(q_ref[0, :, lo:hi].astype(F32), qn_ref[...]).astype(BF16)
    yield
    lane_head = _iota((T, wq), 1) >> 6
    scores = []
    for j in range(gq):
        q_j = jnp.where(lane_head == j, q, jnp.zeros((), BF16))
        scores.append(_dot_nt(q_j, k_rep) * scale)
        yield
    ones = jnp.ones((2 * T, wq), BF16)
    probs = []
    for j in range(gq):
        sink = sink_ref[0:1, kv * gq + j:kv * gq + j + 1] * LOG2E
        s = jnp.where(sink_slot, sink, jnp.where(mask, scores[j], NEG_INF))
        m = jnp.max(s, axis=-1, keepdims=True)
        yield
        probs.append(jnp.exp2(s - m).astype(BF16))
    o = jnp.zeros((T, wq), F32)
    for j in range(gq):
        denom = _dot(probs[j], ones)
        o_j = _dot(jnp.where(sink_slot, jnp.zeros((), BF16), probs[j]), v_rep)
        yield
        o = o + jnp.where(lane_head == j, o_j / denom, 0.0)
    o_ref[0, :, lo:hi] = o.astype(o_ref.dtype)


def _swa(q, k, v, q_norm, k_norm, sinks):
    bsz, seq, dq = q.shape
    dk = k.shape[2]
    nq = dq // HEAD
    cur = lambda b, i: (b, i, 0)
    prev = lambda b, i: (b, jnp.maximum(i - 1, 0), 0)
    const = lambda b, i: (0, 0)
    qn = jnp.tile(q_norm, dq // SW_KV_HEADS // HEAD).reshape(1, -1)
    kn = jnp.tile(k_norm, dk // HEAD).reshape(1, dk)
    return pl.pallas_call(
        _swa_kernel,
        grid=(bsz, seq // SW_BLOCK),
        in_specs=[pl.BlockSpec((1, SW_BLOCK, dq), cur),
                  pl.BlockSpec((1, SW_BLOCK, dk), cur), pl.BlockSpec((1, SW_BLOCK, dk), prev),
                  pl.BlockSpec((1, SW_BLOCK, dk), cur), pl.BlockSpec((1, SW_BLOCK, dk), prev),
                  pl.BlockSpec(qn.shape, const), pl.BlockSpec((1, dk), const),
                  pl.BlockSpec((1, nq), const)],
        out_specs=pl.BlockSpec((1, SW_BLOCK, dq), cur),
        out_shape=jax.ShapeDtypeStruct((bsz, seq, dq), BF16),
        compiler_params=_cparams(("parallel", "parallel")),
        name="swa_sink_attention",
    )(q, k, k, v, v, qn, kn, sinks.reshape(1, nq))


def kernel(x, c, ada_w, ada_b, norm_g, ffn_w_in, ffn_w_out, rw_mu, rw_w_rkv, rw_w_o, rw_w0, rw_w1, rw_w2, rw_a0, rw_a1, rw_a2, rw_g1, rw_g2, rw_k_k, rw_k_a, rw_r_k, rw_ln_w, rw_ln_b, rw_v0, rw_v1, rw_v2, mb_w_in, mb_conv_w, mb_conv_b, mb_dt_bias, mb_A_log, mb_D, mb_norm_g, mb_w_out, sw_w_qkv, sw_q_norm, sw_k_norm, sw_sinks, sw_w_o):
    bsz, seq, d = x.shape
    depth = ada_w.shape[0]
    t = bsz * seq
    mods = _modulation(c, ada_w, ada_b)
    x2 = x.reshape(t, d)
    v_first = None
    for i in range(depth):
        mod = mods[i]
        x2 = _ffn(x2, norm_g[i, 0], mod, ffn_w_in[i, 0].astype(BF16), ffn_w_out[i, 0].astype(BF16),
                  sub=0, seq=seq)
        kind, j = i % 3, i // 3
        if kind == 0:
            vres = None if v_first is None else (rw_v0[j - 1], rw_v1[j - 1], rw_v2[j - 1])
            r, k, v, ld, a, gt = _rwkv_pre(
                x2, norm_g[i, 1], mod, rw_mu[j], rw_w_rkv[j], rw_w0[j], rw_w1[j], rw_w2[j],
                rw_a0[j], rw_a1[j], rw_a2[j], rw_g1[j], rw_g2[j], vres, v_first, seq=seq)
            if v_first is None:
                v_first = v
            sh = (bsz, seq, d)
            z = _rwkv_scan(r.reshape(sh), k.reshape(sh), v.reshape(sh), ld.reshape(sh), a.reshape(sh),
                           gt.reshape(sh), rw_k_k[j], rw_k_a[j], rw_r_k[j], rw_ln_w[j], rw_ln_b[j])
            mixer = (z.reshape(t, d), rw_w_o[j].astype(BF16))
        elif kind == 1:
            d_inner = mb_w_out.shape[1]
            z, xs, bm, cm, dt = _mamba_in(x2, norm_g[i, 1], mod, mb_w_in[j], seq=seq, d_inner=d_inner)
            y = _ssd(xs.reshape(bsz, seq, -1), bm.reshape(bsz, seq, -1), cm.reshape(bsz, seq, -1),
                     z.reshape(bsz, seq, -1), dt.reshape(bsz, seq, -1), mb_conv_w[j], mb_conv_b[j],
                     mb_dt_bias[j], mb_A_log[j], mb_D[j], mb_norm_g[j])
            mixer = (y.reshape(t, d_inner), mb_w_out[j].astype(BF16))
        else:
            q, k, v = _swa_qkv(x2, norm_g[i, 1], mod, sw_w_qkv[j], seq=seq)
            o = _swa(q.reshape(bsz, seq, -1), k.reshape(bsz, seq, -1), v.reshape(bsz, seq, -1),
                     sw_q_norm[j], sw_k_norm[j], sw_sinks[j])
            mixer = (o.reshape(t, -1), sw_w_o[j].astype(BF16))
        x2 = _ffn(x2, norm_g[i, 2], mod, ffn_w_in[i, 1].astype(BF16), ffn_w_out[i, 1].astype(BF16),
                  sub=2, seq=seq, mixer=mixer)
    return x2.reshape(bsz, seq, d)
```

```python
import functools

import jax
import jax.numpy as jnp
from jax import lax
from jax.experimental import pallas as pl
from jax.experimental.pallas import tpu as pltpu

F32 = jnp.float32
BF16 = jnp.bfloat16

NORM_EPS = 1e-6
MACARON_W = 0.5
N_SUB = 3
HEAD = 64
RW_GN_EPS = 64e-5
RW_CHUNK = 64
RW_GROUP = 2
MB_GROUPS = 8
MB_STATE = 128
MB_CONV = 4
MB_CHUNK = 128
SW_KV_HEADS = 4
SW_BLOCK = 128
NEG_INF = -1e30
LOG2E = 1.4426950408889634
DECAY_SCALE = 0.6065306597126334
LANE = 128
VMEM_LIMIT = 56 * 1024 * 1024


def _cparams(sem):
    return pltpu.CompilerParams(dimension_semantics=sem, vmem_limit_bytes=VMEM_LIMIT)


def _dot(a, b):
    return jnp.dot(a, b, preferred_element_type=F32)


def _dot_nt(a, b):
    return lax.dot_general(a, b, (((1,), (1,)), ((), ())), preferred_element_type=F32)


def _dot_tn(a, b):
    return lax.dot_general(a, b, (((0,), (0,)), ((), ())), preferred_element_type=F32)


def _split3(x):
    hi = x.astype(BF16)
    r1 = x - hi.astype(F32)
    mid = r1.astype(BF16)
    lo = (r1 - mid.astype(F32)).astype(BF16)
    return hi, mid, lo


def _dot_exact_lhs(m, x):
    hi, mid, lo = _split3(x)
    return _dot(m, hi) + _dot(m, mid) + _dot(m, lo)


def _sigmoid(x):
    return 0.5 + 0.5 * jnp.tanh(0.5 * x)


def _silu(x):
    h = 0.5 * x
    return h + h * jnp.tanh(h)


def _softplus(x):
    return jnp.maximum(x, 0.0) + jnp.log(1.0 + jnp.exp(-jnp.abs(x)))


def _adaln(x, g, shift, scale):
    ms = jnp.mean(x * x, axis=-1, keepdims=True)
    return (x * lax.rsqrt(ms + NORM_EPS)) * (g * (1.0 + scale)) + shift


def _iota(shape, axis):
    return lax.broadcasted_iota(jnp.int32, shape, axis)


def _shift_rows(cur, tail, s):
    rolled = pltpu.roll(cur, s, 0)
    head = jnp.where(_iota(tail.shape, 0) < s, pltpu.roll(tail, s, 0), rolled[0:8])
    return jnp.concatenate([head, rolled[8:]], axis=0)


def _run_interleaved(chains, shared_rhs=None):
    done = object()
    reqs = [next(ch, done) for ch in chains]
    while chains:
        live = [(ch, rq) for ch, rq in zip(chains, reqs) if rq is not done]
        asked = [rq[0] for _, rq in live if rq is not None]
        answers = iter(())
        if asked:
            passes = max(rq[1] for _, rq in live if rq is not None)
            x = jnp.concatenate(asked, axis=0)
            hi_part = x.astype(BF16)
            prod = _dot(hi_part, shared_rhs)
            if passes == 2:
                prod = prod + _dot((x - hi_part.astype(F32)).astype(BF16), shared_rhs)
            n = asked[0].shape[0]
            answers = iter([prod[i * n:(i + 1) * n] for i in range(len(asked))])
        chains, reqs = [], []
        for ch, rq in live:
            try:
                reqs.append(ch.send(next(answers) if rq is not None else None))
                chains.append(ch)
            except StopIteration:
                pass


def _mod_kernel(c_ref, w_ref, b_ref, o_ref):
    ca = _silu(c_ref[...])
    a_hi, a_mid, a_lo = _split3(ca)
    w_hi, w_mid, w_lo = _split3(w_ref[0])
    acc = _dot(a_hi, w_hi) + (_dot(a_hi, w_mid) + _dot(a_mid, w_hi))
    acc = acc + (_dot(a_hi, w_lo) + _dot(a_mid, w_mid) + _dot(a_lo, w_hi))
    o_ref[0] = acc + b_ref[0]


def _modulation(c, ada_w, ada_b):
    depth, d, n = ada_w.shape
    bsz = c.shape[0]
    tn = 1152
    out = pl.pallas_call(
        _mod_kernel,
        grid=(depth, n // tn),
        in_specs=[
            pl.BlockSpec((bsz, d), lambda l, j: (0, 0)),
            pl.BlockSpec((1, d, tn), lambda l, j: (l, 0, j)),
            pl.BlockSpec((1, 1, tn), lambda l, j: (l, 0, j)),
        ],
        out_specs=pl.BlockSpec((1, bsz, tn), lambda l, j: (l, 0, j)),
        out_shape=jax.ShapeDtypeStruct((depth, bsz, n), F32),
        compiler_params=_cparams(("parallel", "parallel")),
        name="adaln_modulation",
    )(c, ada_w, ada_b.reshape(depth, 1, n))
    return out.reshape(depth, bsz, N_SUB * 3, d)


def _ffn_kernel(*refs, sub, fc, fused_mixer):
    if fused_mixer:
        x_ref, g_ref, mod_ref, win_ref, wout_ref, z_ref, wo_ref, o_ref, h_ref, acc_ref = refs
        res_ref = o_ref
    else:
        x_ref, g_ref, mod_ref, win_ref, wout_ref, o_ref, h_ref, acc_ref = refs
        res_ref = x_ref
    dff = wout_ref.shape[0]
    shift = mod_ref[0, 3 * sub:3 * sub + 1, :]
    scale = mod_ref[0, 3 * sub + 1:3 * sub + 2, :]
    gate = mod_ref[0, 3 * sub + 2:3 * sub + 3, :]

    if fused_mixer:
        o_ref[...] = x_ref[...] + mod_ref[0, 5:6, :] * _dot(z_ref[...], wo_ref[...])
    h_ref[...] = _adaln(res_ref[...], g_ref[...], shift, scale).astype(BF16)
    for c in range(dff // fc):
        h = h_ref[...]
        a = _dot(h, win_ref[:, c * fc:(c + 1) * fc])
        b = _dot(h, win_ref[:, dff + c * fc:dff + (c + 1) * fc])
        act = (_silu(a) * b).astype(BF16)
        part = _dot(act, wout_ref[c * fc:(c + 1) * fc, :])
        if c == 0:
            acc_ref[...] = part
        else:
            acc_ref[...] += part
    o_ref[...] = res_ref[...] + (MACARON_W * gate) * acc_ref[...]


def _ffn(x2, g, mod, w_in, w_out, *, sub, seq, mixer=None, tm=None, fc=256):
    t, d = x2.shape
    dff = w_out.shape[0]
    if tm is None:
        tm = 1024
    tm = min(tm, seq)
    tiles_per_seq = seq // tm
    resident = dict(pipeline_mode=pl.Buffered(1))
    args = [x2, g.reshape(1, d), mod, w_in, w_out]
    in_specs = [
        pl.BlockSpec((tm, d), lambda i: (i, 0)),
        pl.BlockSpec((1, d), lambda i: (0, 0)),
        pl.BlockSpec((1, N_SUB * 3, d), lambda i: (i // tiles_per_seq, 0, 0)),
        pl.BlockSpec((d, 2 * dff), lambda i: (0, 0), **resident),
        pl.BlockSpec((dff, d), lambda i: (0, 0), **resident),
    ]
    if mixer is not None:
        z2, w_o = mixer
        args += [z2, w_o]
        in_specs += [pl.BlockSpec((tm, z2.shape[1]), lambda i: (i, 0)),
                     pl.BlockSpec(w_o.shape, lambda i: (0, 0), **resident)]
    return pl.pallas_call(
        functools.partial(_ffn_kernel, sub=sub, fc=fc, fused_mixer=mixer is not None),
        grid=(t // tm,),
        in_specs=in_specs,
        out_specs=pl.BlockSpec((tm, d), lambda i: (i, 0)),
        out_shape=jax.ShapeDtypeStruct((t, d), F32),
        scratch_shapes=[pltpu.VMEM((tm, d), BF16), pltpu.VMEM((tm, d), F32)],
        compiler_params=_cparams(("parallel",)),
        name="macaron_ffn_mixer_out" if mixer is not None else "macaron_ffn",
    )(*args)


def _rwkv_pre_kernel(*refs, tiles_per_seq, has_vres):
    if has_vres:
        (x_ref, xp_ref, g_ref, mod_ref, mu_ref, wr_ref, wk_ref, wv_ref, w0_ref, w1_ref, w2_ref,
         a0_ref, a1_ref, a2_ref, g1_ref, g2_ref, v0_ref, v1_ref, v2_ref, vf_ref,
         r_ref, k_ref, v_ref, ld_ref, a_ref, gt_ref, h_ref, dx_ref) = refs
    else:
        (x_ref, xp_ref, g_ref, mod_ref, mu_ref, wr_ref, wk_ref, wv_ref, w0_ref, w1_ref, w2_ref,
         a0_ref, a1_ref, a2_ref, g1_ref, g2_ref,
         r_ref, k_ref, v_ref, ld_ref, a_ref, gt_ref, h_ref, dx_ref) = refs
    shift = mod_ref[0, 3:4, :]
    scale = mod_ref[0, 4:5, :]
    g = g_ref[...]
    h = _adaln(x_ref[...], g, shift, scale)
    first = (pl.program_id(0) % tiles_per_seq) == 0
    hp = _adaln(xp_ref[...], g, shift, scale)
    hp = jnp.where(first, 0.0, hp)
    h_ref[...] = h
    dx_ref[...] = _shift_rows(h, hp, 1) - h

    def chain(lo, hi, delay):
        for _ in range(delay):
            yield

        def mixed(j):
            return (h_ref[lo:hi, :] + dx_ref[lo:hi, :] * mu_ref[j:j + 1, :]).astype(BF16)

        w_mid = jnp.tanh(_dot(mixed(1), w1_ref[...])).astype(BF16)
        a_mid = _dot(mixed(4), a1_ref[...]).astype(BF16)
        yield
        wl = w0_ref[...] + _dot(w_mid, w2_ref[...])
        a_pre = a0_ref[...] + _dot(a_mid, a2_ref[...])
        g_mid = _sigmoid(_dot(mixed(5), g1_ref[...])).astype(BF16)
        yield
        r_ref[lo:hi, :] = _dot(mixed(0), wr_ref[...]).astype(r_ref.dtype)
        yield
        ld_ref[lo:hi, :] = (-DECAY_SCALE) * _sigmoid(wl)
        k_ref[lo:hi, :] = _dot(mixed(2), wk_ref[...]).astype(k_ref.dtype)
        yield
        a_ref[lo:hi, :] = _sigmoid(a_pre).astype(a_ref.dtype)
        gt_ref[lo:hi, :] = _dot(g_mid, g2_ref[...]).astype(gt_ref.dtype)
        xv = mixed(3)
        if has_vres:
            mixv = _sigmoid(v0_ref[...] + _dot(_dot(xv, v1_ref[...]).astype(BF16), v2_ref[...]))
        yield
        v = _dot(xv, wv_ref[...])
        if has_vres:
            v = v + (vf_ref[lo:hi, :].astype(F32) - v) * mixv
        v_ref[lo:hi, :] = v.astype(v_ref.dtype)

    tm = h_ref.shape[0]
    nsplit = 2 if tm % 32 == 0 else 1
    rows = tm // nsplit
    _run_interleaved([chain(i * rows, (i + 1) * rows, 2 * i) for i in range(nsplit)])


def _pad_cols(w, n):
    return jnp.pad(w, ((0, 0), (0, n - w.shape[1])))


def _pad_rows(w, n):
    return jnp.pad(w, ((0, n - w.shape[0]), (0, 0)))


def _round_up(n, m):
    return (n + m - 1) // m * m


def _lora_pair(w_a, w_b):
    n = _round_up(w_a.shape[1], LANE)
    return _pad_cols(w_a, n).astype(BF16), _pad_rows(w_b, n).astype(BF16)


def _rwkv_pre(x2, g, mod, mu, w_rkv, w0, w1, w2, a0, a1, a2, g1, g2, vres, v_first, *, seq, tm=512):
    t, d = x2.shape
    tm = min(tm, seq)
    tiles_per_seq = seq // tm
    has_vres = vres is not None
    const = lambda i: (0, 0)
    res = dict(pipeline_mode=pl.Buffered(1))
    row = lambda i: (i, 0)
    w1p, w2p = _lora_pair(w1, w2)
    a1p, a2p = _lora_pair(a1, a2)
    g1p, g2p = _lora_pair(g1, g2)
    args = [x2, x2, g.reshape(1, d), mod, _pad_rows(mu, 8),
            w_rkv[0].astype(BF16), w_rkv[1].astype(BF16), w_rkv[2].astype(BF16),
            w0.reshape(1, d), w1p, w2p, a0.reshape(1, d), a1p, a2p, g1p, g2p]
    in_specs = [
        pl.BlockSpec((tm, d), row),
        pl.BlockSpec((8, d), lambda i: (jnp.maximum(i * (tm // 8) - 1, 0), 0)),
        pl.BlockSpec((1, d), const),
        pl.BlockSpec((1, N_SUB * 3, d), lambda i: (i // tiles_per_seq, 0, 0)),
        pl.BlockSpec((8, d), const),
        pl.BlockSpec((d, d), const, **res), pl.BlockSpec((d, d), const, **res),
        pl.BlockSpec((d, d), const, **res),
        pl.BlockSpec((1, d), const),
        pl.BlockSpec(w1p.shape, const, **res), pl.BlockSpec(w2p.shape, const, **res),
        pl.BlockSpec((1, d), const),
        pl.BlockSpec(a1p.shape, const, **res), pl.BlockSpec(a2p.shape, const, **res),
        pl.BlockSpec(g1p.shape, const, **res), pl.BlockSpec(g2p.shape, const, **res),
    ]
    if has_vres:
        v0, v1, v2 = vres
        v1p, v2p = _lora_pair(v1, v2)
        args += [v0.reshape(1, d), v1p, v2p, v_first]
        in_specs += [pl.BlockSpec((1, d), const), pl.BlockSpec(v1p.shape, const, **res),
                     pl.BlockSpec(v2p.shape, const, **res), pl.BlockSpec((tm, d), row)]
    out_dtypes = (BF16, BF16, BF16, F32, BF16, BF16)
    return pl.pallas_call(
        functools.partial(_rwkv_pre_kernel, tiles_per_seq=tiles_per_seq, has_vres=has_vres),
        grid=(t // tm,),
        in_specs=in_specs,
        out_specs=[pl.BlockSpec((tm, d), row)] * 6,
        out_shape=[jax.ShapeDtypeStruct((t, d), dt_) for dt_ in out_dtypes],
        scratch_shapes=[pltpu.VMEM((tm, d), F32), pltpu.VMEM((tm, d), F32)],
        compiler_params=_cparams(("parallel",)),
        name="rwkv7_projections",
    )(*args)


def _rwkv_scan_kernel(*refs):
    st_ref = refs[-1]

    @pl.when(pl.program_id(2) == 0)
    def _():
        st_ref[...] = jnp.zeros_like(st_ref)

    nb = refs[0].shape[0]
    ng = st_ref.shape[0] // nb
    w = st_ref.shape[1]
    head_ones = jnp.where((_iota((w, w), 0) >> 6) == (_iota((w, w), 1) >> 6), 1.0, 0.0).astype(BF16)
    _run_interleaved([_rwkv_chain(bi, gi, *refs) for bi in range(nb) for gi in range(ng)],
                     shared_rhs=head_ones)


def _rwkv_chain(bi, gi, r_ref, k_ref, v_ref, ld_ref, a_ref, gt_ref, kk_ref, ka_ref, rk_ref,
                lnw_ref, lnb_ref, z_ref, st_ref):
    L = RW_CHUNK
    W = RW_GROUP * HEAD
    R = RW_GROUP * L
    lo, hi = gi * W, (gi + 1) * W
    si = bi * (st_ref.shape[0] // r_ref.shape[0]) + gi

    r = r_ref[bi, :, lo:hi].astype(F32)
    k = k_ref[bi, :, lo:hi].astype(F32)
    v = v_ref[bi, :, lo:hi].astype(F32)
    ld = ld_ref[bi, :, lo:hi]
    a = a_ref[bi, :, lo:hi].astype(F32)

    kkn = k * kk_ref[:, lo:hi]
    tril = jnp.where(_iota((L, L), 0) >= _iota((L, L), 1), 1.0, 0.0).astype(BF16)
    cum = _dot_exact_lhs(tril, ld)
    kk_ss = yield kkn * kkn, 1
    kk = kkn / jnp.maximum(jnp.sqrt(kk_ss), 1e-12)
    kmod = k * (1.0 + (a - 1.0) * ka_ref[:, lo:hi])
    c_end = cum[L - 1:L, :]
    e_pos = jnp.exp(cum)
    e_neg = jnp.exp(-cum)
    e_prev = jnp.exp(cum - ld)
    e_end = jnp.exp(c_end - cum)
    kka = kk * a

    rows = _iota((R, W), 0)
    cols = _iota((R, W), 1)
    same_head = (rows >> 6) == (cols >> 6)

    def bd(x):
        xb = x.astype(BF16)
        return jnp.where(same_head, jnp.concatenate([xb] * RW_GROUP, axis=0), jnp.zeros((), BF16))

    ar = jnp.concatenate([-kk * e_prev, r * e_pos], axis=0).astype(BF16)
    bk_s = jnp.concatenate([bd(kka * e_neg), bd(kmod * e_neg)], axis=0)
    v_s = bd(v)
    prod = _dot_nt(ar, bk_s)
    st = st_ref[si]
    from_state = _dot_nt(ar, st.astype(BF16))
    yield
    mt = _iota((L, R), 0)
    ms = _iota((L, R), 1) & (L - 1)
    strict = mt > ms
    incl = mt >= ms
    a_ab = jnp.where(strict, prod[0:L, 0:R], 0.0)
    a_ak = jnp.where(strict, prod[0:L, R:2 * R], 0.0)
    a_rb = jnp.where(incl, prod[L:2 * L, 0:R], 0.0)
    a_rk = jnp.where(incl, prod[L:2 * L, R:2 * R], 0.0)

    def mm(x, y):
        return _dot(x.astype(BF16), bd(y))

    eye = jnp.where(mt == ms, 1.0, 0.0)
    a_d = jnp.where((mt >> 3) == (ms >> 3), a_ab, 0.0)
    a_d2 = mm(a_d, a_d)
    rhs = from_state[0:L] + _dot(a_ak.astype(BF16), v_s)
    bonus = (yield r * kmod * rk_ref[:, lo:hi], 1) * v
    a_d4 = mm(a_d2, a_d2)
    tinv = mm(eye + a_d, eye + a_d2)
    yield
    tinv = mm(tinv, eye + a_d4)
    yield
    for sh in range(3, L.bit_length() - 1):
        off = ((mt >> (sh + 1)) == (ms >> (sh + 1))) & ((mt >> sh) != (ms >> sh))
        half = mm(tinv, jnp.where(off, a_ab, 0.0))
        yield
        tinv = tinv + mm(half, tinv)
        yield

    u = mm(tinv, rhs)
    yield
    y = from_state[L:2 * L] + _dot(
        jnp.concatenate([a_rb.astype(BF16), a_rk.astype(BF16)], axis=1),
        jnp.concatenate([bd(u), v_s], axis=0))
    new_terms = _dot_tn(jnp.concatenate([u, v], axis=0).astype(BF16),
                        jnp.concatenate([kka * e_end, kmod * e_end], axis=0).astype(BF16))
    st_ref[si] = st * jnp.exp(c_end) + jnp.where(same_head, new_terms, 0.0)
    mean = (yield y, 2) * (1.0 / HEAD)
    yc = y - mean
    var = (yield yc * yc, 1) * (1.0 / HEAD)
    yn = yc * lax.rsqrt(var + RW_GN_EPS) * lnw_ref[:, lo:hi] + lnb_ref[:, lo:hi]
    z_ref[bi, :, lo:hi] = ((yn + bonus) * gt_ref[bi, :, lo:hi].astype(F32)).astype(z_ref.dtype)


def _rwkv_scan(r, k, v, ld, a, gt, k_k, k_a, r_k, ln_w, ln_b, *, lanes=1024, nb=4):
    bsz, seq, d = r.shape
    assert RW_CHUNK == HEAD, "the side-by-side per-head matrix layout needs chunk length == head width"
    w = RW_GROUP * HEAD
    lanes = min(lanes, d)
    nb = min(nb, bsz)
    act = pl.BlockSpec((nb, RW_CHUNK, lanes), lambda b, g, c: (b, c, g))
    par = pl.BlockSpec((1, lanes), lambda b, g, c: (0, g))
    return pl.pallas_call(
        _rwkv_scan_kernel,
        grid=(bsz // nb, d // lanes, seq // RW_CHUNK),
        in_specs=[act] * 6 + [par] * 5,
        out_specs=act,
        out_shape=jax.ShapeDtypeStruct((bsz, seq, d), BF16),
        scratch_shapes=[pltpu.VMEM((nb * (lanes // w), w, w), F32)],
        compiler_params=_cparams(("parallel", "parallel", "arbitrary")),
        name="rwkv7_chunk_scan",
    )(r, k, v, ld, a, gt, k_k.reshape(1, d), k_a.reshape(1, d), r_k.reshape(1, d),
      ln_w.reshape(1, d), ln_b.reshape(1, d))


def _mamba_in_kernel(x_ref, g_ref, mod_ref, wz_ref, wx_ref, wb_ref, wc_ref, wdt_ref,
                     z_ref, xs_ref, b_ref, c_ref, dt_ref, h_ref):
    h_ref[...] = _adaln(x_ref[...], g_ref[...], mod_ref[0, 3:4, :], mod_ref[0, 4:5, :]).astype(BF16)
    z_ref[...] = _dot(h_ref[...], wz_ref[...]).astype(z_ref.dtype)
    xs_ref[...] = _dot(h_ref[...], wx_ref[...]).astype(xs_ref.dtype)
    b_ref[...] = _dot(h_ref[...], wb_ref[...]).astype(b_ref.dtype)
    c_ref[...] = _dot(h_ref[...], wc_ref[...]).astype(c_ref.dtype)
    dt_ref[...] = _dot(h_ref[...], wdt_ref[...])


def _mamba_in(x2, g, mod, w_in, *, seq, d_inner, tm=512):
    t, d = x2.shape
    gn = MB_GROUPS * MB_STATE
    tm = min(tm, seq)
    tiles_per_seq = seq // tm
    wz = w_in[:, :d_inner].astype(BF16)
    wx = w_in[:, d_inner:2 * d_inner].astype(BF16)
    wb = w_in[:, 2 * d_inner:2 * d_inner + gn].astype(BF16)
    wc = w_in[:, 2 * d_inner + gn:2 * d_inner + 2 * gn].astype(BF16)
    wdt = _pad_cols(w_in[:, 2 * d_inner + 2 * gn:], LANE).astype(BF16)
    const = lambda i: (0, 0)
    row = lambda i: (i, 0)
    res = dict(pipeline_mode=pl.Buffered(1))
    widths = (d_inner, d_inner, gn, gn, LANE)
    return pl.pallas_call(
        _mamba_in_kernel,
        grid=(t // tm,),
        in_specs=[pl.BlockSpec((tm, d), row), pl.BlockSpec((1, d), const),
                  pl.BlockSpec((1, N_SUB * 3, d), lambda i: (i // tiles_per_seq, 0, 0))]
        + [pl.BlockSpec((d, n), const, **res) for n in widths],
        out_specs=[pl.BlockSpec((tm, n), row) for n in widths],
        out_shape=[jax.ShapeDtypeStruct((t, n), dt_)
                   for n, dt_ in zip(widths, (BF16, BF16, BF16, BF16, F32))],
        scratch_shapes=[pltpu.VMEM((tm, d), BF16)],
        compiler_params=_cparams(("parallel",)),
        name="mamba2_in_proj",
    )(x2, g.reshape(1, d), mod, wz, wx, wb, wc, wdt)


def _ssd_kernel(xs_ref, b_ref, c_ref, z_ref, dt_ref, cwx_ref, cwb_ref, cwc_ref, cbx_ref, cbb_ref,
                cbc_ref, dtb_ref, alog_ref, dsk_ref, ng_ref, y_ref,
                st_ref, sx_ref, sb_ref, sc_ref):
    L = MB_CHUNK
    stages = ((sx_ref, xs_ref), (sb_ref, b_ref), (sc_ref, c_ref))

    @pl.when(pl.program_id(1) == 0)
    def _():
        st_ref[...] = jnp.zeros_like(st_ref)
        for s_ref, _ in stages:
            s_ref[0:8, :] = jnp.zeros((8, s_ref.shape[1]), F32)

    for s_ref, cur_ref in stages:
        s_ref[8:8 + L, :] = cur_ref[0].astype(F32)

    dt = _softplus(dt_ref[0] + dtb_ref[...])
    da = dt * (-jnp.exp(alog_ref[...]))
    tril = jnp.where(_iota((L, L), 0) >= _iota((L, L), 1), 1.0, 0.0).astype(BF16)
    acum = _dot_exact_lhs(tril, da) * LOG2E
    shared = (_split3(dt), _split3(acum))
    refs = (sx_ref, sb_ref, sc_ref, z_ref, cwx_ref, cwb_ref, cwc_ref, cbx_ref, cbb_ref, cbc_ref,
            dsk_ref, ng_ref, y_ref, st_ref)
    _run_interleaved([_ssd_chain(g, shared, *refs) for g in range(st_ref.shape[0])])

    for s_ref, _ in stages:
        s_ref[0:8, :] = s_ref[L:L + 8, :]


def _ssd_chain(g, shared, sx_ref, sb_ref, sc_ref, z_ref, cwx_ref, cwb_ref, cwc_ref, cbx_ref,
               cbb_ref, cbc_ref, dsk_ref, ng_ref, y_ref, st_ref):
    L = MB_CHUNK
    n = MB_STATE
    wx = st_ref.shape[2]
    hg = wx // HEAD
    dt3, ac3 = shared
    xl, xh = g * wx, (g + 1) * wx
    nl, nh = g * n, (g + 1) * n

    def conv_silu(s_ref, w_ref, bias_ref, lo, hi):
        staged = s_ref[:, lo:hi]
        prev = pltpu.roll(staged, 1, 0)
        w = [w_ref[j:j + 1, lo:hi] for j in range(MB_CONV)]
        older = pltpu.roll(staged * w[1] + prev * w[0], 2, 0)
        acc = bias_ref[:, lo:hi] + staged * w[3] + prev * w[2] + older
        return _silu(acc[8:8 + L])

    xs = conv_silu(sx_ref, cwx_ref, cbx_ref, xl, xh)
    bm = conv_silu(sb_ref, cwb_ref, cbb_ref, nl, nh).astype(BF16)
    cm = conv_silu(sc_ref, cwc_ref, cbc_ref, nl, nh).astype(BF16)

    head0 = hg * g
    sel_x = jnp.where(_iota((LANE, wx), 0) == head0 + (_iota((LANE, wx), 1) >> 6), 1.0, 0.0).astype(BF16)
    sel_r = jnp.where(_iota((8, LANE), 1) == head0 + _iota((8, LANE), 0), 1.0, 0.0).astype(BF16)
    sel_c = jnp.where(_iota((LANE, hg * L), 0) == head0 + (_iota((LANE, hg * L), 1) >> 7), 1.0, 0.0).astype(BF16)
    both = [jnp.concatenate([d_p, a_p], axis=0) for d_p, a_p in zip(dt3, ac3)]
    in_x = _dot(both[0], sel_x) + _dot(both[1], sel_x) + _dot(both[2], sel_x)
    dt_x, ac_x = in_x[0:L], in_x[L:2 * L]
    ac_row = _dot_nt(sel_r, ac3[0]) + _dot_nt(sel_r, ac3[1]) + _dot_nt(sel_r, ac3[2])
    ac_col = _dot(ac3[0], sel_c) + _dot(ac3[1], sel_c) + _dot(ac3[2], sel_c)
    cb = _dot_nt(cm, bm)
    st = st_ref[g]
    y_off = _dot(cm, st.astype(BF16))
    yield

    xdt = xs * dt_x
    causal = _iota((L, L), 0) >= _iota((L, L), 1)
    lane_head = _iota((L, wx), 1) >> 6
    xdt_b = xdt.astype(BF16)
    g_parts = []
    x_parts = []
    for j in range(hg):
        seg = jnp.where(causal, ac_col[:, j * L:(j + 1) * L] - ac_row[j:j + 1, :], -jnp.inf)
        g_parts.append((cb * jnp.exp2(seg)).astype(BF16))
        x_parts.append(jnp.where(lane_head == j, xdt_b, jnp.zeros((), BF16)))
    y_diag = _dot(jnp.concatenate(g_parts, axis=1), jnp.concatenate(x_parts, axis=0))
    ac_last = ac_x[L - 1:L, :]
    xdec = (xdt * jnp.exp2(ac_last - ac_x)).astype(BF16)
    st_ref[g] = st * jnp.exp2(ac_last) + _dot_tn(bm, xdec)
    yield

    y = y_diag + y_off * jnp.exp2(ac_x) + xs * dsk_ref[:, xl:xh]
    y = y * _silu(z_ref[0, :, xl:xh].astype(F32))
    ms = jnp.mean(y * y, axis=-1, keepdims=True)
    y_ref[0, :, xl:xh] = (y * lax.rsqrt(ms + NORM_EPS) * ng_ref[:, xl:xh]).astype(y_ref.dtype)


def _ssd(xs, bm, cm, z, dt, conv_w, conv_b, dt_bias, a_log, d_skip, norm_g):
    bsz, seq, d_inner = xs.shape
    n = MB_STATE
    gn = MB_GROUPS * n
    wx = d_inner // MB_GROUPS
    heads = d_inner // HEAD
    assert conv_w.shape[0] == MB_CONV == 4, "the in-kernel conv is written for 4 taps"
    cw = _pad_rows(conv_w, 8)
    cwx, cwb, cwc = cw[:, :d_inner], cw[:, d_inner:d_inner + gn], cw[:, d_inner + gn:]
    cb = conv_b.reshape(1, -1)
    cbx, cbb, cbc = cb[:, :d_inner], cb[:, d_inner:d_inner + gn], cb[:, d_inner + gn:]
    dtb = _pad_cols(dt_bias.reshape(1, heads), LANE)
    alog = _pad_cols(a_log.reshape(1, heads), LANE)
    dsk = jnp.repeat(d_skip, HEAD).reshape(1, d_inner)
    tok = lambda w: pl.BlockSpec((1, MB_CHUNK, w), lambda b, c: (b, c, 0))
    par = lambda r, w: pl.BlockSpec((r, w), lambda b, c: (0, 0))
    return pl.pallas_call(
        _ssd_kernel,
        grid=(bsz, seq // MB_CHUNK),
        in_specs=[tok(d_inner), tok(gn), tok(gn), tok(d_inner), tok(LANE),
                  par(8, d_inner), par(8, gn), par(8, gn), par(1, d_inner), par(1, gn), par(1, gn),
                  par(1, LANE), par(1, LANE), par(1, d_inner), par(1, d_inner)],
        out_specs=tok(d_inner),
        out_shape=jax.ShapeDtypeStruct((bsz, seq, d_inner), BF16),
        scratch_shapes=[pltpu.VMEM((MB_GROUPS, n, wx), F32),
                        pltpu.VMEM((8 + MB_CHUNK,d_inner), F32),
                        pltpu.VMEM((8 + MB_CHUNK,gn), F32),
                        pltpu.VMEM((8 + MB_CHUNK,gn), F32)],
        compiler_params=_cparams(("parallel", "arbitrary")),
        name="mamba2_conv_ssd",
    )(xs, bm, cm, z, dt, cwx, cwb, cwc, cbx, cbb, cbc, dtb, alog, dsk, norm_g.reshape(1, d_inner))


def _swa_qkv_kernel(x_ref, g_ref, mod_ref, wq_ref, wk_ref, wv_ref, q_ref, k_ref, v_ref, h_ref):
    h_ref[...] = _adaln(x_ref[...], g_ref[...], mod_ref[0, 3:4, :], mod_ref[0, 4:5, :]).astype(BF16)
    q_ref[...] = _dot(h_ref[...], wq_ref[...]).astype(q_ref.dtype)
    k_ref[...] = _dot(h_ref[...], wk_ref[...]).astype(k_ref.dtype)
    v_ref[...] = _dot(h_ref[...], wv_ref[...]).astype(v_ref.dtype)


def _swa_qkv(x2, g, mod, w_qkv, *, seq, tm=512):
    t, d = x2.shape
    nq = d
    nk = SW_KV_HEADS * HEAD
    tm = min(tm, seq)
    tiles_per_seq = seq // tm
    wq = w_qkv[:, :nq].astype(BF16)
    wk = w_qkv[:, nq:nq + nk].astype(BF16)
    wv = w_qkv[:, nq + nk:].astype(BF16)
    const = lambda i: (0, 0)
    row = lambda i: (i, 0)
    res = dict(pipeline_mode=pl.Buffered(1))
    widths = (nq, nk, nk)
    return pl.pallas_call(
        _swa_qkv_kernel,
        grid=(t // tm,),
        in_specs=[pl.BlockSpec((tm, d), row), pl.BlockSpec((1, d), const),
                  pl.BlockSpec((1, N_SUB * 3, d), lambda i: (i // tiles_per_seq, 0, 0))]
        + [pl.BlockSpec((d, n), const, **res) for n in widths],
        out_specs=[pl.BlockSpec((tm, n), row) for n in widths],
        out_shape=[jax.ShapeDtypeStruct((t, n), BF16) for n in widths],
        scratch_shapes=[pltpu.VMEM((tm, d), BF16)],
        compiler_params=_cparams(("parallel",)),
        name="swa_qkv_proj",
    )(x2, g.reshape(1, d), mod, wq, wk, wv)


def _swa_kernel(q_ref, kc_ref, kp_ref, vc_ref, vp_ref, qn_ref, kn_ref, sink_ref, o_ref):
    T = SW_BLOCK
    wk = kc_ref.shape[2]
    first_key = jnp.where(pl.program_id(1) > 0, 0, T)

    def head_norm(x, g):
        w = x.shape[1]
        head_ones = jnp.where((_iota((w, w), 0) >> 6) == (_iota((w, w), 1) >> 6), 1.0, 0.0).astype(BF16)
        ms = _dot((x * x).astype(BF16), head_ones) * (1.0 / HEAD)
        return x * lax.rsqrt(ms + NORM_EPS) * g

    k_all = jnp.concatenate([kp_ref[0], kc_ref[0]], axis=0).astype(F32)
    k_all = head_norm(k_all, kn_ref[...]).astype(BF16)
    v_all = jnp.concatenate([vp_ref[0], vc_ref[0]], axis=0).astype(BF16)

    qi = _iota((T, 2 * T), 0)
    si = _iota((T, 2 * T), 1)
    rel = qi + T - si
    band = (rel >= 0) & (rel < T)
    chains = []
    for blk in range(q_ref.shape[1] // T):
        mask = band & (si >= first_key) if blk == 0 else band
        shared = (k_all[blk * T:(blk + 2) * T], v_all[blk * T:(blk + 2) * T], mask, si == qi, head_norm)
        chains += [_swa_chain(blk, kv, shared, q_ref, qn_ref, sink_ref, o_ref) for kv in range(wk // HEAD)]
    _run_interleaved(chains)


def _swa_chain(blk, kv, shared, q_ref, qn_ref, sink_ref, o_ref):
    T = SW_BLOCK
    kcat, vcat, mask, sink_slot, head_norm = shared
    r0, r1 = blk * T, (blk + 1) * T
    wk = kcat.shape[1]
    gq = q_ref.shape[2] // wk
    wq = gq * HEAD
    scale = HEAD ** -0.5 * LOG2E
    lo, hi = kv * wq, (kv + 1) * wq

    rep = jnp.where(_iota((wk, wq), 0) == kv * HEAD + (_iota((wk, wq), 1) & (HEAD - 1)), 1.0, 0.0).astype(BF16)
    k_rep = _dot(kcat, rep).astype(BF16)
    v_rep = _dot(vcat, rep).astype(BF16)
    q = head_norm(q_ref[0, r0:r1, lo:hi].astype(F32), qn_ref[...]).astype(BF16)
    yield
    lane_head = _iota((T, wq), 1) >> 6
    scores = []
    for j in range(gq):
        q_j = jnp.where(lane_head == j, q, jnp.zeros((), BF16))
        scores.append(_dot_nt(q_j, k_rep) * scale)
        yield
    ones = jnp.ones((2 * T, wq), BF16)
    probs = []
    for j in range(gq):
        sink = sink_ref[0:1, kv * gq + j:kv * gq + j + 1] * LOG2E
        s = jnp.where(sink_slot, sink, jnp.where(mask, scores[j], NEG_INF))
        m = jnp.max(s, axis=-1, keepdims=True)
        yield
        probs.append(jnp.exp2(s - m).astype(BF16))
    o = jnp.zeros((T, wq), F32)
    for j in range(gq):
        denom = _dot(probs[j], ones)
        o_j = _dot(jnp.where(sink_slot, jnp.zeros((), BF16), probs[j]), v_rep)
        yield
        o = o + jnp.where(lane_head == j, o_j / denom, 0.0)
    o_ref[0, r0:r1, lo:hi] = o.astype(o_ref.dtype)


def _swa(q, k, v, q_norm, k_norm, sinks, *, nblk=2):
    bsz, seq, dq = q.shape
    dk = k.shape[2]
    nq = dq // HEAD
    rows = nblk * SW_BLOCK
    cur = lambda b, i: (b, i, 0)
    prev = lambda b, i: (b, jnp.maximum(i * nblk - 1, 0), 0)
    const = lambda b, i: (0, 0)
    qn = jnp.tile(q_norm, dq // SW_KV_HEADS // HEAD).reshape(1, -1)
    kn = jnp.tile(k_norm, dk // HEAD).reshape(1, dk)
    return pl.pallas_call(
        _swa_kernel,
        grid=(bsz, seq // rows),
        in_specs=[pl.BlockSpec((1, rows, dq), cur),
                  pl.BlockSpec((1, rows, dk), cur), pl.BlockSpec((1, SW_BLOCK, dk), prev),
                  pl.BlockSpec((1, rows, dk), cur), pl.BlockSpec((1, SW_BLOCK, dk), prev),
                  pl.BlockSpec(qn.shape, const), pl.BlockSpec((1, dk), const),
                  pl.BlockSpec((1, nq), const)],
        out_specs=pl.BlockSpec((1, rows, dq), cur),
        out_shape=jax.ShapeDtypeStruct((bsz, seq, dq), BF16),
        compiler_params=_cparams(("parallel", "parallel")),
        name="swa_sink_attention",
    )(q, k, k, v, v, qn, kn, sinks.reshape(1, nq))


def kernel(x, c, ada_w, ada_b, norm_g, ffn_w_in, ffn_w_out, rw_mu, rw_w_rkv, rw_w_o, rw_w0, rw_w1, rw_w2, rw_a0, rw_a1, rw_a2, rw_g1, rw_g2, rw_k_k, rw_k_a, rw_r_k, rw_ln_w, rw_ln_b, rw_v0, rw_v1, rw_v2, mb_w_in, mb_conv_w, mb_conv_b, mb_dt_bias, mb_A_log, mb_D, mb_norm_g, mb_w_out, sw_w_qkv, sw_q_norm, sw_k_norm, sw_sinks, sw_w_o):
    bsz, seq, d = x.shape
    depth = ada_w.shape[0]
    t = bsz * seq
    mods = _modulation(c, ada_w, ada_b)
    x2 = x.reshape(t, d)
    v_first = None
    for i in range(depth):
        mod = mods[i]
        x2 = _ffn(x2, norm_g[i, 0], mod, ffn_w_in[i, 0].astype(BF16), ffn_w_out[i, 0].astype(BF16),
                  sub=0, seq=seq)
        kind, j = i % 3, i // 3
        if kind == 0:
            vres = None if v_first is None else (rw_v0[j - 1], rw_v1[j - 1], rw_v2[j - 1])
            r, k, v, ld, a, gt = _rwkv_pre(
                x2, norm_g[i, 1], mod, rw_mu[j], rw_w_rkv[j], rw_w0[j], rw_w1[j], rw_w2[j],
                rw_a0[j], rw_a1[j], rw_a2[j], rw_g1[j], rw_g2[j], vres, v_first, seq=seq)
            if v_first is None:
                v_first = v
            sh = (bsz, seq, d)
            z = _rwkv_scan(r.reshape(sh), k.reshape(sh), v.reshape(sh), ld.reshape(sh), a.reshape(sh),
                           gt.reshape(sh), rw_k_k[j], rw_k_a[j], rw_r_k[j], rw_ln_w[j], rw_ln_b[j])
            mixer = (z.reshape(t, d), rw_w_o[j].astype(BF16))
        elif kind == 1:
            d_inner = mb_w_out.shape[1]
            z, xs, bm, cm, dt = _mamba_in(x2, norm_g[i, 1], mod, mb_w_in[j], seq=seq, d_inner=d_inner)
            y = _ssd(xs.reshape(bsz, seq, -1), bm.reshape(bsz, seq, -1), cm.reshape(bsz, seq, -1),
                     z.reshape(bsz, seq, -1), dt.reshape(bsz, seq, -1), mb_conv_w[j], mb_conv_b[j],
                     mb_dt_bias[j], mb_A_log[j], mb_D[j], mb_norm_g[j])
            mixer = (y.reshape(t, d_inner), mb_w_out[j].astype(BF16))
        else:
            q, k, v = _swa_qkv(x2, norm_g[i, 1], mod, sw_w_qkv[j], seq=seq)
            o = _swa(q.reshape(bsz, seq, -1), k.reshape(bsz, seq, -1), v.reshape(bsz, seq, -1),
                     sw_q_norm[j], sw_k_norm[j], sw_sinks[j])
            mixer = (o.reshape(t, -1), sw_w_o[j].astype(BF16))
        x2 = _ffn(x2, norm_g[i, 2], mod, ffn_w_in[i, 1].astype(BF16), ffn_w_out[i, 1].astype(BF16),
                  sub=2, seq=seq, mixer=mixer)
    return x2.reshape(bsz, seq, d)
```

```python
import functools

import jax
import jax.numpy as jnp
from jax import lax
from jax.experimental import pallas as pl
from jax.experimental.pallas import tpu as pltpu

F32 = jnp.float32
BF16 = jnp.bfloat16

NORM_EPS = 1e-6
MACARON_W = 0.5
N_SUB = 3
HEAD = 64
RW_GN_EPS = 64e-5
RW_CHUNK = 64
RW_GROUP = 2
MB_GROUPS = 8
MB_STATE = 128
MB_CONV = 4
MB_CHUNK = 128
SW_KV_HEADS = 4
SW_BLOCK = 128
NEG_INF = -1e30
LOG2E = 1.4426950408889634
DECAY_SCALE = 0.6065306597126334
LANE = 128
VMEM_LIMIT = 56 * 1024 * 1024


def _cparams(sem):
    return pltpu.CompilerParams(dimension_semantics=sem, vmem_limit_bytes=VMEM_LIMIT)


def _dot(a, b):
    return jnp.dot(a, b, preferred_element_type=F32)


def _dot_nt(a, b):
    return lax.dot_general(a, b, (((1,), (1,)), ((), ())), preferred_element_type=F32)


def _dot_tn(a, b):
    return lax.dot_general(a, b, (((0,), (0,)), ((), ())), preferred_element_type=F32)


def _split3(x):
    hi = x.astype(BF16)
    r1 = x - hi.astype(F32)
    mid = r1.astype(BF16)
    lo = (r1 - mid.astype(F32)).astype(BF16)
    return hi, mid, lo


def _dot_exact_lhs(m, x):
    hi, mid, lo = _split3(x)
    return _dot(m, hi) + _dot(m, mid) + _dot(m, lo)


def _sigmoid(x):
    return 0.5 + 0.5 * jnp.tanh(0.5 * x)


def _silu(x):
    h = 0.5 * x
    return h + h * jnp.tanh(h)


def _softplus(x):
    return jnp.maximum(x, 0.0) + jnp.log(1.0 + jnp.exp(-jnp.abs(x)))


def _adaln(x, g, shift, scale):
    ms = jnp.mean(x * x, axis=-1, keepdims=True)
    return (x * lax.rsqrt(ms + NORM_EPS)) * (g * (1.0 + scale)) + shift


def _iota(shape, axis):
    return lax.broadcasted_iota(jnp.int32, shape, axis)


def _shift_rows(cur, tail, s):
    rolled = pltpu.roll(cur, s, 0)
    head = jnp.where(_iota(tail.shape, 0) < s, pltpu.roll(tail, s, 0), rolled[0:8])
    return jnp.concatenate([head, rolled[8:]], axis=0)


def _run_interleaved(chains, shared_rhs=None):
    done = object()
    reqs = [next(ch, done) for ch in chains]
    while chains:
        live = [(ch, rq) for ch, rq in zip(chains, reqs) if rq is not done]
        asked = [rq[0] for _, rq in live if rq is not None]
        answers = iter(())
        if asked:
            passes = max(rq[1] for _, rq in live if rq is not None)
            x = jnp.concatenate(asked, axis=0)
            hi_part = x.astype(BF16)
            prod = _dot(hi_part, shared_rhs)
            if passes == 2:
                prod = prod + _dot((x - hi_part.astype(F32)).astype(BF16), shared_rhs)
            n = asked[0].shape[0]
            answers = iter([prod[i * n:(i + 1) * n] for i in range(len(asked))])
        chains, reqs = [], []
        for ch, rq in live:
            try:
                reqs.append(ch.send(next(answers) if rq is not None else None))
                chains.append(ch)
            except StopIteration:
                pass


def _mod_kernel(c_ref, w_ref, b_ref, o_ref):
    ca = _silu(c_ref[...])
    a_hi, a_mid, a_lo = _split3(ca)
    w_hi, w_mid, w_lo = _split3(w_ref[0])
    acc = _dot(a_hi, w_hi) + (_dot(a_hi, w_mid) + _dot(a_mid, w_hi))
    acc = acc + (_dot(a_hi, w_lo) + _dot(a_mid, w_mid) + _dot(a_lo, w_hi))
    o_ref[0] = acc + b_ref[0]


def _modulation(c, ada_w, ada_b):
    depth, d, n = ada_w.shape
    bsz = c.shape[0]
    tn = 1152
    out = pl.pallas_call(
        _mod_kernel,
        grid=(depth, n // tn),
        in_specs=[
            pl.BlockSpec((bsz, d), lambda l, j: (0, 0)),
            pl.BlockSpec((1, d, tn), lambda l, j: (l, 0, j)),
            pl.BlockSpec((1, 1, tn), lambda l, j: (l, 0, j)),
        ],
        out_specs=pl.BlockSpec((1, bsz, tn), lambda l, j: (l, 0, j)),
        out_shape=jax.ShapeDtypeStruct((depth, bsz, n), F32),
        compiler_params=_cparams(("parallel", "parallel")),
        name="adaln_modulation",
    )(c, ada_w, ada_b.reshape(depth, 1, n))
    return out.reshape(depth, bsz, N_SUB * 3, d)


def _ffn_kernel(*refs, sub, fc, fused_mixer):
    if fused_mixer:
        x_ref, g_ref, mod_ref, win_ref, wout_ref, z_ref, wo_ref, o_ref, h_ref, acc_ref = refs
        res_ref = o_ref
    else:
        x_ref, g_ref, mod_ref, win_ref, wout_ref, o_ref, h_ref, acc_ref = refs
        res_ref = x_ref
    dff = wout_ref.shape[0]
    shift = mod_ref[0, 3 * sub:3 * sub + 1, :]
    scale = mod_ref[0, 3 * sub + 1:3 * sub + 2, :]
    gate = mod_ref[0, 3 * sub + 2:3 * sub + 3, :]

    if fused_mixer:
        o_ref[...] = x_ref[...] + mod_ref[0, 5:6, :] * _dot(z_ref[...], wo_ref[...])
    h_ref[...] = _adaln(res_ref[...], g_ref[...], shift, scale).astype(BF16)
    for c in range(dff // fc):
        h = h_ref[...]
        a = _dot(h, win_ref[:, c * fc:(c + 1) * fc])
        b = _dot(h, win_ref[:, dff + c * fc:dff + (c + 1) * fc])
        act = (_silu(a) * b).astype(BF16)
        part = _dot(act, wout_ref[c * fc:(c + 1) * fc, :])
        if c == 0:
            acc_ref[...] = part
        else:
            acc_ref[...] += part
    o_ref[...] = res_ref[...] + (MACARON_W * gate) * acc_ref[...]


def _ffn(x2, g, mod, w_in, w_out, *, sub, seq, mixer=None, tm=None, fc=256):
    t, d = x2.shape
    dff = w_out.shape[0]
    if tm is None:
        tm = 1024
    tm = min(tm, seq)
    tiles_per_seq = seq // tm
    resident = dict(pipeline_mode=pl.Buffered(1))
    args = [x2, g.reshape(1, d), mod, w_in, w_out]
    in_specs = [
        pl.BlockSpec((tm, d), lambda i: (i, 0)),
        pl.BlockSpec((1, d), lambda i: (0, 0)),
        pl.BlockSpec((1, N_SUB * 3, d), lambda i: (i // tiles_per_seq, 0, 0)),
        pl.BlockSpec((d, 2 * dff), lambda i: (0, 0), **resident),
        pl.BlockSpec((dff, d), lambda i: (0, 0), **resident),
    ]
    if mixer is not None:
        z2, w_o = mixer
        args += [z2, w_o]
        in_specs += [pl.BlockSpec((tm, z2.shape[1]), lambda i: (i, 0)),
                     pl.BlockSpec(w_o.shape, lambda i: (0, 0), **resident)]
    return pl.pallas_call(
        functools.partial(_ffn_kernel, sub=sub, fc=fc, fused_mixer=mixer is not None),
        grid=(t // tm,),
        in_specs=in_specs,
        out_specs=pl.BlockSpec((tm, d), lambda i: (i, 0)),
        out_shape=jax.ShapeDtypeStruct((t, d), F32),
        scratch_shapes=[pltpu.VMEM((tm, d), BF16), pltpu.VMEM((tm, d), F32)],
        compiler_params=_cparams(("parallel",)),
        name="macaron_ffn_mixer_out" if mixer is not None else "macaron_ffn",
    )(*args)


def _rwkv_pre_kernel(*refs, tiles_per_seq, has_vres):
    if has_vres:
        (x_ref, xp_ref, g_ref, mod_ref, mu_ref, wr_ref, wk_ref, wv_ref, w0_ref, w1_ref, w2_ref,
         a0_ref, a1_ref, a2_ref, g1_ref, g2_ref, v0_ref, v1_ref, v2_ref, vf_ref,
         r_ref, k_ref, v_ref, ld_ref, a_ref, gt_ref, h_ref, dx_ref) = refs
    else:
        (x_ref, xp_ref, g_ref, mod_ref, mu_ref, wr_ref, wk_ref, wv_ref, w0_ref, w1_ref, w2_ref,
         a0_ref, a1_ref, a2_ref, g1_ref, g2_ref,
         r_ref, k_ref, v_ref, ld_ref, a_ref, gt_ref, h_ref, dx_ref) = refs
    shift = mod_ref[0, 3:4, :]
    scale = mod_ref[0, 4:5, :]
    g = g_ref[...]
    h = _adaln(x_ref[...], g, shift, scale)
    first = (pl.program_id(0) % tiles_per_seq) == 0
    hp = _adaln(xp_ref[...], g, shift, scale)
    hp = jnp.where(first, 0.0, hp)
    h_ref[...] = h
    dx_ref[...] = _shift_rows(h, hp, 1) - h

    def chain(lo, hi, delay):
        for _ in range(delay):
            yield

        def mixed(j):
            return (h_ref[lo:hi, :] + dx_ref[lo:hi, :] * mu_ref[j:j + 1, :]).astype(BF16)

        w_mid = jnp.tanh(_dot(mixed(1), w1_ref[...])).astype(BF16)
        a_mid = _dot(mixed(4), a1_ref[...]).astype(BF16)
        yield
        wl = w0_ref[...] + _dot(w_mid, w2_ref[...])
        a_pre = a0_ref[...] + _dot(a_mid, a2_ref[...])
        g_mid = _sigmoid(_dot(mixed(5), g1_ref[...])).astype(BF16)
        yield
        r_ref[lo:hi, :] = _dot(mixed(0), wr_ref[...]).astype(r_ref.dtype)
        yield
        ld_ref[lo:hi, :] = (-DECAY_SCALE) * _sigmoid(wl)
        k_ref[lo:hi, :] = _dot(mixed(2), wk_ref[...]).astype(k_ref.dtype)
        yield
        a_ref[lo:hi, :] = _sigmoid(a_pre).astype(a_ref.dtype)
        gt_ref[lo:hi, :] = _dot(g_mid, g2_ref[...]).astype(gt_ref.dtype)
        xv = mixed(3)
        if has_vres:
            mixv = _sigmoid(v0_ref[...] + _dot(_dot(xv, v1_ref[...]).astype(BF16), v2_ref[...]))
        yield
        v = _dot(xv, wv_ref[...])
        if has_vres:
            v = v + (vf_ref[lo:hi, :].astype(F32) - v) * mixv
        v_ref[lo:hi, :] = v.astype(v_ref.dtype)

    tm = h_ref.shape[0]
    nsplit = 2 if tm % 32 == 0 else 1
    rows = tm // nsplit
    _run_interleaved([chain(i * rows, (i + 1) * rows, 2 * i) for i in range(nsplit)])


def _pad_cols(w, n):
    return jnp.pad(w, ((0, 0), (0, n - w.shape[1])))


def _pad_rows(w, n):
    return jnp.pad(w, ((0, n - w.shape[0]), (0, 0)))


def _round_up(n, m):
    return (n + m - 1) // m * m


def _lora_pair(w_a, w_b):
    n = _round_up(w_a.shape[1], LANE)
    return _pad_cols(w_a, n).astype(BF16), _pad_rows(w_b, n).astype(BF16)


def _rwkv_pre(x2, g, mod, mu, w_rkv, w0, w1, w2, a0, a1, a2, g1, g2, vres, v_first, *, seq, tm=512):
    t, d = x2.shape
    tm = min(tm, seq)
    tiles_per_seq = seq // tm
    has_vres = vres is not None
    const = lambda i: (0, 0)
    res = dict(pipeline_mode=pl.Buffered(1))
    row = lambda i: (i, 0)
    w1p, w2p = _lora_pair(w1, w2)
    a1p, a2p = _lora_pair(a1, a2)
    g1p, g2p = _lora_pair(g1, g2)
    args = [x2, x2, g.reshape(1, d), mod, _pad_rows(mu, 8),
            w_rkv[0].astype(BF16), w_rkv[1].astype(BF16), w_rkv[2].astype(BF16),
            w0.reshape(1, d), w1p, w2p, a0.reshape(1, d), a1p, a2p, g1p, g2p]
    in_specs = [
        pl.BlockSpec((tm, d), row),
        pl.BlockSpec((8, d), lambda i: (jnp.maximum(i * (tm // 8) - 1, 0), 0)),
        pl.BlockSpec((1, d), const),
        pl.BlockSpec((1, N_SUB * 3, d), lambda i: (i // tiles_per_seq, 0, 0)),
        pl.BlockSpec((8, d), const),
        pl.BlockSpec((d, d), const, **res), pl.BlockSpec((d, d), const, **res),
        pl.BlockSpec((d, d), const, **res),
        pl.BlockSpec((1, d), const),
        pl.BlockSpec(w1p.shape, const, **res), pl.BlockSpec(w2p.shape, const, **res),
        pl.BlockSpec((1, d), const),
        pl.BlockSpec(a1p.shape, const, **res), pl.BlockSpec(a2p.shape, const, **res),
        pl.BlockSpec(g1p.shape, const, **res), pl.BlockSpec(g2p.shape, const, **res),
    ]
    if has_vres:
        v0, v1, v2 = vres
        v1p, v2p = _lora_pair(v1, v2)
        args += [v0.reshape(1, d), v1p, v2p, v_first]
        in_specs += [pl.BlockSpec((1, d), const), pl.BlockSpec(v1p.shape, const, **res),
                     pl.BlockSpec(v2p.shape, const, **res), pl.BlockSpec((tm, d), row)]
    out_dtypes = (BF16, BF16, BF16, F32, BF16, BF16)
    return pl.pallas_call(
        functools.partial(_rwkv_pre_kernel, tiles_per_seq=tiles_per_seq, has_vres=has_vres),
        grid=(t // tm,),
        in_specs=in_specs,
        out_specs=[pl.BlockSpec((tm, d), row)] * 6,
        out_shape=[jax.ShapeDtypeStruct((t, d), dt_) for dt_ in out_dtypes],
        scratch_shapes=[pltpu.VMEM((tm, d), F32), pltpu.VMEM((tm, d), F32)],
        compiler_params=_cparams(("parallel",)),
        name="rwkv7_projections",
    )(*args)


def _rwkv_scan_kernel(*refs):
    st_ref = refs[-1]

    @pl.when(pl.program_id(2) == 0)
    def _():
        st_ref[...] = jnp.zeros_like(st_ref)

    nb = refs[0].shape[0]
    ng = st_ref.shape[0] // nb
    w = st_ref.shape[1]
    head_ones = jnp.where((_iota((w, w), 0) >> 6) == (_iota((w, w), 1) >> 6), 1.0, 0.0).astype(BF16)
    _run_interleaved([_rwkv_chain(bi, gi, *refs) for bi in range(nb) for gi in range(ng)],
                     shared_rhs=head_ones)


def _rwkv_chain(bi, gi, r_ref, k_ref, v_ref, ld_ref, a_ref, gt_ref, kk_ref, ka_ref, rk_ref,
                lnw_ref, lnb_ref, z_ref, st_ref):
    L = RW_CHUNK
    W = RW_GROUP * HEAD
    R = RW_GROUP * L
    lo, hi = gi * W, (gi + 1) * W
    si = bi * (st_ref.shape[0] // r_ref.shape[0]) + gi

    r = r_ref[bi, :, lo:hi].astype(F32)
    k = k_ref[bi, :, lo:hi].astype(F32)
    v = v_ref[bi, :, lo:hi].astype(F32)
    ld = ld_ref[bi, :, lo:hi]
    a = a_ref[bi, :, lo:hi].astype(F32)

    kkn = k * kk_ref[:, lo:hi]
    tril = jnp.where(_iota((L, L), 0) >= _iota((L, L), 1), 1.0, 0.0).astype(BF16)
    cum = _dot_exact_lhs(tril, ld)
    kk_ss = yield kkn * kkn, 1
    kk = kkn / jnp.maximum(jnp.sqrt(kk_ss), 1e-12)
    kmod = k * (1.0 + (a - 1.0) * ka_ref[:, lo:hi])
    c_end = cum[L - 1:L, :]
    e_pos = jnp.exp(cum)
    e_neg = jnp.exp(-cum)
    e_prev = jnp.exp(cum - ld)
    e_end = jnp.exp(c_end - cum)
    kka = kk * a

    rows = _iota((R, W), 0)
    cols = _iota((R, W), 1)
    same_head = (rows >> 6) == (cols >> 6)

    def bd(x):
        xb = x.astype(BF16)
        return jnp.where(same_head, jnp.concatenate([xb] * RW_GROUP, axis=0), jnp.zeros((), BF16))

    ar = jnp.concatenate([-kk * e_prev, r * e_pos], axis=0).astype(BF16)
    bk_s = jnp.concatenate([bd(kka * e_neg), bd(kmod * e_neg)], axis=0)
    v_s = bd(v)
    prod = _dot_nt(ar, bk_s)
    st = st_ref[si]
    from_state = _dot_nt(ar, st.astype(BF16))
    yield
    mt = _iota((L, R), 0)
    ms = _iota((L, R), 1) & (L - 1)
    strict = mt > ms
    incl = mt >= ms
    a_ab = jnp.where(strict, prod[0:L, 0:R], 0.0)
    a_ak = jnp.where(strict, prod[0:L, R:2 * R], 0.0)
    a_rb = jnp.where(incl, prod[L:2 * L, 0:R], 0.0)
    a_rk = jnp.where(incl, prod[L:2 * L, R:2 * R], 0.0)

    def mm(x, y):
        return _dot(x.astype(BF16), bd(y))

    eye = jnp.where(mt == ms, 1.0, 0.0)
    a_d = jnp.where((mt >> 3) == (ms >> 3), a_ab, 0.0)
    a_d2 = mm(a_d, a_d)
    rhs = from_state[0:L] + _dot(a_ak.astype(BF16), v_s)
    bonus = (yield r * kmod * rk_ref[:, lo:hi], 1) * v
    a_d4 = mm(a_d2, a_d2)
    tinv = mm(eye + a_d, eye + a_d2)
    yield
    tinv = mm(tinv, eye + a_d4)
    yield
    for sh in range(3, L.bit_length() - 1):
        off = ((mt >> (sh + 1)) == (ms >> (sh + 1))) & ((mt >> sh) != (ms >> sh))
        half = mm(tinv, jnp.where(off, a_ab, 0.0))
        yield
        tinv = tinv + mm(half, tinv)
        yield

    u = mm(tinv, rhs)
    yield
    y = from_state[L:2 * L] + _dot(
        jnp.concatenate([a_rb.astype(BF16), a_rk.astype(BF16)], axis=1),
        jnp.concatenate([bd(u), v_s], axis=0))
    new_terms = _dot_tn(jnp.concatenate([u, v], axis=0).astype(BF16),
                        jnp.concatenate([kka * e_end, kmod * e_end], axis=0).astype(BF16))
    st_ref[si] = st * jnp.exp(c_end) + jnp.where(same_head, new_terms, 0.0)
    mean = (yield y, 2) * (1.0 / HEAD)
    yc = y - mean
    var = (yield yc * yc, 1) * (1.0 / HEAD)
    yn = yc * lax.rsqrt(var + RW_GN_EPS) * lnw_ref[:, lo:hi] + lnb_ref[:, lo:hi]
    z_ref[bi, :, lo:hi] = ((yn + bonus) * gt_ref[bi, :, lo:hi].astype(F32)).astype(z_ref.dtype)


def _rwkv_scan(r, k, v, ld, a, gt, k_k, k_a, r_k, ln_w, ln_b, *, lanes=1024, nb=4):
    bsz, seq, d = r.shape
    assert RW_CHUNK == HEAD, "the side-by-side per-head matrix layout needs chunk length == head width"
    w = RW_GROUP * HEAD
    lanes = min(lanes, d)
    nb = min(nb, bsz)
    act = pl.BlockSpec((nb, RW_CHUNK, lanes), lambda b, g, c: (b, c, g))
    par = pl.BlockSpec((1, lanes), lambda b, g, c: (0, g))
    return pl.pallas_call(
        _rwkv_scan_kernel,
        grid=(bsz // nb, d // lanes, seq // RW_CHUNK),
        in_specs=[act] * 6 + [par] * 5,
        out_specs=act,
        out_shape=jax.ShapeDtypeStruct((bsz, seq, d), BF16),
        scratch_shapes=[pltpu.VMEM((nb * (lanes // w), w, w), F32)],
        compiler_params=_cparams(("parallel", "parallel", "arbitrary")),
        name="rwkv7_chunk_scan",
    )(r, k, v, ld, a, gt, k_k.reshape(1, d), k_a.reshape(1, d), r_k.reshape(1, d),
      ln_w.reshape(1, d), ln_b.reshape(1, d))


def _mamba_in_kernel(x_ref, g_ref, mod_ref, wz_ref, wx_ref, wb_ref, wc_ref, wdt_ref,
                     z_ref, xs_ref, b_ref, c_ref, dt_ref, h_ref):
    h_ref[...] = _adaln(x_ref[...], g_ref[...], mod_ref[0, 3:4, :], mod_ref[0, 4:5, :]).astype(BF16)
    z_ref[...] = _dot(h_ref[...], wz_ref[...]).astype(z_ref.dtype)
    xs_ref[...] = _dot(h_ref[...], wx_ref[...]).astype(xs_ref.dtype)
    b_ref[...] = _dot(h_ref[...], wb_ref[...]).astype(b_ref.dtype)
    c_ref[...] = _dot(h_ref[...], wc_ref[...]).astype(c_ref.dtype)
    dt_ref[...] = _dot(h_ref[...], wdt_ref[...])


def _mamba_in(x2, g, mod, w_in, *, seq, d_inner, tm=512):
    t, d = x2.shape
    gn = MB_GROUPS * MB_STATE
    tm = min(tm, seq)
    tiles_per_seq = seq // tm
    wz = w_in[:, :d_inner].astype(BF16)
    wx = w_in[:, d_inner:2 * d_inner].astype(BF16)
    wb = w_in[:, 2 * d_inner:2 * d_inner + gn].astype(BF16)
    wc = w_in[:, 2 * d_inner + gn:2 * d_inner + 2 * gn].astype(BF16)
    wdt = _pad_cols(w_in[:, 2 * d_inner + 2 * gn:], LANE).astype(BF16)
    const = lambda i: (0, 0)
    row = lambda i: (i, 0)
    res = dict(pipeline_mode=pl.Buffered(1))
    widths = (d_inner, d_inner, gn, gn, LANE)
    return pl.pallas_call(
        _mamba_in_kernel,
        grid=(t // tm,),
        in_specs=[pl.BlockSpec((tm, d), row), pl.BlockSpec((1, d), const),
                  pl.BlockSpec((1, N_SUB * 3, d), lambda i: (i // tiles_per_seq, 0, 0))]
        + [pl.BlockSpec((d, n), const, **res) for n in widths],
        out_specs=[pl.BlockSpec((tm, n), row) for n in widths],
        out_shape=[jax.ShapeDtypeStruct((t, n), dt_)
                   for n, dt_ in zip(widths, (BF16, BF16, BF16, BF16, F32))],
        scratch_shapes=[pltpu.VMEM((tm, d), BF16)],
        compiler_params=_cparams(("parallel",)),
        name="mamba2_in_proj",
    )(x2, g.reshape(1, d), mod, wz, wx, wb, wc, wdt)


def _ssd_kernel(xs_ref, b_ref, c_ref, z_ref, dt_ref, cwx_ref, cwb_ref, cwc_ref, cbx_ref, cbb_ref,
                cbc_ref, dtb_ref, alog_ref, dsk_ref, ng_ref, y_ref,
                st_ref, sx_ref, sb_ref, sc_ref):
    L = MB_CHUNK
    stages = ((sx_ref, xs_ref), (sb_ref, b_ref), (sc_ref, c_ref))

    @pl.when(pl.program_id(1) == 0)
    def _():
        st_ref[...] = jnp.zeros_like(st_ref)
        for s_ref, _ in stages:
            s_ref[0:8, :] = jnp.zeros((8, s_ref.shape[1]), F32)

    for s_ref, cur_ref in stages:
        s_ref[8:8 + L, :] = cur_ref[0].astype(F32)

    dt = _softplus(dt_ref[0] + dtb_ref[...])
    da = dt * (-jnp.exp(alog_ref[...]))
    tril = jnp.where(_iota((L, L), 0) >= _iota((L, L), 1), 1.0, 0.0).astype(BF16)
    acum = _dot_exact_lhs(tril, da) * LOG2E
    shared = (_split3(dt), _split3(acum))
    refs = (sx_ref, sb_ref, sc_ref, z_ref, cwx_ref, cwb_ref, cwc_ref, cbx_ref, cbb_ref, cbc_ref,
            dsk_ref, ng_ref, y_ref, st_ref)
    _run_interleaved([_ssd_chain(g, shared, *refs) for g in range(st_ref.shape[0])])

    for s_ref, _ in stages:
        s_ref[0:8, :] = s_ref[L:L + 8, :]


def _ssd_chain(g, shared, sx_ref, sb_ref, sc_ref, z_ref, cwx_ref, cwb_ref, cwc_ref, cbx_ref,
               cbb_ref, cbc_ref, dsk_ref, ng_ref, y_ref, st_ref):
    L = MB_CHUNK
    n = MB_STATE
    wx = st_ref.shape[2]
    hg = wx // HEAD
    dt3, ac3 = shared
    xl, xh = g * wx, (g + 1) * wx
    nl, nh = g * n, (g + 1) * n

    def conv_silu(s_ref, w_ref, bias_ref, lo, hi):
        staged = s_ref[:, lo:hi]
        prev = pltpu.roll(staged, 1, 0)
        w = [w_ref[j:j + 1, lo:hi] for j in range(MB_CONV)]
        older = pltpu.roll(staged * w[1] + prev * w[0], 2, 0)
        acc = bias_ref[:, lo:hi] + staged * w[3] + prev * w[2] + older
        return _silu(acc[8:8 + L])

    xs = conv_silu(sx_ref, cwx_ref, cbx_ref, xl, xh)
    bm = conv_silu(sb_ref, cwb_ref, cbb_ref, nl, nh).astype(BF16)
    cm = conv_silu(sc_ref, cwc_ref, cbc_ref, nl, nh).astype(BF16)

    head0 = hg * g
    sel_x = jnp.where(_iota((LANE, wx), 0) == head0 + (_iota((LANE, wx), 1) >> 6), 1.0, 0.0).astype(BF16)
    sel_r = jnp.where(_iota((8, LANE), 1) == head0 + _iota((8, LANE), 0), 1.0, 0.0).astype(BF16)
    sel_c = jnp.where(_iota((LANE, hg * L), 0) == head0 + (_iota((LANE, hg * L), 1) >> 7), 1.0, 0.0).astype(BF16)
    both = [jnp.concatenate([d_p, a_p], axis=0) for d_p, a_p in zip(dt3, ac3)]
    in_x = _dot(both[0], sel_x) + _dot(both[1], sel_x) + _dot(both[2], sel_x)
    dt_x, ac_x = in_x[0:L], in_x[L:2 * L]
    ac_row = _dot_nt(sel_r, ac3[0]) + _dot_nt(sel_r, ac3[1]) + _dot_nt(sel_r, ac3[2])
    ac_col = _dot(ac3[0], sel_c) + _dot(ac3[1], sel_c) + _dot(ac3[2], sel_c)
    cb = _dot_nt(cm, bm)
    st = st_ref[g]
    y_off = _dot(cm, st.astype(BF16))
    yield

    xdt = xs * dt_x
    causal = _iota((L, L), 0) >= _iota((L, L), 1)
    lane_head = _iota((L, wx), 1) >> 6
    xdt_b = xdt.astype(BF16)
    g_parts = []
    x_parts = []
    for j in range(hg):
        seg = jnp.where(causal, ac_col[:, j * L:(j + 1) * L] - ac_row[j:j + 1, :], -jnp.inf)
        g_parts.append((cb * jnp.exp2(seg)).astype(BF16))
        x_parts.append(jnp.where(lane_head == j, xdt_b, jnp.zeros((), BF16)))
    y_diag = _dot(jnp.concatenate(g_parts, axis=1), jnp.concatenate(x_parts, axis=0))
    ac_last = ac_x[L - 1:L, :]
    xdec = (xdt * jnp.exp2(ac_last - ac_x)).astype(BF16)
    st_ref[g] = st * jnp.exp2(ac_last) + _dot_tn(bm, xdec)
    yield

    y = y_diag + y_off * jnp.exp2(ac_x) + xs * dsk_ref[:, xl:xh]
    y = y * _silu(z_ref[0, :, xl:xh].astype(F32))
    ms = jnp.mean(y * y, axis=-1, keepdims=True)
    y_ref[0, :, xl:xh] = (y * lax.rsqrt(ms + NORM_EPS) * ng_ref[:, xl:xh]).astype(y_ref.dtype)


def _ssd(xs, bm, cm, z, dt, conv_w, conv_b, dt_bias, a_log, d_skip, norm_g):
    bsz, seq, d_inner = xs.shape
    n = MB_STATE
    gn = MB_GROUPS * n
    wx = d_inner // MB_GROUPS
    heads = d_inner // HEAD
    assert conv_w.shape[0] == MB_CONV == 4, "the in-kernel conv is written for 4 taps"
    cw = _pad_rows(conv_w, 8)
    cwx, cwb, cwc = cw[:, :d_inner], cw[:, d_inner:d_inner + gn], cw[:, d_inner + gn:]
    cb = conv_b.reshape(1, -1)
    cbx, cbb, cbc = cb[:, :d_inner], cb[:, d_inner:d_inner + gn], cb[:, d_inner + gn:]
    dtb = _pad_cols(dt_bias.reshape(1, heads), LANE)
    alog = _pad_cols(a_log.reshape(1, heads), LANE)
    dsk = jnp.repeat(d_skip, HEAD).reshape(1, d_inner)
    tok = lambda w: pl.BlockSpec((1, MB_CHUNK, w), lambda b, c: (b, c, 0))
    par = lambda r, w: pl.BlockSpec((r, w), lambda b, c: (0, 0))
    return pl.pallas_call(
        _ssd_kernel,
        grid=(bsz, seq // MB_CHUNK),
        in_specs=[tok(d_inner), tok(gn), tok(gn), tok(d_inner), tok(LANE),
                  par(8, d_inner), par(8, gn), par(8, gn), par(1, d_inner), par(1, gn), par(1, gn),
                  par(1, LANE), par(1, LANE), par(1, d_inner), par(1, d_inner)],
        out_specs=tok(d_inner),
        out_shape=jax.ShapeDtypeStruct((bsz, seq, d_inner), BF16),
        scratch_shapes=[pltpu.VMEM((MB_GROUPS, n, wx), F32),
                        pltpu.VMEM((8 + MB_CHUNK,d_inner), F32),
                        pltpu.VMEM((8 + MB_CHUNK,gn), F32),
                        pltpu.VMEM((8 + MB_CHUNK,gn), F32)],
        compiler_params=_cparams(("parallel", "arbitrary")),
        name="mamba2_conv_ssd",
    )(xs, bm, cm, z, dt, cwx, cwb, cwc, cbx, cbb, cbc, dtb, alog, dsk, norm_g.reshape(1, d_inner))


def _swa_qkv_kernel(x_ref, g_ref, mod_ref, wq_ref, wk_ref, wv_ref, q_ref, k_ref, v_ref, h_ref):
    h_ref[...] = _adaln(x_ref[...], g_ref[...], mod_ref[0, 3:4, :], mod_ref[0, 4:5, :]).astype(BF16)
    q_ref[...] = _dot(h_ref[...], wq_ref[...]).astype(q_ref.dtype)
    k_ref[...] = _dot(h_ref[...], wk_ref[...]).astype(k_ref.dtype)
    v_ref[...] = _dot(h_ref[...], wv_ref[...]).astype(v_ref.dtype)


def _swa_qkv(x2, g, mod, w_qkv, *, seq, tm=512):
    t, d = x2.shape
    nq = d
    nk = SW_KV_HEADS * HEAD
    tm = min(tm, seq)
    tiles_per_seq = seq // tm
    wq = w_qkv[:, :nq].astype(BF16)
    wk = w_qkv[:, nq:nq + nk].astype(BF16)
    wv = w_qkv[:, nq + nk:].astype(BF16)
    const = lambda i: (0, 0)
    row = lambda i: (i, 0)
    res = dict(pipeline_mode=pl.Buffered(1))
    widths = (nq, nk, nk)
    return pl.pallas_call(
        _swa_qkv_kernel,
        grid=(t // tm,),
        in_specs=[pl.BlockSpec((tm, d), row), pl.BlockSpec((1, d), const),
                  pl.BlockSpec((1, N_SUB * 3, d), lambda i: (i // tiles_per_seq, 0, 0))]
        + [pl.BlockSpec((d, n), const, **res) for n in widths],
        out_specs=[pl.BlockSpec((tm, n), row) for n in widths],
        out_shape=[jax.ShapeDtypeStruct((t, n), BF16) for n in widths],
        scratch_shapes=[pltpu.VMEM((tm, d), BF16)],
        compiler_params=_cparams(("parallel",)),
        name="swa_qkv_proj",
    )(x2, g.reshape(1, d), mod, wq, wk, wv)


def _swa_kernel(q_ref, kc_ref, kp_ref, vc_ref, vp_ref, qn_ref, kn_ref, sink_ref, o_ref):
    T = SW_BLOCK
    wk = kc_ref.shape[2]
    first_key = jnp.where(pl.program_id(1) > 0, 0, T)

    def head_norm(x, g):
        w = x.shape[1]
        head_ones = jnp.where((_iota((w, w), 0) >> 6) == (_iota((w, w), 1) >> 6), 1.0, 0.0).astype(BF16)
        ms = _dot((x * x).astype(BF16), head_ones) * (1.0 / HEAD)
        return x * lax.rsqrt(ms + NORM_EPS) * g

    k_all = jnp.concatenate([kp_ref[0], kc_ref[0]], axis=0).astype(F32)
    k_all = head_norm(k_all, kn_ref[...]).astype(BF16)
    v_all = jnp.concatenate([vp_ref[0], vc_ref[0]], axis=0).astype(BF16)

    qi = _iota((T, 2 * T), 0)
    si = _iota((T, 2 * T), 1)
    rel = qi + T - si
    band = (rel >= 0) & (rel < T)
    chains = []
    for blk in range(q_ref.shape[1] // T):
        mask = band & (si >= first_key) if blk == 0 else band
        shared = (k_all[blk * T:(blk + 2) * T], v_all[blk * T:(blk + 2) * T], mask, si == qi, head_norm)
        chains += [_swa_chain(blk, kv, shared, q_ref, qn_ref, sink_ref, o_ref) for kv in range(wk // HEAD)]
    _run_interleaved(chains)


def _swa_chain(blk, kv, shared, q_ref, qn_ref, sink_ref, o_ref):
    T = SW_BLOCK
    kcat, vcat, mask, sink_slot, head_norm = shared
    r0, r1 = blk * T, (blk + 1) * T
    wk = kcat.shape[1]
    gq = q_ref.shape[2] // wk
    wq = gq * HEAD
    scale = HEAD ** -0.5 * LOG2E
    lo, hi = kv * wq, (kv + 1) * wq

    rep = jnp.where(_iota((wk, wq), 0) == kv * HEAD + (_iota((wk, wq), 1) & (HEAD - 1)), 1.0, 0.0).astype(BF16)
    k_rep = _dot(kcat, rep).astype(BF16)
    v_rep = _dot(vcat, rep).astype(BF16)
    q = head_norm(q_ref[0, r0:r1, lo:hi].astype(F32), qn_ref[...]).astype(BF16)
    yield
    lane_head = _iota((T, wq), 1) >> 6
    scores = []
    for j in range(gq):
        q_j = jnp.where(lane_head == j, q, jnp.zeros((), BF16))
        scores.append(_dot_nt(q_j, k_rep) * scale)
        yield
    ones = jnp.ones((2 * T, wq), BF16)
    probs = []
    for j in range(gq):
        sink = sink_ref[0:1, kv * gq + j:kv * gq + j + 1] * LOG2E
        s = jnp.where(sink_slot, sink, jnp.where(mask, scores[j], NEG_INF))
        m = jnp.max(s, axis=-1, keepdims=True)
        yield
        probs.append(jnp.exp2(s - m).astype(BF16))
    o = jnp.zeros((T, wq), F32)
    for j in range(gq):
        denom = _dot(probs[j], ones)
        o_j = _dot(jnp.where(sink_slot, jnp.zeros((), BF16), probs[j]), v_rep)
        yield
        o = o + jnp.where(lane_head == j, o_j / denom, 0.0)
    o_ref[0, r0:r1, lo:hi] = o.astype(o_ref.dtype)


def _swa(q, k, v, q_norm, k_norm, sinks, *, nblk=4):
    bsz, seq, dq = q.shape
    dk = k.shape[2]
    nq = dq // HEAD
    nblk = min(nblk, seq // SW_BLOCK)
    rows = nblk * SW_BLOCK
    cur = lambda b, i: (b, i, 0)
    prev = lambda b, i: (b, jnp.maximum(i * nblk - 1, 0), 0)
    const = lambda b, i: (0, 0)
    qn = jnp.tile(q_norm, dq // SW_KV_HEADS // HEAD).reshape(1, -1)
    kn = jnp.tile(k_norm, dk // HEAD).reshape(1, dk)
    return pl.pallas_call(
        _swa_kernel,
        grid=(bsz, seq // rows),
        in_specs=[pl.BlockSpec((1, rows, dq), cur),
                  pl.BlockSpec((1, rows, dk), cur), pl.BlockSpec((1, SW_BLOCK, dk), prev),
                  pl.BlockSpec((1, rows, dk), cur), pl.BlockSpec((1, SW_BLOCK, dk), prev),
                  pl.BlockSpec(qn.shape, const), pl.BlockSpec((1, dk), const),
                  pl.BlockSpec((1, nq), const)],
        out_specs=pl.BlockSpec((1, rows, dq), cur),
        out_shape=jax.ShapeDtypeStruct((bsz, seq, dq), BF16),
        compiler_params=_cparams(("parallel", "parallel")),
        name="swa_sink_attention",
    )(q, k, k, v, v, qn, kn, sinks.reshape(1, nq))


def kernel(x, c, ada_w, ada_b, norm_g, ffn_w_in, ffn_w_out, rw_mu, rw_w_rkv, rw_w_o, rw_w0, rw_w1, rw_w2, rw_a0, rw_a1, rw_a2, rw_g1, rw_g2, rw_k_k, rw_k_a, rw_r_k, rw_ln_w, rw_ln_b, rw_v0, rw_v1, rw_v2, mb_w_in, mb_conv_w, mb_conv_b, mb_dt_bias, mb_A_log, mb_D, mb_norm_g, mb_w_out, sw_w_qkv, sw_q_norm, sw_k_norm, sw_sinks, sw_w_o):
    bsz, seq, d = x.shape
    depth = ada_w.shape[0]
    t = bsz * seq
    mods = _modulation(c, ada_w, ada_b)
    x2 = x.reshape(t, d)
    v_first = None
    for i in range(depth):
        mod = mods[i]
        x2 = _ffn(x2, norm_g[i, 0], mod, ffn_w_in[i, 0].astype(BF16), ffn_w_out[i, 0].astype(BF16),
                  sub=0, seq=seq)
        kind, j = i % 3, i // 3
        if kind == 0:
            vres = None if v_first is None else (rw_v0[j - 1], rw_v1[j - 1], rw_v2[j - 1])
            r, k, v, ld, a, gt = _rwkv_pre(
                x2, norm_g[i, 1], mod, rw_mu[j], rw_w_rkv[j], rw_w0[j], rw_w1[j], rw_w2[j],
                rw_a0[j], rw_a1[j], rw_a2[j], rw_g1[j], rw_g2[j], vres, v_first, seq=seq)
            if v_first is None:
                v_first = v
            sh = (bsz, seq, d)
            z = _rwkv_scan(r.reshape(sh), k.reshape(sh), v.reshape(sh), ld.reshape(sh), a.reshape(sh),
                           gt.reshape(sh), rw_k_k[j], rw_k_a[j], rw_r_k[j], rw_ln_w[j], rw_ln_b[j])
            mixer = (z.reshape(t, d), rw_w_o[j].astype(BF16))
        elif kind == 1:
            d_inner = mb_w_out.shape[1]
            z, xs, bm, cm, dt = _mamba_in(x2, norm_g[i, 1], mod, mb_w_in[j], seq=seq, d_inner=d_inner)
            y = _ssd(xs.reshape(bsz, seq, -1), bm.reshape(bsz, seq, -1), cm.reshape(bsz, seq, -1),
                     z.reshape(bsz, seq, -1), dt.reshape(bsz, seq, -1), mb_conv_w[j], mb_conv_b[j],
                     mb_dt_bias[j], mb_A_log[j], mb_D[j], mb_norm_g[j])
            mixer = (y.reshape(t, d_inner), mb_w_out[j].astype(BF16))
        else:
            q, k, v = _swa_qkv(x2, norm_g[i, 1], mod, sw_w_qkv[j], seq=seq)
            o = _swa(q.reshape(bsz, seq, -1), k.reshape(bsz, seq, -1), v.reshape(bsz, seq, -1),
                     sw_q_norm[j], sw_k_norm[j], sw_sinks[j])
            mixer = (o.reshape(t, -1), sw_w_o[j].astype(BF16))
        x2 = _ffn(x2, norm_g[i, 2], mod, ffn_w_in[i, 1].astype(BF16), ffn_w_out[i, 1].astype(BF16),
                  sub=2, seq=seq, mixer=mixer)
    return x2.reshape(bsz, seq, d)
```

```python
import functools

import jax
import jax.numpy as jnp
from jax import lax
from jax.experimental import pallas as pl
from jax.experimental.pallas import tpu as pltpu

F32 = jnp.float32
BF16 = jnp.bfloat16

NORM_EPS = 1e-6
MACARON_W = 0.5
N_SUB = 3
HEAD = 64
RW_GN_EPS = 64e-5
RW_CHUNK = 64
RW_GROUP = 2
MB_GROUPS = 8
MB_STATE = 128
MB_CONV = 4
MB_CHUNK = 128
SW_KV_HEADS = 4
SW_BLOCK = 128
NEG_INF = -1e30
LOG2E = 1.4426950408889634
DECAY_SCALE = 0.6065306597126334
LANE = 128
VMEM_LIMIT = 56 * 1024 * 1024


def _cparams(sem):
    return pltpu.CompilerParams(dimension_semantics=sem, vmem_limit_bytes=VMEM_LIMIT)


def _dot(a, b):
    return jnp.dot(a, b, preferred_element_type=F32)


def _dot_nt(a, b):
    return lax.dot_general(a, b, (((1,), (1,)), ((), ())), preferred_element_type=F32)


def _dot_tn(a, b):
    return lax.dot_general(a, b, (((0,), (0,)), ((), ())), preferred_element_type=F32)


def _split3(x):
    hi = x.astype(BF16)
    r1 = x - hi.astype(F32)
    mid = r1.astype(BF16)
    lo = (r1 - mid.astype(F32)).astype(BF16)
    return hi, mid, lo


def _dot_exact_lhs(m, x):
    hi, mid, lo = _split3(x)
    return _dot(m, hi) + _dot(m, mid) + _dot(m, lo)


def _sigmoid(x):
    return 0.5 + 0.5 * jnp.tanh(0.5 * x)


def _silu(x):
    h = 0.5 * x
    return h + h * jnp.tanh(h)


def _softplus(x):
    return jnp.maximum(x, 0.0) + jnp.log(1.0 + jnp.exp(-jnp.abs(x)))


def _adaln(x, g, shift, scale):
    ms = jnp.mean(x * x, axis=-1, keepdims=True)
    return (x * lax.rsqrt(ms + NORM_EPS)) * (g * (1.0 + scale)) + shift


def _iota(shape, axis):
    return lax.broadcasted_iota(jnp.int32, shape, axis)


def _shift_rows(cur, tail, s):
    rolled = pltpu.roll(cur, s, 0)
    head = jnp.where(_iota(tail.shape, 0) < s, pltpu.roll(tail, s, 0), rolled[0:8])
    return jnp.concatenate([head, rolled[8:]], axis=0)


def _run_interleaved(chains, shared_rhs=None):
    done = object()
    reqs = [next(ch, done) for ch in chains]
    while chains:
        live = [(ch, rq) for ch, rq in zip(chains, reqs) if rq is not done]
        asked = [rq[0] for _, rq in live if rq is not None]
        answers = iter(())
        if asked:
            passes = max(rq[1] for _, rq in live if rq is not None)
            x = jnp.concatenate(asked, axis=0)
            hi_part = x.astype(BF16)
            prod = _dot(hi_part, shared_rhs)
            if passes == 2:
                prod = prod + _dot((x - hi_part.astype(F32)).astype(BF16), shared_rhs)
            n = asked[0].shape[0]
            answers = iter([prod[i * n:(i + 1) * n] for i in range(len(asked))])
        chains, reqs = [], []
        for ch, rq in live:
            try:
                reqs.append(ch.send(next(answers) if rq is not None else None))
                chains.append(ch)
            except StopIteration:
                pass


def _mod_kernel(c_ref, w_ref, b_ref, o_ref):
    ca = _silu(c_ref[...])
    a_hi, a_mid, a_lo = _split3(ca)
    w_hi, w_mid, w_lo = _split3(w_ref[0])
    acc = _dot(a_hi, w_hi) + (_dot(a_hi, w_mid) + _dot(a_mid, w_hi))
    acc = acc + (_dot(a_hi, w_lo) + _dot(a_mid, w_mid) + _dot(a_lo, w_hi))
    o_ref[0] = acc + b_ref[0]


def _modulation(c, ada_w, ada_b):
    depth, d, n = ada_w.shape
    bsz = c.shape[0]
    tn = 1152
    out = pl.pallas_call(
        _mod_kernel,
        grid=(depth, n // tn),
        in_specs=[
            pl.BlockSpec((bsz, d), lambda l, j: (0, 0)),
            pl.BlockSpec((1, d, tn), lambda l, j: (l, 0, j)),
            pl.BlockSpec((1, 1, tn), lambda l, j: (l, 0, j)),
        ],
        out_specs=pl.BlockSpec((1, bsz, tn), lambda l, j: (l, 0, j)),
        out_shape=jax.ShapeDtypeStruct((depth, bsz, n), F32),
        compiler_params=_cparams(("parallel", "parallel")),
        name="adaln_modulation",
    )(c, ada_w, ada_b.reshape(depth, 1, n))
    return out.reshape(depth, bsz, N_SUB * 3, d)


def _ffn_kernel(*refs, sub, fc, fused_mixer):
    if fused_mixer:
        x_ref, g_ref, mod_ref, win_ref, wout_ref, z_ref, wo_ref, o_ref, h_ref, acc_ref = refs
        res_ref = o_ref
    else:
        x_ref, g_ref, mod_ref, win_ref, wout_ref, o_ref, h_ref, acc_ref = refs
        res_ref = x_ref
    dff = wout_ref.shape[0]
    shift = mod_ref[0, 3 * sub:3 * sub + 1, :]
    scale = mod_ref[0, 3 * sub + 1:3 * sub + 2, :]
    gate = mod_ref[0, 3 * sub + 2:3 * sub + 3, :]

    if fused_mixer:
        o_ref[...] = x_ref[...] + mod_ref[0, 5:6, :] * _dot(z_ref[...], wo_ref[...])
    h_ref[...] = _adaln(res_ref[...], g_ref[...], shift, scale).astype(BF16)
    for c in range(dff // fc):
        h = h_ref[...]
        a = _dot(h, win_ref[:, c * fc:(c + 1) * fc])
        b = _dot(h, win_ref[:, dff + c * fc:dff + (c + 1) * fc])
        act = (_silu(a) * b).astype(BF16)
        part = _dot(act, wout_ref[c * fc:(c + 1) * fc, :])
        if c == 0:
            acc_ref[...] = part
        else:
            acc_ref[...] += part
    o_ref[...] = res_ref[...] + (MACARON_W * gate) * acc_ref[...]


def _ffn(x2, g, mod, w_in, w_out, *, sub, seq, mixer=None, tm=None, fc=256):
    t, d = x2.shape
    dff = w_out.shape[0]
    if tm is None:
        tm = 1024
    tm = min(tm, seq)
    tiles_per_seq = seq // tm
    resident = dict(pipeline_mode=pl.Buffered(1))
    args = [x2, g.reshape(1, d), mod, w_in, w_out]
    in_specs = [
        pl.BlockSpec((tm, d), lambda i: (i, 0)),
        pl.BlockSpec((1, d), lambda i: (0, 0)),
        pl.BlockSpec((1, N_SUB * 3, d), lambda i: (i // tiles_per_seq, 0, 0)),
        pl.BlockSpec((d, 2 * dff), lambda i: (0, 0), **resident),
        pl.BlockSpec((dff, d), lambda i: (0, 0), **resident),
    ]
    if mixer is not None:
        z2, w_o = mixer
        args += [z2, w_o]
        in_specs += [pl.BlockSpec((tm, z2.shape[1]), lambda i: (i, 0)),
                     pl.BlockSpec(w_o.shape, lambda i: (0, 0), **resident)]
    return pl.pallas_call(
        functools.partial(_ffn_kernel, sub=sub, fc=fc, fused_mixer=mixer is not None),
        grid=(t // tm,),
        in_specs=in_specs,
        out_specs=pl.BlockSpec((tm, d), lambda i: (i, 0)),
        out_shape=jax.ShapeDtypeStruct((t, d), F32),
        scratch_shapes=[pltpu.VMEM((tm, d), BF16), pltpu.VMEM((tm, d), F32)],
        compiler_params=_cparams(("parallel",)),
        name="macaron_ffn_mixer_out" if mixer is not None else "macaron_ffn",
    )(*args)


def _rwkv_pre_kernel(*refs, tiles_per_seq, has_vres):
    if has_vres:
        (x_ref, xp_ref, g_ref, mod_ref, mu_ref, wr_ref, wk_ref, wv_ref, w0_ref, w1_ref, w2_ref,
         a0_ref, a1_ref, a2_ref, g1_ref, g2_ref, v0_ref, v1_ref, v2_ref, vf_ref,
         r_ref, k_ref, v_ref, ld_ref, a_ref, gt_ref, h_ref, dx_ref) = refs
    else:
        (x_ref, xp_ref, g_ref, mod_ref, mu_ref, wr_ref, wk_ref, wv_ref, w0_ref, w1_ref, w2_ref,
         a0_ref, a1_ref, a2_ref, g1_ref, g2_ref,
         r_ref, k_ref, v_ref, ld_ref, a_ref, gt_ref, h_ref, dx_ref) = refs
    shift = mod_ref[0, 3:4, :]
    scale = mod_ref[0, 4:5, :]
    g = g_ref[...]
    h = _adaln(x_ref[...], g, shift, scale)
    first = (pl.program_id(0) % tiles_per_seq) == 0
    hp = _adaln(xp_ref[...], g, shift, scale)
    hp = jnp.where(first, 0.0, hp)
    h_ref[...] = h
    dx_ref[...] = _shift_rows(h, hp, 1) - h

    def chain(lo, hi, delay):
        for _ in range(delay):
            yield

        def mixed(j):
            return (h_ref[lo:hi, :] + dx_ref[lo:hi, :] * mu_ref[j:j + 1, :]).astype(BF16)

        w_mid = jnp.tanh(_dot(mixed(1), w1_ref[...])).astype(BF16)
        a_mid = _dot(mixed(4), a1_ref[...]).astype(BF16)
        yield
        wl = w0_ref[...] + _dot(w_mid, w2_ref[...])
        a_pre = a0_ref[...] + _dot(a_mid, a2_ref[...])
        g_mid = _sigmoid(_dot(mixed(5), g1_ref[...])).astype(BF16)
        yield
        r_ref[lo:hi, :] = _dot(mixed(0), wr_ref[...]).astype(r_ref.dtype)
        yield
        ld_ref[lo:hi, :] = (-DECAY_SCALE) * _sigmoid(wl)
        k_ref[lo:hi, :] = _dot(mixed(2), wk_ref[...]).astype(k_ref.dtype)
        yield
        a_ref[lo:hi, :] = _sigmoid(a_pre).astype(a_ref.dtype)
        gt_ref[lo:hi, :] = _dot(g_mid, g2_ref[...]).astype(gt_ref.dtype)
        xv = mixed(3)
        if has_vres:
            mixv = _sigmoid(v0_ref[...] + _dot(_dot(xv, v1_ref[...]).astype(BF16), v2_ref[...]))
        yield
        v = _dot(xv, wv_ref[...])
        if has_vres:
            v = v + (vf_ref[lo:hi, :].astype(F32) - v) * mixv
        v_ref[lo:hi, :] = v.astype(v_ref.dtype)

    tm = h_ref.shape[0]
    nsplit = 2 if tm % 32 == 0 else 1
    rows = tm // nsplit
    _run_interleaved([chain(i * rows, (i + 1) * rows, 2 * i) for i in range(nsplit)])


def _pad_cols(w, n):
    return jnp.pad(w, ((0, 0), (0, n - w.shape[1])))


def _pad_rows(w, n):
    return jnp.pad(w, ((0, n - w.shape[0]), (0, 0)))


def _round_up(n, m):
    return (n + m - 1) // m * m


def _lora_pair(w_a, w_b):
    n = _round_up(w_a.shape[1], LANE)
    return _pad_cols(w_a, n).astype(BF16), _pad_rows(w_b, n).astype(BF16)


def _rwkv_pre(x2, g, mod, mu, w_rkv, w0, w1, w2, a0, a1, a2, g1, g2, vres, v_first, *, seq, tm=512):
    t, d = x2.shape
    tm = min(tm, seq)
    tiles_per_seq = seq // tm
    has_vres = vres is not None
    const = lambda i: (0, 0)
    res = dict(pipeline_mode=pl.Buffered(1))
    row = lambda i: (i, 0)
    w1p, w2p = _lora_pair(w1, w2)
    a1p, a2p = _lora_pair(a1, a2)
    g1p, g2p = _lora_pair(g1, g2)
    args = [x2, x2, g.reshape(1, d), mod, _pad_rows(mu, 8),
            w_rkv[0].astype(BF16), w_rkv[1].astype(BF16), w_rkv[2].astype(BF16),
            w0.reshape(1, d), w1p, w2p, a0.reshape(1, d), a1p, a2p, g1p, g2p]
    in_specs = [
        pl.BlockSpec((tm, d), row),
        pl.BlockSpec((8, d), lambda i: (jnp.maximum(i * (tm // 8) - 1, 0), 0)),
        pl.BlockSpec((1, d), const),
        pl.BlockSpec((1, N_SUB * 3, d), lambda i: (i // tiles_per_seq, 0, 0)),
        pl.BlockSpec((8, d), const),
        pl.BlockSpec((d, d), const, **res), pl.BlockSpec((d, d), const, **res),
        pl.BlockSpec((d, d), const, **res),
        pl.BlockSpec((1, d), const),
        pl.BlockSpec(w1p.shape, const, **res), pl.BlockSpec(w2p.shape, const, **res),
        pl.BlockSpec((1, d), const),
        pl.BlockSpec(a1p.shape, const, **res), pl.BlockSpec(a2p.shape, const, **res),
        pl.BlockSpec(g1p.shape, const, **res), pl.BlockSpec(g2p.shape, const, **res),
    ]
    if has_vres:
        v0, v1, v2 = vres
        v1p, v2p = _lora_pair(v1, v2)
        args += [v0.reshape(1, d), v1p, v2p, v_first]
        in_specs += [pl.BlockSpec((1, d), const), pl.BlockSpec(v1p.shape, const, **res),
                     pl.BlockSpec(v2p.shape, const, **res), pl.BlockSpec((tm, d), row)]
    out_dtypes = (BF16, BF16, BF16, F32, BF16, BF16)
    return pl.pallas_call(
        functools.partial(_rwkv_pre_kernel, tiles_per_seq=tiles_per_seq, has_vres=has_vres),
        grid=(t // tm,),
        in_specs=in_specs,
        out_specs=[pl.BlockSpec((tm, d), row)] * 6,
        out_shape=[jax.ShapeDtypeStruct((t, d), dt_) for dt_ in out_dtypes],
        scratch_shapes=[pltpu.VMEM((tm, d), F32), pltpu.VMEM((tm, d), F32)],
        compiler_params=_cparams(("parallel",)),
        name="rwkv7_projections",
    )(*args)


def _rwkv_scan_kernel(*refs):
    st_ref = refs[-1]

    @pl.when(pl.program_id(2) == 0)
    def _():
        st_ref[...] = jnp.zeros_like(st_ref)

    nb = refs[0].shape[0]
    ng = st_ref.shape[0] // nb
    w = st_ref.shape[1]
    head_ones = jnp.where((_iota((w, w), 0) >> 6) == (_iota((w, w), 1) >> 6), 1.0, 0.0).astype(BF16)
    _run_interleaved([_rwkv_chain(bi, gi, *refs) for bi in range(nb) for gi in range(ng)],
                     shared_rhs=head_ones)


def _rwkv_chain(bi, gi, r_ref, k_ref, v_ref, ld_ref, a_ref, gt_ref, kk_ref, ka_ref, rk_ref,
                lnw_ref, lnb_ref, z_ref, st_ref):
    L = RW_CHUNK
    W = RW_GROUP * HEAD
    R = RW_GROUP * L
    lo, hi = gi * W, (gi + 1) * W
    si = bi * (st_ref.shape[0] // r_ref.shape[0]) + gi

    r = r_ref[bi, :, lo:hi].astype(F32)
    k = k_ref[bi, :, lo:hi].astype(F32)
    v = v_ref[bi, :, lo:hi].astype(F32)
    ld = ld_ref[bi, :, lo:hi]
    a = a_ref[bi, :, lo:hi].astype(F32)

    kkn = k * kk_ref[:, lo:hi]
    tril = jnp.where(_iota((L, L), 0) >= _iota((L, L), 1), 1.0, 0.0).astype(BF16)
    cum = _dot_exact_lhs(tril, ld)
    kk_ss = yield kkn * kkn, 1
    kk = kkn / jnp.maximum(jnp.sqrt(kk_ss), 1e-12)
    kmod = k * (1.0 + (a - 1.0) * ka_ref[:, lo:hi])
    c_end = cum[L - 1:L, :]
    e_pos = jnp.exp(cum)
    e_neg = jnp.exp(-cum)
    e_prev = jnp.exp(cum - ld)
    e_end = jnp.exp(c_end - cum)
    kka = kk * a

    rows = _iota((R, W), 0)
    cols = _iota((R, W), 1)
    same_head = (rows >> 6) == (cols >> 6)

    def bd(x):
        xb = x.astype(BF16)
        return jnp.where(same_head, jnp.concatenate([xb] * RW_GROUP, axis=0), jnp.zeros((), BF16))

    ar = jnp.concatenate([-kk * e_prev, r * e_pos], axis=0).astype(BF16)
    bk_s = jnp.concatenate([bd(kka * e_neg), bd(kmod * e_neg)], axis=0)
    v_s = bd(v)
    prod = _dot_nt(ar, bk_s)
    st = st_ref[si]
    from_state = _dot_nt(ar, st.astype(BF16))
    yield
    mt = _iota((L, R), 0)
    ms = _iota((L, R), 1) & (L - 1)
    strict = mt > ms
    incl = mt >= ms
    a_ab = jnp.where(strict, prod[0:L, 0:R], 0.0)
    a_ak = jnp.where(strict, prod[0:L, R:2 * R], 0.0)
    a_rb = jnp.where(incl, prod[L:2 * L, 0:R], 0.0)
    a_rk = jnp.where(incl, prod[L:2 * L, R:2 * R], 0.0)

    def mm(x, y):
        return _dot(x.astype(BF16), bd(y))

    eye = jnp.where(mt == ms, 1.0, 0.0)
    a_d = jnp.where((mt >> 3) == (ms >> 3), a_ab, 0.0)
    a_d2 = mm(a_d, a_d)
    rhs = from_state[0:L] + _dot(a_ak.astype(BF16), v_s)
    bonus = (yield r * kmod * rk_ref[:, lo:hi], 1) * v
    a_d4 = mm(a_d2, a_d2)
    tinv = mm(eye + a_d, eye + a_d2)
    yield
    tinv = mm(tinv, eye + a_d4)
    yield
    for sh in range(3, L.bit_length() - 1):
        off = ((mt >> (sh + 1)) == (ms >> (sh + 1))) & ((mt >> sh) != (ms >> sh))
        half = mm(tinv, jnp.where(off, a_ab, 0.0))
        yield
        tinv = tinv + mm(half, tinv)
        yield

    u = mm(tinv, rhs)
    yield
    y = from_state[L:2 * L] + _dot(
        jnp.concatenate([a_rb.astype(BF16), a_rk.astype(BF16)], axis=1),
        jnp.concatenate([bd(u), v_s], axis=0))
    new_terms = _dot_tn(jnp.concatenate([u, v], axis=0).astype(BF16),
                        jnp.concatenate([kka * e_end, kmod * e_end], axis=0).astype(BF16))
    st_ref[si] = st * jnp.exp(c_end) + jnp.where(same_head, new_terms, 0.0)
    mean = (yield y, 2) * (1.0 / HEAD)
    yc = y - mean
    var = (yield yc * yc, 1) * (1.0 / HEAD)
    yn = yc * lax.rsqrt(var + RW_GN_EPS) * lnw_ref[:, lo:hi] + lnb_ref[:, lo:hi]
    z_ref[bi, :, lo:hi] = ((yn + bonus) * gt_ref[bi, :, lo:hi].astype(F32)).astype(z_ref.dtype)


def _rwkv_scan(r, k, v, ld, a, gt, k_k, k_a, r_k, ln_w, ln_b, *, lanes=1024, nb=8):
    bsz, seq, d = r.shape
    assert RW_CHUNK == HEAD, "the side-by-side per-head matrix layout needs chunk length == head width"
    w = RW_GROUP * HEAD
    lanes = min(lanes, d)
    nb = min(nb, bsz)
    act = pl.BlockSpec((nb, RW_CHUNK, lanes), lambda b, g, c: (b, c, g))
    par = pl.BlockSpec((1, lanes), lambda b, g, c: (0, g))
    return pl.pallas_call(
        _rwkv_scan_kernel,
        grid=(bsz // nb, d // lanes, seq // RW_CHUNK),
        in_specs=[act] * 6 + [par] * 5,
        out_specs=act,
        out_shape=jax.ShapeDtypeStruct((bsz, seq, d), BF16),
        scratch_shapes=[pltpu.VMEM((nb * (lanes // w), w, w), F32)],
        compiler_params=_cparams(("parallel", "parallel", "arbitrary")),
        name="rwkv7_chunk_scan",
    )(r, k, v, ld, a, gt, k_k.reshape(1, d), k_a.reshape(1, d), r_k.reshape(1, d),
      ln_w.reshape(1, d), ln_b.reshape(1, d))


def _mamba_in_kernel(x_ref, g_ref, mod_ref, wz_ref, wx_ref, wb_ref, wc_ref, wdt_ref,
                     z_ref, xs_ref, b_ref, c_ref, dt_ref, h_ref):
    h_ref[...] = _adaln(x_ref[...], g_ref[...], mod_ref[0, 3:4, :], mod_ref[0, 4:5, :]).astype(BF16)
    z_ref[...] = _dot(h_ref[...], wz_ref[...]).astype(z_ref.dtype)
    xs_ref[...] = _dot(h_ref[...], wx_ref[...]).astype(xs_ref.dtype)
    b_ref[...] = _dot(h_ref[...], wb_ref[...]).astype(b_ref.dtype)
    c_ref[...] = _dot(h_ref[...], wc_ref[...]).astype(c_ref.dtype)
    dt_ref[...] = _dot(h_ref[...], wdt_ref[...])


def _mamba_in(x2, g, mod, w_in, *, seq, d_inner, tm=512):
    t, d = x2.shape
    gn = MB_GROUPS * MB_STATE
    tm = min(tm, seq)
    tiles_per_seq = seq // tm
    wz = w_in[:, :d_inner].astype(BF16)
    wx = w_in[:, d_inner:2 * d_inner].astype(BF16)
    wb = w_in[:, 2 * d_inner:2 * d_inner + gn].astype(BF16)
    wc = w_in[:, 2 * d_inner + gn:2 * d_inner + 2 * gn].astype(BF16)
    wdt = _pad_cols(w_in[:, 2 * d_inner + 2 * gn:], LANE).astype(BF16)
    const = lambda i: (0, 0)
    row = lambda i: (i, 0)
    res = dict(pipeline_mode=pl.Buffered(1))
    widths = (d_inner, d_inner, gn, gn, LANE)
    return pl.pallas_call(
        _mamba_in_kernel,
        grid=(t // tm,),
        in_specs=[pl.BlockSpec((tm, d), row), pl.BlockSpec((1, d), const),
                  pl.BlockSpec((1, N_SUB * 3, d), lambda i: (i // tiles_per_seq, 0, 0))]
        + [pl.BlockSpec((d, n), const, **res) for n in widths],
        out_specs=[pl.BlockSpec((tm, n), row) for n in widths],
        out_shape=[jax.ShapeDtypeStruct((t, n), dt_)
                   for n, dt_ in zip(widths, (BF16, BF16, BF16, BF16, F32))],
        scratch_shapes=[pltpu.VMEM((tm, d), BF16)],
        compiler_params=_cparams(("parallel",)),
        name="mamba2_in_proj",
    )(x2, g.reshape(1, d), mod, wz, wx, wb, wc, wdt)


def _ssd_kernel(xs_ref, b_ref, c_ref, z_ref, dt_ref, cwx_ref, cwb_ref, cwc_ref, cbx_ref, cbb_ref,
                cbc_ref, dtb_ref, alog_ref, dsk_ref, ng_ref, y_ref,
                st_ref, sx_ref, sb_ref, sc_ref):
    L = MB_CHUNK
    stages = ((sx_ref, xs_ref), (sb_ref, b_ref), (sc_ref, c_ref))

    @pl.when(pl.program_id(1) == 0)
    def _():
        st_ref[...] = jnp.zeros_like(st_ref)
        for s_ref, _ in stages:
            s_ref[0:8, :] = jnp.zeros((8, s_ref.shape[1]), F32)

    for s_ref, cur_ref in stages:
        s_ref[8:8 + L, :] = cur_ref[0].astype(F32)

    dt = _softplus(dt_ref[0] + dtb_ref[...])
    da = dt * (-jnp.exp(alog_ref[...]))
    tril = jnp.where(_iota((L, L), 0) >= _iota((L, L), 1), 1.0, 0.0).astype(BF16)
    acum = _dot_exact_lhs(tril, da) * LOG2E
    shared = (_split3(dt), _split3(acum))
    refs = (sx_ref, sb_ref, sc_ref, z_ref, cwx_ref, cwb_ref, cwc_ref, cbx_ref, cbb_ref, cbc_ref,
            dsk_ref, ng_ref, y_ref, st_ref)
    _run_interleaved([_ssd_chain(g, shared, *refs) for g in range(st_ref.shape[0])])

    for s_ref, _ in stages:
        s_ref[0:8, :] = s_ref[L:L + 8, :]


def _ssd_chain(g, shared, sx_ref, sb_ref, sc_ref, z_ref, cwx_ref, cwb_ref, cwc_ref, cbx_ref,
               cbb_ref, cbc_ref, dsk_ref, ng_ref, y_ref, st_ref):
    L = MB_CHUNK
    n = MB_STATE
    wx = st_ref.shape[2]
    hg = wx // HEAD
    dt3, ac3 = shared
    xl, xh = g * wx, (g + 1) * wx
    nl, nh = g * n, (g + 1) * n

    def conv_silu(s_ref, w_ref, bias_ref, lo, hi):
        staged = s_ref[:, lo:hi]
        prev = pltpu.roll(staged, 1, 0)
        w = [w_ref[j:j + 1, lo:hi] for j in range(MB_CONV)]
        older = pltpu.roll(staged * w[1] + prev * w[0], 2, 0)
        acc = bias_ref[:, lo:hi] + staged * w[3] + prev * w[2] + older
        return _silu(acc[8:8 + L])

    xs = conv_silu(sx_ref, cwx_ref, cbx_ref, xl, xh)
    bm = conv_silu(sb_ref, cwb_ref, cbb_ref, nl, nh).astype(BF16)
    cm = conv_silu(sc_ref, cwc_ref, cbc_ref, nl, nh).astype(BF16)

    head0 = hg * g
    sel_x = jnp.where(_iota((LANE, wx), 0) == head0 + (_iota((LANE, wx), 1) >> 6), 1.0, 0.0).astype(BF16)
    sel_r = jnp.where(_iota((8, LANE), 1) == head0 + _iota((8, LANE), 0), 1.0, 0.0).astype(BF16)
    sel_c = jnp.where(_iota((LANE, hg * L), 0) == head0 + (_iota((LANE, hg * L), 1) >> 7), 1.0, 0.0).astype(BF16)
    both = [jnp.concatenate([d_p, a_p], axis=0) for d_p, a_p in zip(dt3, ac3)]
    in_x = _dot(both[0], sel_x) + _dot(both[1], sel_x) + _dot(both[2], sel_x)
    dt_x, ac_x = in_x[0:L], in_x[L:2 * L]
    ac_row = _dot_nt(sel_r, ac3[0]) + _dot_nt(sel_r, ac3[1]) + _dot_nt(sel_r, ac3[2])
    ac_col = _dot(ac3[0], sel_c) + _dot(ac3[1], sel_c) + _dot(ac3[2], sel_c)
    cb = _dot_nt(cm, bm)
    st = st_ref[g]
    y_off = _dot(cm, st.astype(BF16))
    yield

    xdt = xs * dt_x
    causal = _iota((L, L), 0) >= _iota((L, L), 1)
    lane_head = _iota((L, wx), 1) >> 6
    xdt_b = xdt.astype(BF16)
    g_parts = []
    x_parts = []
    for j in range(hg):
        seg = jnp.where(causal, ac_col[:, j * L:(j + 1) * L] - ac_row[j:j + 1, :], -jnp.inf)
        g_parts.append((cb * jnp.exp2(seg)).astype(BF16))
        x_parts.append(jnp.where(lane_head == j, xdt_b, jnp.zeros((), BF16)))
    y_diag = _dot(jnp.concatenate(g_parts, axis=1), jnp.concatenate(x_parts, axis=0))
    ac_last = ac_x[L - 1:L, :]
    xdec = (xdt * jnp.exp2(ac_last - ac_x)).astype(BF16)
    st_ref[g] = st * jnp.exp2(ac_last) + _dot_tn(bm, xdec)
    yield

    y = y_diag + y_off * jnp.exp2(ac_x) + xs * dsk_ref[:, xl:xh]
    y = y * _silu(z_ref[0, :, xl:xh].astype(F32))
    ms = jnp.mean(y * y, axis=-1, keepdims=True)
    y_ref[0, :, xl:xh] = (y * lax.rsqrt(ms + NORM_EPS) * ng_ref[:, xl:xh]).astype(y_ref.dtype)


def _ssd(xs, bm, cm, z, dt, conv_w, conv_b, dt_bias, a_log, d_skip, norm_g):
    bsz, seq, d_inner = xs.shape
    n = MB_STATE
    gn = MB_GROUPS * n
    wx = d_inner // MB_GROUPS
    heads = d_inner // HEAD
    assert conv_w.shape[0] == MB_CONV == 4, "the in-kernel conv is written for 4 taps"
    cw = _pad_rows(conv_w, 8)
    cwx, cwb, cwc = cw[:, :d_inner], cw[:, d_inner:d_inner + gn], cw[:, d_inner + gn:]
    cb = conv_b.reshape(1, -1)
    cbx, cbb, cbc = cb[:, :d_inner], cb[:, d_inner:d_inner + gn], cb[:, d_inner + gn:]
    dtb = _pad_cols(dt_bias.reshape(1, heads), LANE)
    alog = _pad_cols(a_log.reshape(1, heads), LANE)
    dsk = jnp.repeat(d_skip, HEAD).reshape(1, d_inner)
    tok = lambda w: pl.BlockSpec((1, MB_CHUNK, w), lambda b, c: (b, c, 0))
    par = lambda r, w: pl.BlockSpec((r, w), lambda b, c: (0, 0))
    return pl.pallas_call(
        _ssd_kernel,
        grid=(bsz, seq // MB_CHUNK),
        in_specs=[tok(d_inner), tok(gn), tok(gn), tok(d_inner), tok(LANE),
                  par(8, d_inner), par(8, gn), par(8, gn), par(1, d_inner), par(1, gn), par(1, gn),
                  par(1, LANE), par(1, LANE), par(1, d_inner), par(1, d_inner)],
        out_specs=tok(d_inner),
        out_shape=jax.ShapeDtypeStruct((bsz, seq, d_inner), BF16),
        scratch_shapes=[pltpu.VMEM((MB_GROUPS, n, wx), F32),
                        pltpu.VMEM((8 + MB_CHUNK,d_inner), F32),
                        pltpu.VMEM((8 + MB_CHUNK,gn), F32),
                        pltpu.VMEM((8 + MB_CHUNK,gn), F32)],
        compiler_params=_cparams(("parallel", "arbitrary")),
        name="mamba2_conv_ssd",
    )(xs, bm, cm, z, dt, cwx, cwb, cwc, cbx, cbb, cbc, dtb, alog, dsk, norm_g.reshape(1, d_inner))


def _swa_qkv_kernel(x_ref, g_ref, mod_ref, wq_ref, wk_ref, wv_ref, q_ref, k_ref, v_ref, h_ref):
    h_ref[...] = _adaln(x_ref[...], g_ref[...], mod_ref[0, 3:4, :], mod_ref[0, 4:5, :]).astype(BF16)
    q_ref[...] = _dot(h_ref[...], wq_ref[...]).astype(q_ref.dtype)
    k_ref[...] = _dot(h_ref[...], wk_ref[...]).astype(k_ref.dtype)
    v_ref[...] = _dot(h_ref[...], wv_ref[...]).astype(v_ref.dtype)


def _swa_qkv(x2, g, mod, w_qkv, *, seq, tm=512):
    t, d = x2.shape
    nq = d
    nk = SW_KV_HEADS * HEAD
    tm = min(tm, seq)
    tiles_per_seq = seq // tm
    wq = w_qkv[:, :nq].astype(BF16)
    wk = w_qkv[:, nq:nq + nk].astype(BF16)
    wv = w_qkv[:, nq + nk:].astype(BF16)
    const = lambda i: (0, 0)
    row = lambda i: (i, 0)
    res = dict(pipeline_mode=pl.Buffered(1))
    widths = (nq, nk, nk)
    return pl.pallas_call(
        _swa_qkv_kernel,
        grid=(t // tm,),
        in_specs=[pl.BlockSpec((tm, d), row), pl.BlockSpec((1, d), const),
                  pl.BlockSpec((1, N_SUB * 3, d), lambda i: (i // tiles_per_seq, 0, 0))]
        + [pl.BlockSpec((d, n), const, **res) for n in widths],
        out_specs=[pl.BlockSpec((tm, n), row) for n in widths],
        out_shape=[jax.ShapeDtypeStruct((t, n), BF16) for n in widths],
        scratch_shapes=[pltpu.VMEM((tm, d), BF16)],
        compiler_params=_cparams(("parallel",)),
        name="swa_qkv_proj",
    )(x2, g.reshape(1, d), mod, wq, wk, wv)


def _swa_kernel(q_ref, kc_ref, kp_ref, vc_ref, vp_ref, qn_ref, kn_ref, sink_ref, o_ref):
    T = SW_BLOCK
    wk = kc_ref.shape[2]
    first_key = jnp.where(pl.program_id(1) > 0, 0, T)

    def head_norm(x, g):
        w = x.shape[1]
        head_ones = jnp.where((_iota((w, w), 0) >> 6) == (_iota((w, w), 1) >> 6), 1.0, 0.0).astype(BF16)
        ms = _dot((x * x).astype(BF16), head_ones) * (1.0 / HEAD)
        return x * lax.rsqrt(ms + NORM_EPS) * g

    k_all = jnp.concatenate([kp_ref[0], kc_ref[0]], axis=0).astype(F32)
    k_all = head_norm(k_all, kn_ref[...]).astype(BF16)
    v_all = jnp.concatenate([vp_ref[0], vc_ref[0]], axis=0).astype(BF16)

    qi = _iota((T, 2 * T), 0)
    si = _iota((T, 2 * T), 1)
    rel = qi + T - si
    band = (rel >= 0) & (rel < T)
    chains = []
    for blk in range(q_ref.shape[1] // T):
        mask = band & (si >= first_key) if blk == 0 else band
        shared = (k_all[blk * T:(blk + 2) * T], v_all[blk * T:(blk + 2) * T], mask, si == qi, head_norm)
        chains += [_swa_chain(blk, kv, shared, q_ref, qn_ref, sink_ref, o_ref) for kv in range(wk // HEAD)]
    _run_interleaved(chains)


def _swa_chain(blk, kv, shared, q_ref, qn_ref, sink_ref, o_ref):
    T = SW_BLOCK
    kcat, vcat, mask, sink_slot, head_norm = shared
    r0, r1 = blk * T, (blk + 1) * T
    wk = kcat.shape[1]
    gq = q_ref.shape[2] // wk
    wq = gq * HEAD
    scale = HEAD ** -0.5 * LOG2E
    lo, hi = kv * wq, (kv + 1) * wq

    rep = jnp.where(_iota((wk, wq), 0) == kv * HEAD + (_iota((wk, wq), 1) & (HEAD - 1)), 1.0, 0.0).astype(BF16)
    k_rep = _dot(kcat, rep).astype(BF16)
    v_rep = _dot(vcat, rep).astype(BF16)
    q = head_norm(q_ref[0, r0:r1, lo:hi].astype(F32), qn_ref[...]).astype(BF16)
    yield
    lane_head = _iota((T, wq), 1) >> 6
    scores = []
    for j in range(gq):
        q_j = jnp.where(lane_head == j, q, jnp.zeros((), BF16))
        scores.append(_dot_nt(q_j, k_rep) * scale)
        yield
    ones = jnp.ones((2 * T, wq), BF16)
    probs = []
    for j in range(gq):
        sink = sink_ref[0:1, kv * gq + j:kv * gq + j + 1] * LOG2E
        s = jnp.where(sink_slot, sink, jnp.where(mask, scores[j], NEG_INF))
        m = jnp.max(s, axis=-1, keepdims=True)
        yield
        probs.append(jnp.exp2(s - m).astype(BF16))
    o = jnp.zeros((T, wq), F32)
    for j in range(gq):
        denom = _dot(probs[j], ones)
        o_j = _dot(jnp.where(sink_slot, jnp.zeros((), BF16), probs[j]), v_rep)
        yield
        o = o + jnp.where(lane_head == j, o_j / denom, 0.0)
    o_ref[0, r0:r1, lo:hi] = o.astype(o_ref.dtype)


def _swa(q, k, v, q_norm, k_norm, sinks, *, nblk=4):
    bsz, seq, dq = q.shape
    dk = k.shape[2]
    nq = dq // HEAD
    nblk = min(nblk, seq // SW_BLOCK)
    rows = nblk * SW_BLOCK
    cur = lambda b, i: (b, i, 0)
    prev = lambda b, i: (b, jnp.maximum(i * nblk - 1, 0), 0)
    const = lambda b, i: (0, 0)
    qn = jnp.tile(q_norm, dq // SW_KV_HEADS // HEAD).reshape(1, -1)
    kn = jnp.tile(k_norm, dk // HEAD).reshape(1, dk)
    return pl.pallas_call(
        _swa_kernel,
        grid=(bsz, seq // rows),
        in_specs=[pl.BlockSpec((1, rows, dq), cur),
                  pl.BlockSpec((1, rows, dk), cur), pl.BlockSpec((1, SW_BLOCK, dk), prev),
                  pl.BlockSpec((1, rows, dk), cur), pl.BlockSpec((1, SW_BLOCK, dk), prev),
                  pl.BlockSpec(qn.shape, const), pl.BlockSpec((1, dk), const),
                  pl.BlockSpec((1, nq), const)],
        out_specs=pl.BlockSpec((1, rows, dq), cur),
        out_shape=jax.ShapeDtypeStruct((bsz, seq, dq), BF16),
        compiler_params=_cparams(("parallel", "parallel")),
        name="swa_sink_attention",
    )(q, k, k, v, v, qn, kn, sinks.reshape(1, nq))


def kernel(x, c, ada_w, ada_b, norm_g, ffn_w_in, ffn_w_out, rw_mu, rw_w_rkv, rw_w_o, rw_w0, rw_w1, rw_w2, rw_a0, rw_a1, rw_a2, rw_g1, rw_g2, rw_k_k, rw_k_a, rw_r_k, rw_ln_w, rw_ln_b, rw_v0, rw_v1, rw_v2, mb_w_in, mb_conv_w, mb_conv_b, mb_dt_bias, mb_A_log, mb_D, mb_norm_g, mb_w_out, sw_w_qkv, sw_q_norm, sw_k_norm, sw_sinks, sw_w_o):
    bsz, seq, d = x.shape
    depth = ada_w.shape[0]
    t = bsz * seq
    mods = _modulation(c, ada_w, ada_b)
    x2 = x.reshape(t, d)
    v_first = None
    for i in range(depth):
        mod = mods[i]
        x2 = _ffn(x2, norm_g[i, 0], mod, ffn_w_in[i, 0].astype(BF16), ffn_w_out[i, 0].astype(BF16),
                  sub=0, seq=seq)
        kind, j = i % 3, i // 3
        if kind == 0:
            vres = None if v_first is None else (rw_v0[j - 1], rw_v1[j - 1], rw_v2[j - 1])
            r, k, v, ld, a, gt = _rwkv_pre(
                x2, norm_g[i, 1], mod, rw_mu[j], rw_w_rkv[j], rw_w0[j], rw_w1[j], rw_w2[j],
                rw_a0[j], rw_a1[j], rw_a2[j], rw_g1[j], rw_g2[j], vres, v_first, seq=seq)
            if v_first is None:
                v_first = v
            sh = (bsz, seq, d)
            z = _rwkv_scan(r.reshape(sh), k.reshape(sh), v.reshape(sh), ld.reshape(sh), a.reshape(sh),
                           gt.reshape(sh), rw_k_k[j], rw_k_a[j], rw_r_k[j], rw_ln_w[j], rw_ln_b[j])
            mixer = (z.reshape(t, d), rw_w_o[j].astype(BF16))
        elif kind == 1:
            d_inner = mb_w_out.shape[1]
            z, xs, bm, cm, dt = _mamba_in(x2, norm_g[i, 1], mod, mb_w_in[j], seq=seq, d_inner=d_inner)
            y = _ssd(xs.reshape(bsz, seq, -1), bm.reshape(bsz, seq, -1), cm.reshape(bsz, seq, -1),
                     z.reshape(bsz, seq, -1), dt.reshape(bsz, seq, -1), mb_conv_w[j], mb_conv_b[j],
                     mb_dt_bias[j], mb_A_log[j], mb_D[j], mb_norm_g[j])
            mixer = (y.reshape(t, d_inner), mb_w_out[j].astype(BF16))
        else:
            q, k, v = _swa_qkv(x2, norm_g[i, 1], mod, sw_w_qkv[j], seq=seq)
            o = _swa(q.reshape(bsz, seq, -1), k.reshape(bsz, seq, -1), v.reshape(bsz, seq, -1),
                     sw_q_norm[j], sw_k_norm[j], sw_sinks[j])
            mixer = (o.reshape(t, -1), sw_w_o[j].astype(BF16))
        x2 = _ffn(x2, norm_g[i, 2], mod, ffn_w_in[i, 1].astype(BF16), ffn_w_out[i, 1].astype(BF16),
                  sub=2, seq=seq, mixer=mixer)
    return x2.reshape(bsz, seq, d)
```
